```python
import math
import jax, jax.numpy as jnp
from jax import lax
import numpy as np

D_MODEL = 1024
BATCH = 2
SEQ = 8192
DEPTH = 1

N_META = 16
BLOCK = 128
D_FF = 2816
EPS = 1e-6
NEG_INF = -1e30
MLA_HEADS = 8
MLA_Q_RANK = 256
MLA_KV_RANK = 128
MLA_NOPE_DIM = 64
MLA_ROPE_DIM = 32
MLA_QK_DIM = MLA_NOPE_DIM + MLA_ROPE_DIM
MLA_V_DIM = 64
ROPE_THETA = 10000.0
FOX_HEADS = 8
FOX_HEAD_DIM = 64
D_MIX = MLA_HEADS * MLA_V_DIM + FOX_HEADS * FOX_HEAD_DIM
IN_SPLITS = (MLA_Q_RANK, MLA_KV_RANK, MLA_ROPE_DIM, 3 * FOX_HEADS * FOX_HEAD_DIM, FOX_HEADS)
D_IN = sum(IN_SPLITS)

kernel_name = "hymba_mla_fox_macaron"


def rms_norm(x, g):
    xf = x.astype(jnp.float32)
    y = xf * lax.rsqrt(jnp.mean(xf * xf, axis=-1, keepdims=True) + EPS)
    return (y * g.astype(jnp.float32)).astype(x.dtype)


def swiglu(x, w_gate, w_up, w_down):
    return (jax.nn.silu(x @ w_gate) * (x @ w_up)) @ w_down


def rope(x, pos):
    half = x.shape[-1] // 2
    inv_freq = 1.0 / (ROPE_THETA ** (jnp.arange(half, dtype=jnp.float32) / half))
    ang = pos.astype(jnp.float32)[:, None] * inv_freq[None, :]
    cos = jnp.cos(ang)[None, :, None, :].astype(x.dtype)
    sin = jnp.sin(ang)[None, :, None, :].astype(x.dtype)
    x1, x2 = x[..., :half], x[..., half:]
    return jnp.concatenate([x1 * cos - x2 * sin, x2 * cos + x1 * sin], axis=-1)


def block_causal_attention(q, k, v, scale, key_valid, cum=None):
    L = q.shape[1]
    outs = []
    for i in range(L // BLOCK):
        q0, q1 = i * BLOCK, (i + 1) * BLOCK
        s = jnp.einsum('bqhd,bkhd->bhqk', q[:, q0:q1], k[:, :q1],
                       preferred_element_type=jnp.float32) * scale
        if cum is not None:
            cq = jnp.transpose(cum[:, q0:q1], (0, 2, 1))[:, :, :, None]
            ck = jnp.transpose(cum[:, :q1], (0, 2, 1))[:, :, None, :]
            s = s + (cq - ck)
        q_pos = q0 + jnp.arange(BLOCK)
        k_pos = jnp.arange(q1)
        mask = (k_pos[None, :] <= q_pos[:, None]) & key_valid[None, :q1]
        s = jnp.where(mask[None, None], s, NEG_INF)
        p = jax.nn.softmax(s, axis=-1).astype(v.dtype)
        outs.append(jnp.einsum('bhqk,bkhd->bqhd', p, v[:, :q1]))
    return jnp.concatenate(outs, axis=1)


def hybrid_mixer(u, pos, key_valid, w_in, g_cq, w_uq, g_ckv, w_ukv, g_q_mla, g_k_mla,
                 b_forget, g_q_fox, g_k_fox, w_out):
    B, L, _ = u.shape
    proj = u @ w_in
    offs = [int(o) for o in np.cumsum(IN_SPLITS)[:-1]]
    c_q, c_kv, k_pe, fox_qkv, f_logit = jnp.split(proj, offs, axis=-1)

    q = (rms_norm(c_q, g_cq) @ w_uq).reshape(B, L, MLA_HEADS, MLA_QK_DIM)
    q = rms_norm(q, g_q_mla)
    q = jnp.concatenate([q[..., :MLA_NOPE_DIM], rope(q[..., MLA_NOPE_DIM:], pos)], axis=-1)
    kv = (rms_norm(c_kv, g_ckv) @ w_ukv).reshape(B, L, MLA_HEADS, MLA_NOPE_DIM + MLA_V_DIM)
    k_nope, v = kv[..., :MLA_NOPE_DIM], kv[..., MLA_NOPE_DIM:]
    k_pe_b = jnp.broadcast_to(k_pe[:, :, None, :], (B, L, MLA_HEADS, MLA_ROPE_DIM))
    k = rms_norm(jnp.concatenate([k_nope, k_pe_b], axis=-1), g_k_mla)
    k = jnp.concatenate([k[..., :MLA_NOPE_DIM], rope(k[..., MLA_NOPE_DIM:], pos)], axis=-1)
    o_mla = block_causal_attention(q, k, v, 1.0 / math.sqrt(MLA_QK_DIM), key_valid)

    fq, fk, fv = jnp.split(fox_qkv.reshape(B, L, 3, FOX_HEADS, FOX_HEAD_DIM), 3, axis=2)
    fq = rms_norm(fq[:, :, 0], g_q_fox)
    fk = rms_norm(fk[:, :, 0], g_k_fox)
    fv = fv[:, :, 0]
    log_f = jax.nn.log_sigmoid(f_logit.astype(jnp.float32) + b_forget.astype(jnp.float32))
    cum = jnp.cumsum(log_f, axis=1)
    o_fox = block_causal_attention(fq, fk, fv, 1.0 / math.sqrt(FOX_HEAD_DIM), key_valid, cum)

    o = jnp.concatenate([o_mla.reshape(B, L, MLA_HEADS * MLA_V_DIM),
                         o_fox.reshape(B, L, FOX_HEADS * FOX_HEAD_DIM)], axis=-1)
    return o @ w_out


def setup_inputs(seed: int = 0) -> dict:
    key = jax.random.key(seed)
    ks = jax.random.split(key, 24)
    f32 = jnp.float32

    def nrm(k, shape, fan_in):
        return jax.random.normal(k, shape, f32) * (fan_in ** -0.5)

    def gain(k, n):
        return 1.0 + 0.02 * jax.random.normal(k, (DEPTH, n), f32)

    return {
        "x": jax.random.normal(ks[0], (BATCH, SEQ, D_MODEL), f32),
        "meta_tokens": jax.random.normal(ks[1], (N_META, D_MODEL), f32),
        "g_ffn1": gain(ks[2], D_MODEL),
        "w1_gate": nrm(ks[3], (DEPTH, D_MODEL, D_FF), D_MODEL),
        "w1_up": nrm(ks[4], (DEPTH, D_MODEL, D_FF), D_MODEL),
        "w1_down": nrm(ks[5], (DEPTH, D_FF, D_MODEL), D_FF),
        "g_mix": gain(ks[6], D_MODEL),
        "w_in": nrm(ks[7], (DEPTH, D_MODEL, D_IN), D_MODEL),
        "g_cq": gain(ks[8], MLA_Q_RANK),
        "w_uq": nrm(ks[9], (DEPTH, MLA_Q_RANK, MLA_HEADS * MLA_QK_DIM), MLA_Q_RANK),
        "g_ckv": gain(ks[10], MLA_KV_RANK),
        "w_ukv": nrm(ks[11], (DEPTH, MLA_KV_RANK, MLA_HEADS * (MLA_NOPE_DIM + MLA_V_DIM)), MLA_KV_RANK),
        "g_q_mla": gain(ks[12], MLA_QK_DIM),
        "g_k_mla": gain(ks[13], MLA_QK_DIM),
        "b_forget": jax.random.uniform(ks[14], (DEPTH, FOX_HEADS), f32, 1.0, 4.0),
        "g_q_fox": gain(ks[15], FOX_HEAD_DIM),
        "g_k_fox": gain(ks[16], FOX_HEAD_DIM),
        "w_out": nrm(ks[17], (DEPTH, D_MIX, D_MODEL), D_MIX),
        "g_ffn2": gain(ks[18], D_MODEL),
        "w2_gate": nrm(ks[19], (DEPTH, D_MODEL, D_FF), D_MODEL),
        "w2_up": nrm(ks[20], (DEPTH, D_MODEL, D_FF), D_MODEL),
        "w2_down": nrm(ks[21], (DEPTH, D_FF, D_MODEL), D_FF),
    }


def reference(x, meta_tokens, g_ffn1, w1_gate, w1_up, w1_down, g_mix, w_in, g_cq, w_uq,
              g_ckv, w_ukv, g_q_mla, g_k_mla, b_forget, g_q_fox, g_k_fox, w_out,
              g_ffn2, w2_gate, w2_up, w2_down):
    B = x.shape[0]
    pad = BLOCK - N_META
    meta = jnp.broadcast_to(meta_tokens.astype(x.dtype)[None], (B, N_META, D_MODEL))
    h = jnp.concatenate([jnp.zeros((B, pad, D_MODEL), x.dtype), meta, x], axis=1)
    L = h.shape[1]
    idx = jnp.arange(L)
    key_valid = idx >= pad
    pos = jnp.maximum(idx - pad, 0).astype(jnp.int32)
    for l in range(DEPTH):
        h = h + 0.5 * swiglu(rms_norm(h, g_ffn1[l]), w1_gate[l], w1_up[l], w1_down[l])
        h = h + hybrid_mixer(rms_norm(h, g_mix[l]), pos, key_valid, w_in[l], g_cq[l], w_uq[l],
                             g_ckv[l], w_ukv[l], g_q_mla[l], g_k_mla[l], b_forget[l],
                             g_q_fox[l], g_k_fox[l], w_out[l])
        h = h + 0.5 * swiglu(rms_norm(h, g_ffn2[l]), w2_gate[l], w2_up[l], w2_down[l])
    return h[:, BLOCK:]
```

```python
import functools
import math

import jax
import jax.numpy as jnp
import numpy as np
from jax import lax
from jax.experimental import pallas as pl
from jax.experimental.pallas import tpu as pltpu

F32 = jnp.float32
BF16 = jnp.bfloat16

D_MODEL = 1024
D_FF = 2816
N_META = 16
BLOCK = 128
EPS = 1e-6
NEG_INF = -1e30
MLA_HEADS = 8
MLA_Q_RANK = 256
MLA_KV_RANK = 128
MLA_NOPE = 64
MLA_ROPE = 32
MLA_QK = MLA_NOPE + MLA_ROPE
MLA_V = 64
ROPE_THETA = 10000.0
FOX_HEADS = 8
FOX_DIM = 64
N_HEADS = MLA_HEADS + FOX_HEADS
LANES = 128
HALF = LANES // 2
N_SPLIT = 3

FF_CHUNK = 256
FFN_TM = 512
PROJ_TM = 512
ATT_TQ = 512
ATT_TK = 512
VMEM_LIMIT = 56 * 1024 * 1024

_O_CQ = 0
_O_CKV = _O_CQ + MLA_Q_RANK
_O_KPE = _O_CKV + MLA_KV_RANK
_O_KPE_SW = _O_KPE + LANES
_O_FQ = _O_KPE_SW + LANES
_O_FK = _O_FQ + FOX_HEADS * FOX_DIM
_O_FV = _O_FK + FOX_HEADS * FOX_DIM
_O_FL = _O_FV + FOX_HEADS * FOX_DIM
_D_IN_PAD = _O_FL + LANES


def _const_spec(shape):
    nd = len(shape)
    return pl.BlockSpec(shape, lambda *_: (0,) * nd, pipeline_mode=pl.Buffered(1))


def _rms_scale(x, width):
    return lax.rsqrt(jnp.sum(x * x, axis=-1, keepdims=True) * (1.0 / width) + EPS)


def _split3(x):
    pieces = []
    r = x
    for _ in range(N_SPLIT):
        p = r.astype(BF16)
        pieces.append(p)
        r = r - p.astype(F32)
    return pieces


def _ffn_body(h, g_ref, wg_ref, wu_ref, wd_ref):
    n = (h * _rms_scale(h, D_MODEL) * g_ref[...]).astype(BF16)
    acc = jnp.zeros(h.shape, F32)
    for c in range(D_FF // FF_CHUNK):
        sl = slice(c * FF_CHUNK, (c + 1) * FF_CHUNK)
        g = jnp.dot(n, wg_ref[:, sl], preferred_element_type=F32)
        u = jnp.dot(n, wu_ref[:, sl], preferred_element_type=F32)
        a = (g * (1.0 / (1.0 + jnp.exp(-g))) * u).astype(BF16)
        acc = acc + jnp.dot(a, wd_ref[sl, :], preferred_element_type=F32)
    return h + 0.5 * acc


def _ffn_kernel(h_ref, g_ref, wg_ref, wu_ref, wd_ref, o_ref):
    o_ref[...] = _ffn_body(h_ref[...], g_ref, wg_ref, wu_ref, wd_ref)


def _mix_ffn_kernel(h_ref, a_ref, wo_ref, g_ref, wg_ref, wu_ref, wd_ref, o_ref):
    h = h_ref[...] + jnp.dot(a_ref[...], wo_ref[...], preferred_element_type=F32)
    o_ref[...] = _ffn_body(h, g_ref, wg_ref, wu_ref, wd_ref)


def _ffn(h, g, wg, wu, wd, attn=None, w_out=None):
    m = h.shape[0]
    tm = min(FFN_TM, m)
    row = pl.BlockSpec((tm, D_MODEL), lambda i: (i, 0))
    weights = [_const_spec((1, D_MODEL)), _const_spec((D_MODEL, D_FF)),
               _const_spec((D_MODEL, D_FF)), _const_spec((D_FF, D_MODEL))]
    if attn is None:
        kern, in_specs, args = _ffn_kernel, [row] + weights, (h, g, wg, wu, wd)
    else:
        kern = _mix_ffn_kernel
        in_specs = [row, row, _const_spec((D_MODEL, D_MODEL))] + weights
        args = (h, attn, w_out, g, wg, wu, wd)
    return pl.pallas_call(
        kern,
        grid=(m // tm,),
        in_specs=in_specs,
        out_specs=row,
        out_shape=jax.ShapeDtypeStruct((m, D_MODEL), F32),
        compiler_params=pltpu.CompilerParams(
            dimension_semantics=("arbitrary",), vmem_limit_bytes=VMEM_LIMIT),
        name="ffn_mix" if attn is not None else "ffn",
    )(*args)


def _proj_kernel(h_ref, gmix_ref, win_ref, gcq_ref, wuq_ref, gckv_ref, wukv_ref,
                 gq_ref, gqs_ref, gk_ref, gks_ref, cc_ref, ss_ref, bf_ref, gfq_ref, gfk_ref,
                 tri_ref, place_ref, *rest, tm, is_meta):
    if is_meta:
        k_out, v_out = rest
        q_out = carry_sc = None
    else:
        q_out, k_out, v_out, carry_sc = rest

    lane = lax.broadcasted_iota(jnp.int32, (tm, LANES), 1)
    lo = lane < HALF
    bias_e = ((lane >= HALF) & (lane < HALF + N_SPLIT)).astype(F32)
    bias_o = (lane < N_SPLIT).astype(F32)
    one_e = (lane == HALF).astype(F32)
    one_o = (lane == 0).astype(F32)

    h = h_ref[0]
    n = (h * _rms_scale(h, D_MODEL) * gmix_ref[...]).astype(BF16)
    proj = jnp.dot(n, win_ref[...], preferred_element_type=F32)

    c_q = proj[:, _O_CQ:_O_CKV]
    c_kv = proj[:, _O_CKV:_O_KPE]
    kpe = proj[:, _O_KPE:_O_KPE_SW]
    kpe_sw = proj[:, _O_KPE_SW:_O_FQ]

    cc = cc_ref[...]
    ss = ss_ref[...]

    ckvn = (c_kv * _rms_scale(c_kv, MLA_KV_RANK) * gckv_ref[...]).astype(BF16)
    kv = jnp.dot(ckvn, wukv_ref[...], preferred_element_type=F32)
    kc = kpe_sw * gks_ref[...] * ss
    for hd in range(MLA_HEADS):
        x = kv[:, hd * LANES:(hd + 1) * LANES] + kpe
        r = _rms_scale(x, MLA_QK)
        k_out[0, hd] = (r * (x * gk_ref[...] * cc + kc)).astype(BF16)
    if not is_meta:
        cqn = (c_q * _rms_scale(c_q, MLA_Q_RANK) * gcq_ref[...]).astype(BF16)
        qq = jnp.dot(cqn, wuq_ref[...], preferred_element_type=F32)
        for hd in range(MLA_HEADS):
            x = qq[:, hd * LANES:(hd + 1) * LANES]
            xs = qq[:, (MLA_HEADS + hd) * LANES:(MLA_HEADS + hd + 1) * LANES]
            r = _rms_scale(x, MLA_QK)
            q_out[0, hd] = (r * (x * gq_ref[...] * cc + xs * gqs_ref[...] * ss)).astype(BF16)
    v_base = MLA_HEADS * LANES
    for p in range(MLA_HEADS // 2):
        blk = kv[:, v_base + p * LANES:v_base + (p + 1) * LANES]
        v_out[0, 2 * p] = jnp.where(lo, blk, one_e).astype(BF16)
        v_out[0, 2 * p + 1] = jnp.where(lo, one_o, blk).astype(BF16)

    fl = proj[:, _O_FL:_O_FL + LANES] + bf_ref[...]
    log_f = jnp.minimum(fl, 0.0) - jnp.log1p(jnp.exp(-jnp.abs(fl)))
    tri = tri_ref[...]
    cum = jnp.zeros((tm, LANES), F32)
    for piece in _split3(log_f):
        cum = cum + jnp.dot(tri, piece, preferred_element_type=F32)
    last = cum[tm - 1:tm, :]
    if is_meta:
        cum = cum - last
    else:
        @pl.when(pl.program_id(1) == 0)
        def _():
            carry_sc[...] = jnp.zeros_like(carry_sc)
        cum = cum + carry_sc[0:1, :]
        carry_sc[...] = jnp.broadcast_to(cum[tm - 1:tm, :], carry_sc.shape)
    bias = jnp.dot(jnp.concatenate(_split3(-cum), axis=1), place_ref[...],
                   preferred_element_type=F32)
    for p in range(FOX_HEADS // 2):
        he, ho = MLA_HEADS + 2 * p, MLA_HEADS + 2 * p + 1
        x = proj[:, _O_FK + p * LANES:_O_FK + (p + 1) * LANES]
        x2 = x * x
        r = jnp.where(lo,
                      lax.rsqrt(jnp.sum(jnp.where(lo, x2, 0.0), -1, keepdims=True) * (1.0 / FOX_DIM) + EPS),
                      lax.rsqrt(jnp.sum(jnp.where(lo, 0.0, x2), -1, keepdims=True) * (1.0 / FOX_DIM) + EPS))
        y = x * r * gfk_ref[...]
        k_out[0, he] = jnp.where(lo, y, bias[:, (2 * p) * LANES:(2 * p + 1) * LANES]).astype(BF16)
        k_out[0, ho] = jnp.where(lo, bias[:, (2 * p + 1) * LANES:(2 * p + 2) * LANES], y).astype(BF16)
        blk = proj[:, _O_FV + p * LANES:_O_FV + (p + 1) * LANES]
        v_out[0, he] = jnp.where(lo, blk, one_e).astype(BF16)
        v_out[0, ho] = jnp.where(lo, one_o, blk).astype(BF16)
        if not is_meta:
            x = proj[:, _O_FQ + p * LANES:_O_FQ + (p + 1) * LANES]
            x2 = x * x
            r = jnp.where(lo,
                          lax.rsqrt(jnp.sum(jnp.where(lo, x2, 0.0), -1, keepdims=True) * (1.0 / FOX_DIM) + EPS),
                          lax.rsqrt(jnp.sum(jnp.where(lo, 0.0, x2), -1, keepdims=True) * (1.0 / FOX_DIM) + EPS))
            y = x * r * gfq_ref[...]
            q_out[0, he] = jnp.where(lo, y, bias_e).astype(BF16)
            q_out[0, ho] = jnp.where(lo, bias_o, y).astype(BF16)


def _proj(h, consts, cc, ss, *, is_meta):
    b, s, _ = h.shape
    tm = min(PROJ_TM, s)
    (gmix, win, gcq, wuq, gckv, wukv, gq, gqs, gk, gks, bf, gfq, gfk, place) = consts
    tri = jnp.tril(jnp.ones((tm, tm), BF16))
    row_in = pl.BlockSpec((1, tm, D_MODEL), lambda bi, i: (bi, i, 0))
    tab = pl.BlockSpec((tm, LANES), lambda bi, i: (i, 0))
    head_out = pl.BlockSpec((1, N_HEADS, tm, LANES), lambda bi, i: (bi, 0, i, 0))
    head_shape = jax.ShapeDtypeStruct((b, N_HEADS, s, LANES), BF16)
    vec = _const_spec((1, LANES))
    in_specs = [row_in, _const_spec((1, D_MODEL)), _const_spec(win.shape),
                _const_spec((1, MLA_Q_RANK)), _const_spec(wuq.shape),
                _const_spec((1, MLA_KV_RANK)), _const_spec(wukv.shape),
                vec, vec, vec, vec, tab, tab, vec, vec, vec,
                _const_spec((tm, tm)), _const_spec(place.shape)]
    n_out = 2 if is_meta else 3
    scratch = [] if is_meta else [pltpu.VMEM((8, LANES), F32)]
    return pl.pallas_call(
        functools.partial(_proj_kernel, tm=tm, is_meta=is_meta),
        grid=(b, s // tm),
        in_specs=in_specs,
        out_specs=[head_out] * n_out,
        out_shape=[head_shape] * n_out,
        scratch_shapes=scratch,
        compiler_params=pltpu.CompilerParams(
            dimension_semantics=("arbitrary", "arbitrary"), vmem_limit_bytes=VMEM_LIMIT),
        name="proj_meta" if is_meta else "proj",
    )(h, gmix, win, gcq, wuq, gckv, wukv, gq, gqs, gk, gks, cc, ss, bf, gfq, gfk, tri, place)


def _attn_kernel(q_ref, k_ref, v_ref, km_ref, vm_ref, o_ref, m_sc, acc_sc, *, tq, tk):
    i = pl.program_id(2)
    dims = (((1,), (1,)), ((), ()))

    def step(hh, kc, vc, mask, first):
        s = lax.dot_general(q_ref[0, hh], kc, dims, preferred_element_type=F32)
        if mask is not None:
            s = jnp.where(mask, s, NEG_INF)
        smax = jnp.max(s, axis=-1, keepdims=True)
        if first:
            m_new = smax
        else:
            m_old = m_sc[hh]
            m_new = jnp.maximum(m_old, smax)
        p = jnp.exp(s - m_new).astype(BF16)
        pv = jnp.dot(p, vc, preferred_element_type=F32)
        if first:
            acc_sc[hh] = pv
        else:
            acc_sc[hh] = jnp.exp(m_old - m_new) * acc_sc[hh] + pv
        m_sc[hh] = m_new

    meta_valid = lax.broadcasted_iota(jnp.int32, (tq, BLOCK), 1) >= BLOCK - N_META
    for hh in range(2):
        step(hh, km_ref[0, hh], vm_ref[0, hh], meta_valid, True)

    def body(j, carry):
        off = pl.multiple_of(j * tk, tk)
        for hh in range(2):
            step(hh, k_ref[0, hh, pl.ds(off, tk), :], v_ref[0, hh, pl.ds(off, tk), :], None, False)
        return carry

    lax.fori_loop(0, i, body, 0)

    causal = (lax.broadcasted_iota(jnp.int32, (tq, tk), 1)
              <= lax.broadcasted_iota(jnp.int32, (tq, tk), 0))
    off = pl.multiple_of(i * tk, tk)
    for hh in range(2):
        step(hh, k_ref[0, hh, pl.ds(off, tk), :], v_ref[0, hh, pl.ds(off, tk), :], causal, False)

    a0 = acc_sc[0]
    a1 = acc_sc[1]
    lo = lax.broadcasted_iota(jnp.int32, (tq, LANES), 1) < HALF
    o_ref[0] = jnp.where(lo, a0 / a0[:, HALF:HALF + 1], a1 / a1[:, 0:1]).astype(BF16)


def _attention(q, k, v, km, vm):
    b, _, s, _ = q.shape
    tq, tk = ATT_TQ, ATT_TK
    assert tq == tk and s % tq == 0
    pairs = N_HEADS // 2
    return pl.pallas_call(
        functools.partial(_attn_kernel, tq=tq, tk=tk),
        grid=(b, pairs, s // tq),
        in_specs=[
            pl.BlockSpec((1, 2, tq, LANES), lambda bi, p, i: (bi, p, i, 0)),
            pl.BlockSpec((1, 2, s, LANES), lambda bi, p, i: (bi, p, 0, 0)),
            pl.BlockSpec((1, 2, s, LANES), lambda bi, p, i: (bi, p, 0, 0)),
            pl.BlockSpec((1, 2, BLOCK, LANES), lambda bi, p, i: (0, p, 0, 0)),
            pl.BlockSpec((1, 2, BLOCK, LANES), lambda bi, p, i: (0, p, 0, 0)),
        ],
        out_specs=pl.BlockSpec((1, tq, LANES), lambda bi, p, i: (bi, i, p)),
        out_shape=jax.ShapeDtypeStruct((b, s, pairs * LANES), BF16),
        scratch_shapes=[pltpu.VMEM((2, tq, 1), F32), pltpu.VMEM((2, tq, LANES), F32)],
        compiler_params=pltpu.CompilerParams(
            dimension_semantics=("arbitrary", "arbitrary", "arbitrary"),
            vmem_limit_bytes=VMEM_LIMIT),
        name="attention",
    )(q, k, v, km, vm)


def _pad_lanes(x, offset=0):
    n = x.shape[-1]
    pad = [(0, 0)] * (x.ndim - 1) + [(offset, LANES - offset - n)]
    return jnp.pad(x, pad)


def _swap_halves(x):
    half = MLA_ROPE // 2
    return jnp.concatenate([x[..., half:], x[..., :half]], axis=-1)


def _rope_lane_gains(g, scale):
    g = g.astype(F32) * scale
    direct = _pad_lanes(g)
    paired = _pad_lanes(_swap_halves(g[MLA_NOPE:]), MLA_NOPE)
    return direct[None], paired[None]


def _rope_tables(pos):
    half = MLA_ROPE // 2
    inv_freq = 1.0 / (ROPE_THETA ** (jnp.arange(half, dtype=F32) / half))
    ang = pos.astype(F32)[:, None] * inv_freq[None, :]
    cos, sin = jnp.cos(ang), jnp.sin(ang)
    ones = jnp.ones((pos.shape[0], MLA_NOPE), F32)
    zeros = jnp.zeros((pos.shape[0], LANES - MLA_QK), F32)
    cc = jnp.concatenate([ones, cos, cos, zeros], axis=1)
    ss = jnp.concatenate([0.0 * ones, -sin, sin, zeros], axis=1)
    return cc, ss


def _layout_params(g_mix, w_in, g_cq, w_uq, g_ckv, w_ukv, g_q_mla, g_k_mla, b_forget,
                   g_q_fox, g_k_fox):
    o_kpe = MLA_Q_RANK + MLA_KV_RANK
    o_fox = o_kpe + MLA_ROPE
    o_fl = o_fox + 3 * FOX_HEADS * FOX_DIM
    w_kpe = w_in[:, o_kpe:o_fox]
    win = jnp.concatenate([
        w_in[:, :o_kpe],
        _pad_lanes(w_kpe, MLA_NOPE),
        _pad_lanes(_swap_halves(w_kpe), MLA_NOPE),
        w_in[:, o_fox:o_fl],
        _pad_lanes(w_in[:, o_fl:]),
    ], axis=1).astype(BF16)
    assert win.shape[1] == _D_IN_PAD

    uq = w_uq.reshape(MLA_Q_RANK, MLA_HEADS, MLA_QK)
    uq_direct = _pad_lanes(uq).reshape(MLA_Q_RANK, MLA_HEADS * LANES)
    uq_paired = _pad_lanes(_swap_halves(uq[..., MLA_NOPE:]), MLA_NOPE).reshape(MLA_Q_RANK, MLA_HEADS * LANES)
    wuq = jnp.concatenate([uq_direct, uq_paired], axis=1).astype(BF16)

    ukv = w_ukv.reshape(MLA_KV_RANK, MLA_HEADS, MLA_NOPE + MLA_V)
    uk = _pad_lanes(ukv[..., :MLA_NOPE]).reshape(MLA_KV_RANK, MLA_HEADS * LANES)
    uv = ukv[..., MLA_NOPE:].reshape(MLA_KV_RANK, MLA_HEADS * MLA_V)
    wukv = jnp.concatenate([uk, uv], axis=1).astype(BF16)

    gq, gqs = _rope_lane_gains(g_q_mla, 1.0 / math.sqrt(MLA_QK))
    gk, gks = _rope_lane_gains(g_k_mla, 1.0)
    gfq = jnp.tile(g_q_fox.astype(F32) * (1.0 / math.sqrt(FOX_DIM)), 2)[None]
    gfk = jnp.tile(g_k_fox.astype(F32), 2)[None]
    bf = _pad_lanes(b_forget.astype(F32))[None]

    place = np.zeros((N_SPLIT * LANES, FOX_HEADS * LANES), np.float32)
    for h in range(FOX_HEADS):
        base = HALF if h % 2 == 0 else 0
        for j in range(N_SPLIT):
            place[j * LANES + h, h * LANES + base + j] = 1.0
    place = jnp.asarray(place, BF16)

    return (g_mix[None].astype(F32), win, g_cq[None].astype(F32), wuq, g_ckv[None].astype(F32), wukv,
            gq, gqs, gk, gks, bf, gfq, gfk, place)


def kernel(x, meta_tokens, g_ffn1, w1_gate, w1_up, w1_down, g_mix, w_in, g_cq, w_uq, g_ckv, w_ukv,
           g_q_mla, g_k_mla, b_forget, g_q_fox, g_k_fox, w_out, g_ffn2, w2_gate, w2_up, w2_down):
    b, s, d = x.shape
    depth = g_ffn1.shape[0]
    assert depth == 1 and d == D_MODEL
    l = 0

    ffn1 = (g_ffn1[l][None], w1_gate[l].astype(BF16), w1_up[l].astype(BF16), w1_down[l].astype(BF16))
    ffn2 = (g_ffn2[l][None], w2_gate[l].astype(BF16), w2_up[l].astype(BF16), w2_down[l].astype(BF16))
    consts = _layout_params(g_mix[l], w_in[l], g_cq[l], w_uq[l], g_ckv[l], w_ukv[l], g_q_mla[l],
                            g_k_mla[l], b_forget[l], g_q_fox[l], g_k_fox[l])

    pad = BLOCK - N_META
    h_meta = jnp.concatenate([jnp.zeros((pad, d), F32), meta_tokens.astype(F32)], axis=0)
    pos_meta = jnp.maximum(jnp.arange(BLOCK) - pad, 0)
    pos_tok = N_META + jnp.arange(s)

    h_meta = _ffn(h_meta, *ffn1)
    km, vm = _proj(h_meta[None], consts, *_rope_tables(pos_meta), is_meta=True)

    h1 = _ffn(x.reshape(b * s, d), *ffn1)
    q, k, v = _proj(h1.reshape(b, s, d), consts, *_rope_tables(pos_tok), is_meta=False)
    attn = _attention(q, k, v, km, vm)
    out = _ffn(h1, *ffn2, attn=attn.reshape(b * s, d), w_out=w_out[l].astype(BF16))
    return out.reshape(b, s, d)
```

```python
import functools
import math

import jax
import jax.numpy as jnp
import numpy as np
from jax import lax
from jax.experimental import pallas as pl
from jax.experimental.pallas import tpu as pltpu

F32 = jnp.float32
BF16 = jnp.bfloat16

D_MODEL = 1024
D_FF = 2816
N_META = 16
BLOCK = 128
EPS = 1e-6
NEG_INF = -1e30
MLA_HEADS = 8
MLA_Q_RANK = 256
MLA_KV_RANK = 128
MLA_NOPE = 64
MLA_ROPE = 32
MLA_QK = MLA_NOPE + MLA_ROPE
MLA_V = 64
ROPE_THETA = 10000.0
FOX_HEADS = 8
FOX_DIM = 64
N_HEADS = MLA_HEADS + FOX_HEADS
LANES = 128
HALF = LANES // 2
N_SPLIT = 3
LOG2E = math.log2(math.e)

FF_CHUNK = 256
FFN_TM = 512
PROJ_TM = 512
ATT_TQ = 512
ATT_TK = 512
VMEM_LIMIT = 56 * 1024 * 1024

_O_CQ = 0
_O_CKV = _O_CQ + MLA_Q_RANK
_O_KPE = _O_CKV + MLA_KV_RANK
_O_KPE_SW = _O_KPE + LANES
_O_FQ = _O_KPE_SW + LANES
_O_FK = _O_FQ + FOX_HEADS * FOX_DIM
_O_FL = _O_FK + FOX_HEADS * FOX_DIM
_D_IN_PAD = _O_FL + LANES

_NT = (((1,), (1,)), ((), ()))


def _const_spec(shape):
    nd = len(shape)
    return pl.BlockSpec(shape, lambda *_: (0,) * nd, pipeline_mode=pl.Buffered(1))


def _rms_scale(x, width):
    return lax.rsqrt(jnp.sum(x * x, axis=-1, keepdims=True) * (1.0 / width) + EPS)


def _split3(x):
    pieces = []
    r = x
    for _ in range(N_SPLIT):
        p = r.astype(BF16)
        pieces.append(p)
        r = r - p.astype(F32)
    return pieces


def _ffn_body(h, g_ref, wg_ref, wu_ref, wd_ref):
    n = (h * _rms_scale(h, D_MODEL) * g_ref[...]).astype(BF16)
    acc = jnp.zeros(h.shape, F32)
    for c in range(D_FF // FF_CHUNK):
        sl = slice(c * FF_CHUNK, (c + 1) * FF_CHUNK)
        g = jnp.dot(n, wg_ref[:, sl], preferred_element_type=F32)
        u = jnp.dot(n, wu_ref[:, sl], preferred_element_type=F32)
        a = (g * (1.0 / (1.0 + jnp.exp(-g))) * u).astype(BF16)
        acc = acc + jnp.dot(a, wd_ref[sl, :], preferred_element_type=F32)
    return h + 0.5 * acc


def _ffn_kernel(h_ref, g_ref, wg_ref, wu_ref, wd_ref, o_ref):
    o_ref[...] = _ffn_body(h_ref[...], g_ref, wg_ref, wu_ref, wd_ref)


def _mix_ffn_kernel(h_ref, a_ref, wo_ref, g_ref, wg_ref, wu_ref, wd_ref, o_ref):
    h = h_ref[...] + jnp.dot(a_ref[...], wo_ref[...], preferred_element_type=F32)
    o_ref[...] = _ffn_body(h, g_ref, wg_ref, wu_ref, wd_ref)


def _ffn(h, g, wg, wu, wd, attn=None, w_out=None):
    m = h.shape[0]
    tm = min(FFN_TM, m)
    row = pl.BlockSpec((tm, D_MODEL), lambda i: (i, 0))
    weights = [_const_spec((1, D_MODEL)), _const_spec((D_MODEL, D_FF)),
               _const_spec((D_MODEL, D_FF)), _const_spec((D_FF, D_MODEL))]
    if attn is None:
        kern, in_specs, args = _ffn_kernel, [row] + weights, (h, g, wg, wu, wd)
    else:
        kern = _mix_ffn_kernel
        in_specs = [row, row, _const_spec((D_MODEL, D_MODEL))] + weights
        args = (h, attn, w_out, g, wg, wu, wd)
    return pl.pallas_call(
        kern,
        grid=(m // tm,),
        in_specs=in_specs,
        out_specs=row,
        out_shape=jax.ShapeDtypeStruct((m, D_MODEL), F32),
        compiler_params=pltpu.CompilerParams(
            dimension_semantics=("arbitrary",), vmem_limit_bytes=VMEM_LIMIT),
        name="ffn_mix" if attn is not None else "ffn",
    )(*args)


def _half_rms_scale(x, lo):
    x2 = x * x
    s_lo = jnp.sum(jnp.where(lo, x2, 0.0), axis=-1, keepdims=True)
    s_hi = jnp.sum(jnp.where(lo, 0.0, x2), axis=-1, keepdims=True)
    return jnp.where(lo, lax.rsqrt(s_lo * (1.0 / FOX_DIM) + EPS), lax.rsqrt(s_hi * (1.0 / FOX_DIM) + EPS))


def _proj_kernel(h_ref, gmix_ref, win_ref, gcq_ref, wuq_ref, gckv_ref, wuk_ref, wuvt_ref, wfvt_ref,
                 gq_ref, gqs_ref, gk_ref, gks_ref, cc_ref, ss_ref, bf_ref, gfq_ref, gfk_ref,
                 tri_ref, place_ref, *rest, tm, is_meta):
    if is_meta:
        k_out, vt_out = rest
        q_out = carry_sc = None
    else:
        q_out, k_out, vt_out, carry_sc = rest

    lane = lax.broadcasted_iota(jnp.int32, (tm, LANES), 1)
    lo = lane < HALF
    bias_e = ((lane >= HALF) & (lane < HALF + N_SPLIT)).astype(F32)
    bias_o = (lane < N_SPLIT).astype(F32)
    ones_row = (lax.broadcasted_iota(jnp.int32, (HALF, tm), 0) == 0).astype(F32)

    def put_vt(head, blk):
        rows = [blk, ones_row] if head % 2 == 0 else [ones_row, blk]
        vt_out[0, head, 0] = jnp.concatenate(rows, axis=0).astype(BF16)

    h = h_ref[0]
    n = (h * _rms_scale(h, D_MODEL) * gmix_ref[...]).astype(BF16)
    proj = jnp.dot(n, win_ref[...], preferred_element_type=F32)

    c_q = proj[:, _O_CQ:_O_CKV]
    c_kv = proj[:, _O_CKV:_O_KPE]
    kpe = proj[:, _O_KPE:_O_KPE_SW]
    kpe_sw = proj[:, _O_KPE_SW:_O_FQ]

    cc = cc_ref[...]
    ss = ss_ref[...]

    ckvn = (c_kv * _rms_scale(c_kv, MLA_KV_RANK) * gckv_ref[...]).astype(BF16)
    kn = jnp.dot(ckvn, wuk_ref[...], preferred_element_type=F32)
    kc = kpe_sw * gks_ref[...] * ss
    for hd in range(MLA_HEADS):
        x = kn[:, hd * LANES:(hd + 1) * LANES] + kpe
        r = _rms_scale(x, MLA_QK)
        k_out[0, hd] = (r * (x * gk_ref[...] * cc + kc)).astype(BF16)
    vt = lax.dot_general(wuvt_ref[...], ckvn, _NT, preferred_element_type=F32)
    for hd in range(MLA_HEADS):
        put_vt(hd, vt[hd * MLA_V:(hd + 1) * MLA_V, :])
    if not is_meta:
        cqn = (c_q * _rms_scale(c_q, MLA_Q_RANK) * gcq_ref[...]).astype(BF16)
        qq = jnp.dot(cqn, wuq_ref[...], preferred_element_type=F32)
        for hd in range(MLA_HEADS):
            x = qq[:, hd * LANES:(hd + 1) * LANES]
            xs = qq[:, (MLA_HEADS + hd) * LANES:(MLA_HEADS + hd + 1) * LANES]
            r = _rms_scale(x, MLA_QK)
            q_out[0, hd] = (r * (x * gq_ref[...] * cc + xs * gqs_ref[...] * ss)).astype(BF16)

    fl = proj[:, _O_FL:_O_FL + LANES] + bf_ref[...]
    log_f = jnp.minimum(fl, 0.0) - jnp.log1p(jnp.exp(-jnp.abs(fl)))
    tri = tri_ref[...]
    cum = jnp.zeros((tm, LANES), F32)
    for piece in _split3(log_f):
        cum = cum + jnp.dot(tri, piece, preferred_element_type=F32)
    if is_meta:
        cum = cum - cum[tm - 1:tm, :]
    else:
        @pl.when(pl.program_id(1) == 0)
        def _():
            carry_sc[...] = jnp.zeros_like(carry_sc)
        cum = cum + carry_sc[0:1, :]
        carry_sc[...] = jnp.broadcast_to(cum[tm - 1:tm, :], carry_sc.shape)
    bias = jnp.dot(jnp.concatenate(_split3(cum * (-LOG2E)), axis=1), place_ref[...],
                   preferred_element_type=F32)
    fvt = lax.dot_general(wfvt_ref[...], n, _NT, preferred_element_type=F32)
    for p in range(FOX_HEADS // 2):
        he, ho = MLA_HEADS + 2 * p, MLA_HEADS + 2 * p + 1
        x = proj[:, _O_FK + p * LANES:_O_FK + (p + 1) * LANES]
        y = x * _half_rms_scale(x, lo) * gfk_ref[...]
        k_out[0, he] = jnp.where(lo, y, bias[:, (2 * p) * LANES:(2 * p + 1) * LANES]).astype(BF16)
        k_out[0, ho] = jnp.where(lo, bias[:, (2 * p + 1) * LANES:(2 * p + 2) * LANES], y).astype(BF16)
        put_vt(he, fvt[(2 * p) * FOX_DIM:(2 * p + 1) * FOX_DIM, :])
        put_vt(ho, fvt[(2 * p + 1) * FOX_DIM:(2 * p + 2) * FOX_DIM, :])
        if not is_meta:
            x = proj[:, _O_FQ + p * LANES:_O_FQ + (p + 1) * LANES]
            y = x * _half_rms_scale(x, lo) * gfq_ref[...]
            q_out[0, he] = jnp.where(lo, y, bias_e).astype(BF16)
            q_out[0, ho] = jnp.where(lo, bias_o, y).astype(BF16)


def _proj(h, consts, cc, ss, *, is_meta):
    b, s, _ = h.shape
    tm = min(PROJ_TM, s)
    (gmix, win, gcq, wuq, gckv, wuk, wuvt, wfvt, gq, gqs, gk, gks, bf, gfq, gfk, place) = consts
    tri = jnp.tril(jnp.ones((tm, tm), BF16))
    row_in = pl.BlockSpec((1, tm, D_MODEL), lambda bi, i: (bi, i, 0))
    tab = pl.BlockSpec((tm, LANES), lambda bi, i: (i, 0))
    head_out = pl.BlockSpec((1, N_HEADS, tm, LANES), lambda bi, i: (bi, 0, i, 0))
    head_shape = jax.ShapeDtypeStruct((b, N_HEADS, s, LANES), BF16)
    vt_out = pl.BlockSpec((1, N_HEADS, 1, LANES, tm), lambda bi, i: (bi, 0, i, 0, 0))
    vt_shape = jax.ShapeDtypeStruct((b, N_HEADS, s // tm, LANES, tm), BF16)
    vec = _const_spec((1, LANES))
    in_specs = [row_in, _const_spec((1, D_MODEL)), _const_spec(win.shape),
                _const_spec((1, MLA_Q_RANK)), _const_spec(wuq.shape),
                _const_spec((1, MLA_KV_RANK)), _const_spec(wuk.shape),
                _const_spec(wuvt.shape), _const_spec(wfvt.shape),
                vec, vec, vec, vec, tab, tab, vec, vec, vec,
                _const_spec((tm, tm)), _const_spec(place.shape)]
    if is_meta:
        out_specs, out_shape, scratch = [head_out, vt_out], [head_shape, vt_shape], []
    else:
        out_specs, out_shape = [head_out, head_out, vt_out], [head_shape, head_shape, vt_shape]
        scratch = [pltpu.VMEM((8, LANES), F32)]
    return pl.pallas_call(
        functools.partial(_proj_kernel, tm=tm, is_meta=is_meta),
        grid=(b, s // tm),
        in_specs=in_specs,
        out_specs=out_specs,
        out_shape=out_shape,
        scratch_shapes=scratch,
        compiler_params=pltpu.CompilerParams(
            dimension_semantics=("arbitrary", "arbitrary"), vmem_limit_bytes=VMEM_LIMIT),
        name="proj_meta" if is_meta else "proj",
    )(h, gmix, win, gcq, wuq, gckv, wuk, wuvt, wfvt, gq, gqs, gk, gks, cc, ss, bf, gfq, gfk, tri, place)


def _attn_kernel(q_ref, k_ref, vt_ref, km_ref, vmt_ref, o_ref, m_sc, acc_sc, *, tq, tk):
    i = pl.program_id(2)

    def scores(hh, kc):
        return lax.dot_general(kc, q_ref[0, hh], _NT, preferred_element_type=F32)

    def update(hh, s, vt, first):
        smax = jnp.max(s, axis=0, keepdims=True)
        if first:
            m_new = smax
        else:
            m_old = m_sc[hh]
            m_new = jnp.maximum(m_old, smax)
        p = jnp.exp2(s - m_new).astype(BF16)
        pv = jnp.dot(vt, p, preferred_element_type=F32)
        if first:
            acc_sc[hh] = pv
        else:
            acc_sc[hh] = jnp.exp2(m_old - m_new) * acc_sc[hh] + pv
        m_sc[hh] = m_new

    meta_valid = lax.broadcasted_iota(jnp.int32, (BLOCK, tq), 0) >= BLOCK - N_META
    s_meta = [jnp.where(meta_valid, scores(hh, km_ref[0, hh]), NEG_INF) for hh in range(2)]
    for hh in range(2):
        update(hh, s_meta[hh], vmt_ref[0, hh, 0], True)

    def body(j, carry):
        off = pl.multiple_of(j * tk, tk)
        s = [scores(hh, k_ref[0, hh, pl.ds(off, tk), :]) for hh in range(2)]
        for hh in range(2):
            update(hh, s[hh], vt_ref[0, hh, j], False)
        return carry

    lax.fori_loop(0, i, body, 0)

    causal = (lax.broadcasted_iota(jnp.int32, (tk, tq), 0)
              <= lax.broadcasted_iota(jnp.int32, (tk, tq), 1))
    off = pl.multiple_of(i * tk, tk)
    s = [jnp.where(causal, scores(hh, k_ref[0, hh, pl.ds(off, tk), :]), NEG_INF) for hh in range(2)]
    for hh in range(2):
        update(hh, s[hh], vt_ref[0, hh, i], False)

    a_e = acc_sc[0]
    a_o = acc_sc[1]
    top = lax.broadcasted_iota(jnp.int32, (LANES, tq), 0) < HALF
    out_t = jnp.where(top, a_e / a_e[HALF:HALF + 1, :], a_o / a_o[0:1, :])
    o_ref[0] = out_t.T.astype(BF16)


def _attention(q, k, vt, km, vmt):
    b, _, s, _ = q.shape
    tq, tk = ATT_TQ, ATT_TK
    assert tq == tk and s % tq == 0 and vt.shape[-1] == tk
    pairs = N_HEADS // 2
    return pl.pallas_call(
        functools.partial(_attn_kernel, tq=tq, tk=tk),
        grid=(b, pairs, s // tq),
        in_specs=[
            pl.BlockSpec((1, 2, tq, LANES), lambda bi, p, i: (bi, p, i, 0)),
            pl.BlockSpec((1, 2, s, LANES), lambda bi, p, i: (bi, p, 0, 0)),
            pl.BlockSpec((1, 2, s // tk, LANES, tk), lambda bi, p, i: (bi, p, 0, 0, 0)),
            pl.BlockSpec((1, 2, BLOCK, LANES), lambda bi, p, i: (0, p, 0, 0)),
            pl.BlockSpec((1, 2, 1, LANES, BLOCK), lambda bi, p, i: (0, p, 0, 0, 0)),
        ],
        out_specs=pl.BlockSpec((1, tq, LANES), lambda bi, p, i: (bi, i, p)),
        out_shape=jax.ShapeDtypeStruct((b, s, pairs * LANES), BF16),
        scratch_shapes=[pltpu.VMEM((2, 1, tq), F32), pltpu.VMEM((2, LANES, tq), F32)],
        compiler_params=pltpu.CompilerParams(
            dimension_semantics=("arbitrary", "arbitrary", "arbitrary"),
            vmem_limit_bytes=VMEM_LIMIT),
        name="attention",
    )(q, k, vt, km, vmt)


def _pad_lanes(x, offset=0):
    n = x.shape[-1]
    pad = [(0, 0)] * (x.ndim - 1) + [(offset, LANES - offset - n)]
    return jnp.pad(x, pad)


def _swap_halves(x):
    half = MLA_ROPE // 2
    return jnp.concatenate([x[..., half:], x[..., :half]], axis=-1)


def _rope_lane_gains(g, scale):
    g = g.astype(F32) * scale
    direct = _pad_lanes(g)
    paired = _pad_lanes(_swap_halves(g[MLA_NOPE:]), MLA_NOPE)
    return direct[None], paired[None]


def _rope_tables(pos):
    half = MLA_ROPE // 2
    inv_freq = 1.0 / (ROPE_THETA ** (jnp.arange(half, dtype=F32) / half))
    ang = pos.astype(F32)[:, None] * inv_freq[None, :]
    cos, sin = jnp.cos(ang), jnp.sin(ang)
    ones = jnp.ones((pos.shape[0], MLA_NOPE), F32)
    zeros = jnp.zeros((pos.shape[0], LANES - MLA_QK), F32)
    cc = jnp.concatenate([ones, cos, cos, zeros], axis=1)
    ss = jnp.concatenate([0.0 * ones, -sin, sin, zeros], axis=1)
    return cc, ss


def _layout_params(g_mix, w_in, g_cq, w_uq, g_ckv, w_ukv, g_q_mla, g_k_mla, b_forget,
                   g_q_fox, g_k_fox):
    o_kpe = MLA_Q_RANK + MLA_KV_RANK
    o_fox = o_kpe + MLA_ROPE
    o_fv = o_fox + 2 * FOX_HEADS * FOX_DIM
    o_fl = o_fv + FOX_HEADS * FOX_DIM
    w_kpe = w_in[:, o_kpe:o_fox]
    win = jnp.concatenate([
        w_in[:, :o_kpe],
        _pad_lanes(w_kpe, MLA_NOPE),
        _pad_lanes(_swap_halves(w_kpe), MLA_NOPE),
        w_in[:, o_fox:o_fv],
        _pad_lanes(w_in[:, o_fl:]),
    ], axis=1).astype(BF16)
    assert win.shape[1] == _D_IN_PAD
    wfvt = w_in[:, o_fv:o_fl].T.astype(BF16)

    uq = w_uq.reshape(MLA_Q_RANK, MLA_HEADS, MLA_QK)
    uq_direct = _pad_lanes(uq).reshape(MLA_Q_RANK, MLA_HEADS * LANES)
    uq_paired = _pad_lanes(_swap_halves(uq[..., MLA_NOPE:]), MLA_NOPE).reshape(MLA_Q_RANK, MLA_HEADS * LANES)
    wuq = jnp.concatenate([uq_direct, uq_paired], axis=1).astype(BF16)

    ukv = w_ukv.reshape(MLA_KV_RANK, MLA_HEADS, MLA_NOPE + MLA_V)
    wuk = _pad_lanes(ukv[..., :MLA_NOPE]).reshape(MLA_KV_RANK, MLA_HEADS * LANES).astype(BF16)
    wuvt = ukv[..., MLA_NOPE:].reshape(MLA_KV_RANK, MLA_HEADS * MLA_V).T.astype(BF16)

    gq, gqs = _rope_lane_gains(g_q_mla, LOG2E / math.sqrt(MLA_QK))
    gk, gks = _rope_lane_gains(g_k_mla, 1.0)
    gfq = jnp.tile(g_q_fox.astype(F32) * (LOG2E / math.sqrt(FOX_DIM)), 2)[None]
    gfk = jnp.tile(g_k_fox.astype(F32), 2)[None]
    bf = _pad_lanes(b_forget.astype(F32))[None]

    place = np.zeros((N_SPLIT * LANES, FOX_HEADS * LANES), np.float32)
    for h in range(FOX_HEADS):
        base = HALF if h % 2 == 0 else 0
        for j in range(N_SPLIT):
            place[j * LANES + h, h * LANES + base + j] = 1.0
    place = jnp.asarray(place, BF16)

    return (g_mix[None].astype(F32), win, g_cq[None].astype(F32), wuq, g_ckv[None].astype(F32),
            wuk, wuvt, wfvt, gq, gqs, gk, gks, bf, gfq, gfk, place)


def kernel(x, meta_tokens, g_ffn1, w1_gate, w1_up, w1_down, g_mix, w_in, g_cq, w_uq, g_ckv, w_ukv,
           g_q_mla, g_k_mla, b_forget, g_q_fox, g_k_fox, w_out, g_ffn2, w2_gate, w2_up, w2_down):
    b, s, d = x.shape
    depth = g_ffn1.shape[0]
    assert depth == 1 and d == D_MODEL
    l = 0

    ffn1 = (g_ffn1[l][None], w1_gate[l].astype(BF16), w1_up[l].astype(BF16), w1_down[l].astype(BF16))
    ffn2 = (g_ffn2[l][None], w2_gate[l].astype(BF16), w2_up[l].astype(BF16), w2_down[l].astype(BF16))
    consts = _layout_params(g_mix[l], w_in[l], g_cq[l], w_uq[l], g_ckv[l], w_ukv[l], g_q_mla[l],
                            g_k_mla[l], b_forget[l], g_q_fox[l], g_k_fox[l])

    pad = BLOCK - N_META
    h_meta = jnp.concatenate([jnp.zeros((pad, d), F32), meta_tokens.astype(F32)], axis=0)
    pos_meta = jnp.maximum(jnp.arange(BLOCK) - pad, 0)
    pos_tok = N_META + jnp.arange(s)

    h_meta = _ffn(h_meta, *ffn1)
    km, vmt = _proj(h_meta[None], consts, *_rope_tables(pos_meta), is_meta=True)

    h1 = _ffn(x.reshape(b * s, d), *ffn1)
    q, k, vt = _proj(h1.reshape(b, s, d), consts, *_rope_tables(pos_tok), is_meta=False)
    attn = _attention(q, k, vt, km, vmt)
    out = _ffn(h1, *ffn2, attn=attn.reshape(b * s, d), w_out=w_out[l].astype(BF16))
    return out.reshape(b, s, d)
```

```python
import functools
import math

import jax
import jax.numpy as jnp
import numpy as np
from jax import lax
from jax.experimental import pallas as pl
from jax.experimental.pallas import tpu as pltpu

F32 = jnp.float32
BF16 = jnp.bfloat16

D_MODEL = 1024
D_FF = 2816
N_META = 16
BLOCK = 128
EPS = 1e-6
NEG_INF = -1e30
MLA_HEADS = 8
MLA_Q_RANK = 256
MLA_KV_RANK = 128
MLA_NOPE = 64
MLA_ROPE = 32
MLA_QK = MLA_NOPE + MLA_ROPE
MLA_V = 64
ROPE_THETA = 10000.0
FOX_HEADS = 8
FOX_DIM = 64
N_HEADS = MLA_HEADS + FOX_HEADS
LANES = 128
HALF = LANES // 2
N_SPLIT = 3
LOG2E = math.log2(math.e)

FF_CHUNK = 256
FFN_TM = 512
PROJ_TM = 512
ATT_TQ = 1024
ATT_TK = 1024
VMEM_LIMIT = 56 * 1024 * 1024

_O_CQ = 0
_O_CKV = _O_CQ + MLA_Q_RANK
_O_KPE = _O_CKV + MLA_KV_RANK
_O_KPE_SW = _O_KPE + LANES
_O_FQ = _O_KPE_SW + LANES
_O_FK = _O_FQ + FOX_HEADS * FOX_DIM
_O_FL = _O_FK + FOX_HEADS * FOX_DIM
_D_IN_PAD = _O_FL + LANES

_NT = (((1,), (1,)), ((), ()))


def _const_spec(shape):
    nd = len(shape)
    return pl.BlockSpec(shape, lambda *_: (0,) * nd, pipeline_mode=pl.Buffered(1))


def _rms_scale(x, width):
    return lax.rsqrt(jnp.sum(x * x, axis=-1, keepdims=True) * (1.0 / width) + EPS)


def _split3(x):
    pieces = []
    r = x
    for _ in range(N_SPLIT):
        p = r.astype(BF16)
        pieces.append(p)
        r = r - p.astype(F32)
    return pieces


def _ffn_body(h, g_ref, wg_ref, wu_ref, wd_ref):
    n = (h * _rms_scale(h, D_MODEL) * g_ref[...]).astype(BF16)
    acc = jnp.zeros(h.shape, F32)
    for c in range(D_FF // FF_CHUNK):
        sl = slice(c * FF_CHUNK, (c + 1) * FF_CHUNK)
        g = jnp.dot(n, wg_ref[:, sl], preferred_element_type=F32)
        u = jnp.dot(n, wu_ref[:, sl], preferred_element_type=F32)
        a = (g * (1.0 / (1.0 + jnp.exp(-g))) * u).astype(BF16)
        acc = acc + jnp.dot(a, wd_ref[sl, :], preferred_element_type=F32)
    return h + 0.5 * acc


def _ffn_kernel(h_ref, g_ref, wg_ref, wu_ref, wd_ref, o_ref):
    o_ref[...] = _ffn_body(h_ref[...], g_ref, wg_ref, wu_ref, wd_ref)


def _mix_ffn_kernel(h_ref, a_ref, wo_ref, g_ref, wg_ref, wu_ref, wd_ref, o_ref):
    h = h_ref[...] + jnp.dot(a_ref[...], wo_ref[...], preferred_element_type=F32)
    o_ref[...] = _ffn_body(h, g_ref, wg_ref, wu_ref, wd_ref)


def _ffn(h, g, wg, wu, wd, attn=None, w_out=None):
    m = h.shape[0]
    tm = min(FFN_TM, m)
    row = pl.BlockSpec((tm, D_MODEL), lambda i: (i, 0))
    weights = [_const_spec((1, D_MODEL)), _const_spec((D_MODEL, D_FF)),
               _const_spec((D_MODEL, D_FF)), _const_spec((D_FF, D_MODEL))]
    if attn is None:
        kern, in_specs, args = _ffn_kernel, [row] + weights, (h, g, wg, wu, wd)
    else:
        kern = _mix_ffn_kernel
        in_specs = [row, row, _const_spec((D_MODEL, D_MODEL))] + weights
        args = (h, attn, w_out, g, wg, wu, wd)
    return pl.pallas_call(
        kern,
        grid=(m // tm,),
        in_specs=in_specs,
        out_specs=row,
        out_shape=jax.ShapeDtypeStruct((m, D_MODEL), F32),
        compiler_params=pltpu.CompilerParams(
            dimension_semantics=("arbitrary",), vmem_limit_bytes=VMEM_LIMIT),
        name="ffn_mix" if attn is not None else "ffn",
    )(*args)


def _half_rms_scale(x, lo):
    x2 = x * x
    s_lo = jnp.sum(jnp.where(lo, x2, 0.0), axis=-1, keepdims=True)
    s_hi = jnp.sum(jnp.where(lo, 0.0, x2), axis=-1, keepdims=True)
    return jnp.where(lo, lax.rsqrt(s_lo * (1.0 / FOX_DIM) + EPS), lax.rsqrt(s_hi * (1.0 / FOX_DIM) + EPS))


def _proj_kernel(h_ref, gmix_ref, win_ref, gcq_ref, wuq_ref, gckv_ref, wuk_ref, wuvt_ref, wfvt_ref,
                 gq_ref, gqs_ref, gk_ref, gks_ref, cc_ref, ss_ref, bf_ref, gfq_ref, gfk_ref,
                 tri_ref, place_ref, *rest, tm, is_meta):
    if is_meta:
        k_out, vt_out = rest
        q_out = carry_sc = None
    else:
        q_out, k_out, vt_out, carry_sc = rest

    lane = lax.broadcasted_iota(jnp.int32, (tm, LANES), 1)
    lo = lane < HALF
    bias_e = ((lane >= HALF) & (lane < HALF + N_SPLIT)).astype(F32)
    bias_o = (lane < N_SPLIT).astype(F32)
    ones_row = (lax.broadcasted_iota(jnp.int32, (HALF, tm), 0) == 0).astype(F32)

    def put_vt(head, blk):
        rows = [blk, ones_row] if head % 2 == 0 else [ones_row, blk]
        vt_out[0, head, 0] = jnp.concatenate(rows, axis=0).astype(BF16)

    h = h_ref[0]
    n = (h * _rms_scale(h, D_MODEL) * gmix_ref[...]).astype(BF16)
    proj = jnp.dot(n, win_ref[...], preferred_element_type=F32)

    c_q = proj[:, _O_CQ:_O_CKV]
    c_kv = proj[:, _O_CKV:_O_KPE]
    kpe = proj[:, _O_KPE:_O_KPE_SW]
    kpe_sw = proj[:, _O_KPE_SW:_O_FQ]

    cc = cc_ref[...]
    ss = ss_ref[...]

    ckvn = (c_kv * _rms_scale(c_kv, MLA_KV_RANK) * gckv_ref[...]).astype(BF16)
    kn = jnp.dot(ckvn, wuk_ref[...], preferred_element_type=F32)
    kc = kpe_sw * gks_ref[...] * ss
    for hd in range(MLA_HEADS):
        x = kn[:, hd * LANES:(hd + 1) * LANES] + kpe
        r = _rms_scale(x, MLA_QK)
        k_out[0, hd] = (r * (x * gk_ref[...] * cc + kc)).astype(BF16)
    vt = lax.dot_general(wuvt_ref[...], ckvn, _NT, preferred_element_type=F32)
    for hd in range(MLA_HEADS):
        put_vt(hd, vt[hd * MLA_V:(hd + 1) * MLA_V, :])
    if not is_meta:
        cqn = (c_q * _rms_scale(c_q, MLA_Q_RANK) * gcq_ref[...]).astype(BF16)
        qq = jnp.dot(cqn, wuq_ref[...], preferred_element_type=F32)
        for hd in range(MLA_HEADS):
            x = qq[:, hd * LANES:(hd + 1) * LANES]
            xs = qq[:, (MLA_HEADS + hd) * LANES:(MLA_HEADS + hd + 1) * LANES]
            r = _rms_scale(x, MLA_QK)
            q_out[0, hd] = (r * (x * gq_ref[...] * cc + xs * gqs_ref[...] * ss)).astype(BF16)

    fl = proj[:, _O_FL:_O_FL + LANES] + bf_ref[...]
    log_f = jnp.minimum(fl, 0.0) - jnp.log1p(jnp.exp(-jnp.abs(fl)))
    tri = tri_ref[...]
    cum = jnp.zeros((tm, LANES), F32)
    for piece in _split3(log_f):
        cum = cum + jnp.dot(tri, piece, preferred_element_type=F32)
    if is_meta:
        cum = cum - cum[tm - 1:tm, :]
    else:
        @pl.when(pl.program_id(1) == 0)
        def _():
            carry_sc[...] = jnp.zeros_like(carry_sc)
        cum = cum + carry_sc[0:1, :]
        carry_sc[...] = jnp.broadcast_to(cum[tm - 1:tm, :], carry_sc.shape)
    bias = jnp.dot(jnp.concatenate(_split3(cum * (-LOG2E)), axis=1), place_ref[...],
                   preferred_element_type=F32)
    fvt = lax.dot_general(wfvt_ref[...], n, _NT, preferred_element_type=F32)
    for p in range(FOX_HEADS // 2):
        he, ho = MLA_HEADS + 2 * p, MLA_HEADS + 2 * p + 1
        x = proj[:, _O_FK + p * LANES:_O_FK + (p + 1) * LANES]
        y = x * _half_rms_scale(x, lo) * gfk_ref[...]
        k_out[0, he] = jnp.where(lo, y, bias[:, (2 * p) * LANES:(2 * p + 1) * LANES]).astype(BF16)
        k_out[0, ho] = jnp.where(lo, bias[:, (2 * p + 1) * LANES:(2 * p + 2) * LANES], y).astype(BF16)
        put_vt(he, fvt[(2 * p) * FOX_DIM:(2 * p + 1) * FOX_DIM, :])
        put_vt(ho, fvt[(2 * p + 1) * FOX_DIM:(2 * p + 2) * FOX_DIM, :])
        if not is_meta:
            x = proj[:, _O_FQ + p * LANES:_O_FQ + (p + 1) * LANES]
            y = x * _half_rms_scale(x, lo) * gfq_ref[...]
            q_out[0, he] = jnp.where(lo, y, bias_e).astype(BF16)
            q_out[0, ho] = jnp.where(lo, bias_o, y).astype(BF16)


def _proj(h, consts, cc, ss, *, is_meta):
    b, s, _ = h.shape
    tm = min(PROJ_TM, s)
    (gmix, win, gcq, wuq, gckv, wuk, wuvt, wfvt, gq, gqs, gk, gks, bf, gfq, gfk, place) = consts
    tri = jnp.tril(jnp.ones((tm, tm), BF16))
    row_in = pl.BlockSpec((1, tm, D_MODEL), lambda bi, i: (bi, i, 0))
    tab = pl.BlockSpec((tm, LANES), lambda bi, i: (i, 0))
    head_out = pl.BlockSpec((1, N_HEADS, tm, LANES), lambda bi, i: (bi, 0, i, 0))
    head_shape = jax.ShapeDtypeStruct((b, N_HEADS, s, LANES), BF16)
    vt_out = pl.BlockSpec((1, N_HEADS, 1, LANES, tm), lambda bi, i: (bi, 0, i, 0, 0))
    vt_shape = jax.ShapeDtypeStruct((b, N_HEADS, s // tm, LANES, tm), BF16)
    vec = _const_spec((1, LANES))
    in_specs = [row_in, _const_spec((1, D_MODEL)), _const_spec(win.shape),
                _const_spec((1, MLA_Q_RANK)), _const_spec(wuq.shape),
                _const_spec((1, MLA_KV_RANK)), _const_spec(wuk.shape),
                _const_spec(wuvt.shape), _const_spec(wfvt.shape),
                vec, vec, vec, vec, tab, tab, vec, vec, vec,
                _const_spec((tm, tm)), _const_spec(place.shape)]
    if is_meta:
        out_specs, out_shape, scratch = [head_out, vt_out], [head_shape, vt_shape], []
    else:
        out_specs, out_shape = [head_out, head_out, vt_out], [head_shape, head_shape, vt_shape]
        scratch = [pltpu.VMEM((8, LANES), F32)]
    return pl.pallas_call(
        functools.partial(_proj_kernel, tm=tm, is_meta=is_meta),
        grid=(b, s // tm),
        in_specs=in_specs,
        out_specs=out_specs,
        out_shape=out_shape,
        scratch_shapes=scratch,
        compiler_params=pltpu.CompilerParams(
            dimension_semantics=("arbitrary", "arbitrary"), vmem_limit_bytes=VMEM_LIMIT),
        name="proj_meta" if is_meta else "proj",
    )(h, gmix, win, gcq, wuq, gckv, wuk, wuvt, wfvt, gq, gqs, gk, gks, cc, ss, bf, gfq, gfk, tri, place)


def _attn_kernel(q_ref, k_ref, vt_ref, km_ref, vmt_ref, o_ref, m_sc, acc_sc, *, tq, tk):
    i = pl.program_id(2)
    sub = tk // vt_ref.shape[-1]

    def values(hh, j):
        return jnp.concatenate([vt_ref[0, hh, j * sub + c] for c in range(sub)], axis=1)

    def scores(hh, kc):
        return lax.dot_general(kc, q_ref[0, hh], _NT, preferred_element_type=F32)

    def update(hh, s, vt, first):
        smax = jnp.max(s, axis=0, keepdims=True)
        if first:
            m_new = smax
        else:
            m_old = m_sc[hh]
            m_new = jnp.maximum(m_old, smax)
        p = jnp.exp2(s - m_new).astype(BF16)
        pv = jnp.dot(vt, p, preferred_element_type=F32)
        if first:
            acc_sc[hh] = pv
        else:
            acc_sc[hh] = jnp.exp2(m_old - m_new) * acc_sc[hh] + pv
        m_sc[hh] = m_new

    meta_valid = lax.broadcasted_iota(jnp.int32, (BLOCK, tq), 0) >= BLOCK - N_META
    s_meta = [jnp.where(meta_valid, scores(hh, km_ref[0, hh]), NEG_INF) for hh in range(2)]
    for hh in range(2):
        update(hh, s_meta[hh], vmt_ref[0, hh, 0], True)

    def body(j, carry):
        off = pl.multiple_of(j * tk, tk)
        s = [scores(hh, k_ref[0, hh, pl.ds(off, tk), :]) for hh in range(2)]
        for hh in range(2):
            update(hh, s[hh], values(hh, j), False)
        return carry

    lax.fori_loop(0, i, body, 0)

    causal = (lax.broadcasted_iota(jnp.int32, (tk, tq), 0)
              <= lax.broadcasted_iota(jnp.int32, (tk, tq), 1))
    off = pl.multiple_of(i * tk, tk)
    s = [jnp.where(causal, scores(hh, k_ref[0, hh, pl.ds(off, tk), :]), NEG_INF) for hh in range(2)]
    for hh in range(2):
        update(hh, s[hh], values(hh, i), False)

    a_e = acc_sc[0]
    a_o = acc_sc[1]
    top = lax.broadcasted_iota(jnp.int32, (LANES, tq), 0) < HALF
    out_t = jnp.where(top, a_e / a_e[HALF:HALF + 1, :], a_o / a_o[0:1, :])
    o_ref[0] = out_t.T.astype(BF16)


def _attention(q, k, vt, km, vmt):
    b, _, s, _ = q.shape
    tq, tk = ATT_TQ, ATT_TK
    vchunk = vt.shape[-1]
    assert tq == tk and s % tq == 0 and tk % vchunk == 0
    pairs = N_HEADS // 2
    return pl.pallas_call(
        functools.partial(_attn_kernel, tq=tq, tk=tk),
        grid=(b, pairs, s // tq),
        in_specs=[
            pl.BlockSpec((1, 2, tq, LANES), lambda bi, p, i: (bi, p, i, 0)),
            pl.BlockSpec((1, 2, s, LANES), lambda bi, p, i: (bi, p, 0, 0)),
            pl.BlockSpec((1, 2, s // vchunk, LANES, vchunk), lambda bi, p, i: (bi, p, 0, 0, 0)),
            pl.BlockSpec((1, 2, BLOCK, LANES), lambda bi, p, i: (0, p, 0, 0)),
            pl.BlockSpec((1, 2, 1, LANES, BLOCK), lambda bi, p, i: (0, p, 0, 0, 0)),
        ],
        out_specs=pl.BlockSpec((1, tq, LANES), lambda bi, p, i: (bi, i, p)),
        out_shape=jax.ShapeDtypeStruct((b, s, pairs * LANES), BF16),
        scratch_shapes=[pltpu.VMEM((2, 1, tq), F32), pltpu.VMEM((2, LANES, tq), F32)],
        compiler_params=pltpu.CompilerParams(
            dimension_semantics=("arbitrary", "arbitrary", "arbitrary"),
            vmem_limit_bytes=VMEM_LIMIT),
        name="attention",
    )(q, k, vt, km, vmt)


def _pad_lanes(x, offset=0):
    n = x.shape[-1]
    pad = [(0, 0)] * (x.ndim - 1) + [(offset, LANES - offset - n)]
    return jnp.pad(x, pad)


def _swap_halves(x):
    half = MLA_ROPE // 2
    return jnp.concatenate([x[..., half:], x[..., :half]], axis=-1)


def _rope_lane_gains(g, scale):
    g = g.astype(F32) * scale
    direct = _pad_lanes(g)
    paired = _pad_lanes(_swap_halves(g[MLA_NOPE:]), MLA_NOPE)
    return direct[None], paired[None]


def _rope_tables(pos):
    half = MLA_ROPE // 2
    inv_freq = 1.0 / (ROPE_THETA ** (jnp.arange(half, dtype=F32) / half))
    ang = pos.astype(F32)[:, None] * inv_freq[None, :]
    cos, sin = jnp.cos(ang), jnp.sin(ang)
    ones = jnp.ones((pos.shape[0], MLA_NOPE), F32)
    zeros = jnp.zeros((pos.shape[0], LANES - MLA_QK), F32)
    cc = jnp.concatenate([ones, cos, cos, zeros], axis=1)
    ss = jnp.concatenate([0.0 * ones, -sin, sin, zeros], axis=1)
    return cc, ss


def _layout_params(g_mix, w_in, g_cq, w_uq, g_ckv, w_ukv, g_q_mla, g_k_mla, b_forget,
                   g_q_fox, g_k_fox):
    o_kpe = MLA_Q_RANK + MLA_KV_RANK
    o_fox = o_kpe + MLA_ROPE
    o_fv = o_fox + 2 * FOX_HEADS * FOX_DIM
    o_fl = o_fv + FOX_HEADS * FOX_DIM
    w_kpe = w_in[:, o_kpe:o_fox]
    win = jnp.concatenate([
        w_in[:, :o_kpe],
        _pad_lanes(w_kpe, MLA_NOPE),
        _pad_lanes(_swap_halves(w_kpe), MLA_NOPE),
        w_in[:, o_fox:o_fv],
        _pad_lanes(w_in[:, o_fl:]),
    ], axis=1).astype(BF16)
    assert win.shape[1] == _D_IN_PAD
    wfvt = w_in[:, o_fv:o_fl].T.astype(BF16)

    uq = w_uq.reshape(MLA_Q_RANK, MLA_HEADS, MLA_QK)
    uq_direct = _pad_lanes(uq).reshape(MLA_Q_RANK, MLA_HEADS * LANES)
    uq_paired = _pad_lanes(_swap_halves(uq[..., MLA_NOPE:]), MLA_NOPE).reshape(MLA_Q_RANK, MLA_HEADS * LANES)
    wuq = jnp.concatenate([uq_direct, uq_paired], axis=1).astype(BF16)

    ukv = w_ukv.reshape(MLA_KV_RANK, MLA_HEADS, MLA_NOPE + MLA_V)
    wuk = _pad_lanes(ukv[..., :MLA_NOPE]).reshape(MLA_KV_RANK, MLA_HEADS * LANES).astype(BF16)
    wuvt = ukv[..., MLA_NOPE:].reshape(MLA_KV_RANK, MLA_HEADS * MLA_V).T.astype(BF16)

    gq, gqs = _rope_lane_gains(g_q_mla, LOG2E / math.sqrt(MLA_QK))
    gk, gks = _rope_lane_gains(g_k_mla, 1.0)
    gfq = jnp.tile(g_q_fox.astype(F32) * (LOG2E / math.sqrt(FOX_DIM)), 2)[None]
    gfk = jnp.tile(g_k_fox.astype(F32), 2)[None]
    bf = _pad_lanes(b_forget.astype(F32))[None]

    place = np.zeros((N_SPLIT * LANES, FOX_HEADS * LANES), np.float32)
    for h in range(FOX_HEADS):
        base = HALF if h % 2 == 0 else 0
        for j in range(N_SPLIT):
            place[j * LANES + h, h * LANES + base + j] = 1.0
    place = jnp.asarray(place, BF16)

    return (g_mix[None].astype(F32), win, g_cq[None].astype(F32), wuq, g_ckv[None].astype(F32),
            wuk, wuvt, wfvt, gq, gqs, gk, gks, bf, gfq, gfk, place)


def kernel(x, meta_tokens, g_ffn1, w1_gate, w1_up, w1_down, g_mix, w_in, g_cq, w_uq, g_ckv, w_ukv,
           g_q_mla, g_k_mla, b_forget, g_q_fox, g_k_fox, w_out, g_ffn2, w2_gate, w2_up, w2_down):
    b, s, d = x.shape
    depth = g_ffn1.shape[0]
    assert depth == 1 and d == D_MODEL
    l = 0

    ffn1 = (g_ffn1[l][None], w1_gate[l].astype(BF16), w1_up[l].astype(BF16), w1_down[l].astype(BF16))
    ffn2 = (g_ffn2[l][None], w2_gate[l].astype(BF16), w2_up[l].astype(BF16), w2_down[l].astype(BF16))
    consts = _layout_params(g_mix[l], w_in[l], g_cq[l], w_uq[l], g_ckv[l], w_ukv[l], g_q_mla[l],
                            g_k_mla[l], b_forget[l], g_q_fox[l], g_k_fox[l])

    pad = BLOCK - N_META
    h_meta = jnp.concatenate([jnp.zeros((pad, d), F32), meta_tokens.astype(F32)], axis=0)
    pos_meta = jnp.maximum(jnp.arange(BLOCK) - pad, 0)
    pos_tok = N_META + jnp.arange(s)

    h_meta = _ffn(h_meta, *ffn1)
    km, vmt = _proj(h_meta[None], consts, *_rope_tables(pos_meta), is_meta=True)

    h1 = _ffn(x.reshape(b * s, d), *ffn1)
    q, k, vt = _proj(h1.reshape(b, s, d), consts, *_rope_tables(pos_tok), is_meta=False)
    attn = _attention(q, k, vt, km, vmt)
    out = _ffn(h1, *ffn2, attn=attn.reshape(b * s, d), w_out=w_out[l].astype(BF16))
    return out.reshape(b, s, d)
```

```python
import functools
import math

import jax
import jax.numpy as jnp
import numpy as np
from jax import lax
from jax.experimental import pallas as pl
from jax.experimental.pallas import tpu as pltpu

F32 = jnp.float32
BF16 = jnp.bfloat16

D_MODEL = 1024
D_FF = 2816
N_META = 16
BLOCK = 128
EPS = 1e-6
NEG_INF = -1e30
MLA_HEADS = 8
MLA_Q_RANK = 256
MLA_KV_RANK = 128
MLA_NOPE = 64
MLA_ROPE = 32
MLA_QK = MLA_NOPE + MLA_ROPE
MLA_V = 64
ROPE_THETA = 10000.0
FOX_HEADS = 8
FOX_DIM = 64
N_HEADS = MLA_HEADS + FOX_HEADS
LANES = 128
HALF = LANES // 2
N_SPLIT = 3
LOG2E = math.log2(math.e)

FF_CHUNK = 256
FFN_TM = 512
PROJ_TM = 512
ATT_TQ = 1024
ATT_TK = 1024
DIAG_COLS = 256
VMEM_LIMIT = 56 * 1024 * 1024

_O_CQ = 0
_O_CKV = _O_CQ + MLA_Q_RANK
_O_KPE = _O_CKV + MLA_KV_RANK
_O_KPE_SW = _O_KPE + LANES
_O_FQ = _O_KPE_SW + LANES
_O_FK = _O_FQ + FOX_HEADS * FOX_DIM
_O_FL = _O_FK + FOX_HEADS * FOX_DIM
_D_IN_PAD = _O_FL + LANES

_NT = (((1,), (1,)), ((), ()))


def _const_spec(shape):
    nd = len(shape)
    return pl.BlockSpec(shape, lambda *_: (0,) * nd, pipeline_mode=pl.Buffered(1))


def _rms_scale(x, width):
    return lax.rsqrt(jnp.sum(x * x, axis=-1, keepdims=True) * (1.0 / width) + EPS)


def _split3(x):
    pieces = []
    r = x
    for _ in range(N_SPLIT):
        p = r.astype(BF16)
        pieces.append(p)
        r = r - p.astype(F32)
    return pieces


def _ffn_body(h, g_ref, wg_ref, wu_ref, wd_ref):
    n = (h * _rms_scale(h, D_MODEL) * g_ref[...]).astype(BF16)
    acc = jnp.zeros(h.shape, F32)
    for c in range(D_FF // FF_CHUNK):
        sl = slice(c * FF_CHUNK, (c + 1) * FF_CHUNK)
        g = jnp.dot(n, wg_ref[:, sl], preferred_element_type=F32)
        u = jnp.dot(n, wu_ref[:, sl], preferred_element_type=F32)
        a = (g * (1.0 / (1.0 + jnp.exp(-g))) * u).astype(BF16)
        acc = acc + jnp.dot(a, wd_ref[sl, :], preferred_element_type=F32)
    return h + 0.5 * acc


def _ffn_kernel(h_ref, g_ref, wg_ref, wu_ref, wd_ref, o_ref):
    o_ref[...] = _ffn_body(h_ref[...], g_ref, wg_ref, wu_ref, wd_ref)


def _mix_ffn_kernel(h_ref, a_ref, wo_ref, g_ref, wg_ref, wu_ref, wd_ref, o_ref):
    h = h_ref[...] + jnp.dot(a_ref[...], wo_ref[...], preferred_element_type=F32)
    o_ref[...] = _ffn_body(h, g_ref, wg_ref, wu_ref, wd_ref)


def _ffn(h, g, wg, wu, wd, attn=None, w_out=None):
    m = h.shape[0]
    tm = min(FFN_TM, m)
    row = pl.BlockSpec((tm, D_MODEL), lambda i: (i, 0))
    weights = [_const_spec((1, D_MODEL)), _const_spec((D_MODEL, D_FF)),
               _const_spec((D_MODEL, D_FF)), _const_spec((D_FF, D_MODEL))]
    if attn is None:
        kern, in_specs, args = _ffn_kernel, [row] + weights, (h, g, wg, wu, wd)
    else:
        kern = _mix_ffn_kernel
        in_specs = [row, row, _const_spec((D_MODEL, D_MODEL))] + weights
        args = (h, attn, w_out, g, wg, wu, wd)
    return pl.pallas_call(
        kern,
        grid=(m // tm,),
        in_specs=in_specs,
        out_specs=row,
        out_shape=jax.ShapeDtypeStruct((m, D_MODEL), F32),
        compiler_params=pltpu.CompilerParams(
            dimension_semantics=("arbitrary",), vmem_limit_bytes=VMEM_LIMIT),
        name="ffn_mix" if attn is not None else "ffn",
    )(*args)


def _half_rms_scale(x, lo):
    x2 = x * x
    s_lo = jnp.sum(jnp.where(lo, x2, 0.0), axis=-1, keepdims=True)
    s_hi = jnp.sum(jnp.where(lo, 0.0, x2), axis=-1, keepdims=True)
    return jnp.where(lo, lax.rsqrt(s_lo * (1.0 / FOX_DIM) + EPS), lax.rsqrt(s_hi * (1.0 / FOX_DIM) + EPS))


def _proj_kernel(h_ref, gmix_ref, win_ref, gcq_ref, wuq_ref, gckv_ref, wuk_ref, wuvt_ref, wfvt_ref,
                 gq_ref, gqs_ref, gk_ref, gks_ref, cc_ref, ss_ref, bf_ref, gfq_ref, gfk_ref,
                 tri_ref, place_ref, *rest, tm, is_meta):
    if is_meta:
        k_out, vt_out = rest
        q_out = carry_sc = None
    else:
        q_out, k_out, vt_out, carry_sc = rest

    lane = lax.broadcasted_iota(jnp.int32, (tm, LANES), 1)
    lo = lane < HALF
    bias_e = ((lane >= HALF) & (lane < HALF + N_SPLIT)).astype(F32)
    bias_o = (lane < N_SPLIT).astype(F32)
    ones_row = (lax.broadcasted_iota(jnp.int32, (HALF, tm), 0) == 0).astype(F32)

    def put_vt(head, blk):
        rows = [blk, ones_row] if head % 2 == 0 else [ones_row, blk]
        vt_out[0, head, 0] = jnp.concatenate(rows, axis=0).astype(BF16)

    h = h_ref[0]
    n = (h * _rms_scale(h, D_MODEL) * gmix_ref[...]).astype(BF16)
    proj = jnp.dot(n, win_ref[...], preferred_element_type=F32)

    c_q = proj[:, _O_CQ:_O_CKV]
    c_kv = proj[:, _O_CKV:_O_KPE]
    kpe = proj[:, _O_KPE:_O_KPE_SW]
    kpe_sw = proj[:, _O_KPE_SW:_O_FQ]

    cc = cc_ref[...]
    ss = ss_ref[...]

    ckvn = (c_kv * _rms_scale(c_kv, MLA_KV_RANK) * gckv_ref[...]).astype(BF16)
    kn = jnp.dot(ckvn, wuk_ref[...], preferred_element_type=F32)
    kc = kpe_sw * gks_ref[...] * ss
    for hd in range(MLA_HEADS):
        x = kn[:, hd * LANES:(hd + 1) * LANES] + kpe
        r = _rms_scale(x, MLA_QK)
        k_out[0, hd] = (r * (x * gk_ref[...] * cc + kc)).astype(BF16)
    vt = lax.dot_general(wuvt_ref[...], ckvn, _NT, preferred_element_type=F32)
    for hd in range(MLA_HEADS):
        put_vt(hd, vt[hd * MLA_V:(hd + 1) * MLA_V, :])
    if not is_meta:
        cqn = (c_q * _rms_scale(c_q, MLA_Q_RANK) * gcq_ref[...]).astype(BF16)
        qq = jnp.dot(cqn, wuq_ref[...], preferred_element_type=F32)
        for hd in range(MLA_HEADS):
            x = qq[:, hd * LANES:(hd + 1) * LANES]
            xs = qq[:, (MLA_HEADS + hd) * LANES:(MLA_HEADS + hd + 1) * LANES]
            r = _rms_scale(x, MLA_QK)
            q_out[0, hd] = (r * (x * gq_ref[...] * cc + xs * gqs_ref[...] * ss)).astype(BF16)

    fl = proj[:, _O_FL:_O_FL + LANES] + bf_ref[...]
    log_f = jnp.minimum(fl, 0.0) - jnp.log1p(jnp.exp(-jnp.abs(fl)))
    tri = tri_ref[...]
    cum = jnp.zeros((tm, LANES), F32)
    for piece in _split3(log_f):
        cum = cum + jnp.dot(tri, piece, preferred_element_type=F32)
    if is_meta:
        cum = cum - cum[tm - 1:tm, :]
    else:
        @pl.when(pl.program_id(1) == 0)
        def _():
            carry_sc[...] = jnp.zeros_like(carry_sc)
        cum = cum + carry_sc[0:1, :]
        carry_sc[...] = jnp.broadcast_to(cum[tm - 1:tm, :], carry_sc.shape)
    bias = jnp.dot(jnp.concatenate(_split3(cum * (-LOG2E)), axis=1), place_ref[...],
                   preferred_element_type=F32)
    fvt = lax.dot_general(wfvt_ref[...], n, _NT, preferred_element_type=F32)
    for p in range(FOX_HEADS // 2):
        he, ho = MLA_HEADS + 2 * p, MLA_HEADS + 2 * p + 1
        x = proj[:, _O_FK + p * LANES:_O_FK + (p + 1) * LANES]
        y = x * _half_rms_scale(x, lo) * gfk_ref[...]
        k_out[0, he] = jnp.where(lo, y, bias[:, (2 * p) * LANES:(2 * p + 1) * LANES]).astype(BF16)
        k_out[0, ho] = jnp.where(lo, bias[:, (2 * p + 1) * LANES:(2 * p + 2) * LANES], y).astype(BF16)
        put_vt(he, fvt[(2 * p) * FOX_DIM:(2 * p + 1) * FOX_DIM, :])
        put_vt(ho, fvt[(2 * p + 1) * FOX_DIM:(2 * p + 2) * FOX_DIM, :])
        if not is_meta:
            x = proj[:, _O_FQ + p * LANES:_O_FQ + (p + 1) * LANES]
            y = x * _half_rms_scale(x, lo) * gfq_ref[...]
            q_out[0, he] = jnp.where(lo, y, bias_e).astype(BF16)
            q_out[0, ho] = jnp.where(lo, bias_o, y).astype(BF16)


def _proj(h, consts, cc, ss, *, is_meta):
    b, s, _ = h.shape
    tm = min(PROJ_TM, s)
    (gmix, win, gcq, wuq, gckv, wuk, wuvt, wfvt, gq, gqs, gk, gks, bf, gfq, gfk, place) = consts
    tri = jnp.tril(jnp.ones((tm, tm), BF16))
    row_in = pl.BlockSpec((1, tm, D_MODEL), lambda bi, i: (bi, i, 0))
    tab = pl.BlockSpec((tm, LANES), lambda bi, i: (i, 0))
    head_out = pl.BlockSpec((1, N_HEADS, tm, LANES), lambda bi, i: (bi, 0, i, 0))
    head_shape = jax.ShapeDtypeStruct((b, N_HEADS, s, LANES), BF16)
    vt_out = pl.BlockSpec((1, N_HEADS, 1, LANES, tm), lambda bi, i: (bi, 0, i, 0, 0))
    vt_shape = jax.ShapeDtypeStruct((b, N_HEADS, s // tm, LANES, tm), BF16)
    vec = _const_spec((1, LANES))
    in_specs = [row_in, _const_spec((1, D_MODEL)), _const_spec(win.shape),
                _const_spec((1, MLA_Q_RANK)), _const_spec(wuq.shape),
                _const_spec((1, MLA_KV_RANK)), _const_spec(wuk.shape),
                _const_spec(wuvt.shape), _const_spec(wfvt.shape),
                vec, vec, vec, vec, tab, tab, vec, vec, vec,
                _const_spec((tm, tm)), _const_spec(place.shape)]
    if is_meta:
        out_specs, out_shape, scratch = [head_out, vt_out], [head_shape, vt_shape], []
    else:
        out_specs, out_shape = [head_out, head_out, vt_out], [head_shape, head_shape, vt_shape]
        scratch = [pltpu.VMEM((8, LANES), F32)]
    return pl.pallas_call(
        functools.partial(_proj_kernel, tm=tm, is_meta=is_meta),
        grid=(b, s // tm),
        in_specs=in_specs,
        out_specs=out_specs,
        out_shape=out_shape,
        scratch_shapes=scratch,
        compiler_params=pltpu.CompilerParams(
            dimension_semantics=("arbitrary", "arbitrary"), vmem_limit_bytes=VMEM_LIMIT),
        name="proj_meta" if is_meta else "proj",
    )(h, gmix, win, gcq, wuq, gckv, wuk, wuvt, wfvt, gq, gqs, gk, gks, cc, ss, bf, gfq, gfk, tri, place)


def _attn_kernel(q_ref, k_ref, vt_ref, km_ref, vmt_ref, o_ref, m_sc, acc_sc, *, tq, tk):
    i = pl.program_id(2)
    sub = tk // vt_ref.shape[-1]

    def values(hh, j):
        return jnp.concatenate([vt_ref[0, hh, j * sub + c] for c in range(sub)], axis=1)

    def scores(hh, kc, cols=slice(None)):
        return lax.dot_general(kc, q_ref[0, hh, cols, :], _NT, preferred_element_type=F32)

    def update(hh, s, vt, cols=slice(None)):
        m_old = m_sc[hh, :, cols]
        m_new = jnp.maximum(m_old, jnp.max(s, axis=0, keepdims=True))
        p = jnp.exp2(s - m_new).astype(BF16)
        pv = jnp.dot(vt, p, preferred_element_type=F32)
        acc_sc[hh, :, cols] = jnp.exp2(m_old - m_new) * acc_sc[hh, :, cols] + pv
        m_sc[hh, :, cols] = m_new

    m_sc[...] = jnp.full(m_sc.shape, NEG_INF, F32)
    acc_sc[...] = jnp.zeros(acc_sc.shape, F32)

    def body(j, carry):
        off = pl.multiple_of(j * tk, tk)
        s = [scores(hh, k_ref[0, hh, pl.ds(off, tk), :]) for hh in range(2)]
        for hh in range(2):
            update(hh, s[hh], values(hh, j))
        return carry

    lax.fori_loop(0, i, body, 0)

    meta_valid = lax.broadcasted_iota(jnp.int32, (BLOCK, tq), 0) >= BLOCK - N_META
    s_meta = [jnp.where(meta_valid, scores(hh, km_ref[0, hh]), NEG_INF) for hh in range(2)]
    off = pl.multiple_of(i * tk, tk)
    s_diag = {}
    for hh in range(2):
        for c in range(tq // DIAG_COLS):
            nk = (c + 1) * DIAG_COLS
            cols = slice(c * DIAG_COLS, nk)
            s = scores(hh, k_ref[0, hh, pl.ds(off, nk), :], cols)
            causal = (lax.broadcasted_iota(jnp.int32, (nk, DIAG_COLS), 0)
                      <= lax.broadcasted_iota(jnp.int32, (nk, DIAG_COLS), 1) + c * DIAG_COLS)
            s_diag[hh, c] = jnp.where(causal, s, NEG_INF)
    for hh in range(2):
        update(hh, s_meta[hh], vmt_ref[0, hh, 0])
        vt = values(hh, i)
        for c in range(tq // DIAG_COLS):
            nk = (c + 1) * DIAG_COLS
            update(hh, s_diag[hh, c], vt[:, :nk], slice(c * DIAG_COLS, nk))

    a_e = acc_sc[0]
    a_o = acc_sc[1]
    top = lax.broadcasted_iota(jnp.int32, (LANES, tq), 0) < HALF
    out_t = jnp.where(top, a_e / a_e[HALF:HALF + 1, :], a_o / a_o[0:1, :])
    o_ref[0] = out_t.T.astype(BF16)


def _attention(q, k, vt, km, vmt):
    b, _, s, _ = q.shape
    tq, tk = ATT_TQ, ATT_TK
    vchunk = vt.shape[-1]
    assert tq == tk and s % tq == 0 and tk % vchunk == 0
    pairs = N_HEADS // 2
    return pl.pallas_call(
        functools.partial(_attn_kernel, tq=tq, tk=tk),
        grid=(b, pairs, s // tq),
        in_specs=[
            pl.BlockSpec((1, 2, tq, LANES), lambda bi, p, i: (bi, p, i, 0)),
            pl.BlockSpec((1, 2, s, LANES), lambda bi, p, i: (bi, p, 0, 0)),
            pl.BlockSpec((1, 2, s // vchunk, LANES, vchunk), lambda bi, p, i: (bi, p, 0, 0, 0)),
            pl.BlockSpec((1, 2, BLOCK, LANES), lambda bi, p, i: (0, p, 0, 0)),
            pl.BlockSpec((1, 2, 1, LANES, BLOCK), lambda bi, p, i: (0, p, 0, 0, 0)),
        ],
        out_specs=pl.BlockSpec((1, tq, LANES), lambda bi, p, i: (bi, i, p)),
        out_shape=jax.ShapeDtypeStruct((b, s, pairs * LANES), BF16),
        scratch_shapes=[pltpu.VMEM((2, 1, tq), F32), pltpu.VMEM((2, LANES, tq), F32)],
        compiler_params=pltpu.CompilerParams(
            dimension_semantics=("arbitrary", "arbitrary", "arbitrary"),
            vmem_limit_bytes=VMEM_LIMIT),
        name="attention",
    )(q, k, vt, km, vmt)


def _pad_lanes(x, offset=0):
    n = x.shape[-1]
    pad = [(0, 0)] * (x.ndim - 1) + [(offset, LANES - offset - n)]
    return jnp.pad(x, pad)


def _swap_halves(x):
    half = MLA_ROPE // 2
    return jnp.concatenate([x[..., half:], x[..., :half]], axis=-1)


def _rope_lane_gains(g, scale):
    g = g.astype(F32) * scale
    direct = _pad_lanes(g)
    paired = _pad_lanes(_swap_halves(g[MLA_NOPE:]), MLA_NOPE)
    return direct[None], paired[None]


def _rope_tables(pos):
    half = MLA_ROPE // 2
    inv_freq = 1.0 / (ROPE_THETA ** (jnp.arange(half, dtype=F32) / half))
    ang = pos.astype(F32)[:, None] * inv_freq[None, :]
    cos, sin = jnp.cos(ang), jnp.sin(ang)
    ones = jnp.ones((pos.shape[0], MLA_NOPE), F32)
    zeros = jnp.zeros((pos.shape[0], LANES - MLA_QK), F32)
    cc = jnp.concatenate([ones, cos, cos, zeros], axis=1)
    ss = jnp.concatenate([0.0 * ones, -sin, sin, zeros], axis=1)
    return cc, ss


def _layout_params(g_mix, w_in, g_cq, w_uq, g_ckv, w_ukv, g_q_mla, g_k_mla, b_forget,
                   g_q_fox, g_k_fox):
    o_kpe = MLA_Q_RANK + MLA_KV_RANK
    o_fox = o_kpe + MLA_ROPE
    o_fv = o_fox + 2 * FOX_HEADS * FOX_DIM
    o_fl = o_fv + FOX_HEADS * FOX_DIM
    w_kpe = w_in[:, o_kpe:o_fox]
    win = jnp.concatenate([
        w_in[:, :o_kpe],
        _pad_lanes(w_kpe, MLA_NOPE),
        _pad_lanes(_swap_halves(w_kpe), MLA_NOPE),
        w_in[:, o_fox:o_fv],
        _pad_lanes(w_in[:, o_fl:]),
    ], axis=1).astype(BF16)
    assert win.shape[1] == _D_IN_PAD
    wfvt = w_in[:, o_fv:o_fl].T.astype(BF16)

    uq = w_uq.reshape(MLA_Q_RANK, MLA_HEADS, MLA_QK)
    uq_direct = _pad_lanes(uq).reshape(MLA_Q_RANK, MLA_HEADS * LANES)
    uq_paired = _pad_lanes(_swap_halves(uq[..., MLA_NOPE:]), MLA_NOPE).reshape(MLA_Q_RANK, MLA_HEADS * LANES)
    wuq = jnp.concatenate([uq_direct, uq_paired], axis=1).astype(BF16)

    ukv = w_ukv.reshape(MLA_KV_RANK, MLA_HEADS, MLA_NOPE + MLA_V)
    wuk = _pad_lanes(ukv[..., :MLA_NOPE]).reshape(MLA_KV_RANK, MLA_HEADS * LANES).astype(BF16)
    wuvt = ukv[..., MLA_NOPE:].reshape(MLA_KV_RANK, MLA_HEADS * MLA_V).T.astype(BF16)

    gq, gqs = _rope_lane_gains(g_q_mla, LOG2E / math.sqrt(MLA_QK))
    gk, gks = _rope_lane_gains(g_k_mla, 1.0)
    gfq = jnp.tile(g_q_fox.astype(F32) * (LOG2E / math.sqrt(FOX_DIM)), 2)[None]
    gfk = jnp.tile(g_k_fox.astype(F32), 2)[None]
    bf = _pad_lanes(b_forget.astype(F32))[None]

    place = np.zeros((N_SPLIT * LANES, FOX_HEADS * LANES), np.float32)
    for h in range(FOX_HEADS):
        base = HALF if h % 2 == 0 else 0
        for j in range(N_SPLIT):
            place[j * LANES + h, h * LANES + base + j] = 1.0
    place = jnp.asarray(place, BF16)

    return (g_mix[None].astype(F32), win, g_cq[None].astype(F32), wuq, g_ckv[None].astype(F32),
            wuk, wuvt, wfvt, gq, gqs, gk, gks, bf, gfq, gfk, place)


def kernel(x, meta_tokens, g_ffn1, w1_gate, w1_up, w1_down, g_mix, w_in, g_cq, w_uq, g_ckv, w_ukv,
           g_q_mla, g_k_mla, b_forget, g_q_fox, g_k_fox, w_out, g_ffn2, w2_gate, w2_up, w2_down):
    b, s, d = x.shape
    depth = g_ffn1.shape[0]
    assert depth == 1 and d == D_MODEL
    l = 0

    ffn1 = (g_ffn1[l][None], w1_gate[l].astype(BF16), w1_up[l].astype(BF16), w1_down[l].astype(BF16))
    ffn2 = (g_ffn2[l][None], w2_gate[l].astype(BF16), w2_up[l].astype(BF16), w2_down[l].astype(BF16))
    consts = _layout_params(g_mix[l], w_in[l], g_cq[l], w_uq[l], g_ckv[l], w_ukv[l], g_q_mla[l],
                            g_k_mla[l], b_forget[l], g_q_fox[l], g_k_fox[l])

    pad = BLOCK - N_META
    h_meta = jnp.concatenate([jnp.zeros((pad, d), F32), meta_tokens.astype(F32)], axis=0)
    pos_meta = jnp.maximum(jnp.arange(BLOCK) - pad, 0)
    pos_tok = N_META + jnp.arange(s)

    h_meta = _ffn(h_meta, *ffn1)
    km, vmt = _proj(h_meta[None], consts, *_rope_tables(pos_meta), is_meta=True)

    h1 = _ffn(x.reshape(b * s, d), *ffn1)
    q, k, vt = _proj(h1.reshape(b, s, d), consts, *_rope_tables(pos_tok), is_meta=False)
    attn = _attention(q, k, vt, km, vmt)
    out = _ffn(h1, *ffn2, attn=attn.reshape(b * s, d), w_out=w_out[l].astype(BF16))
    return out.reshape(b, s, d)
```

```python
import functools
import math

import jax
import jax.numpy as jnp
import numpy as np
from jax import lax
from jax.experimental import pallas as pl
from jax.experimental.pallas import tpu as pltpu

F32 = jnp.float32
BF16 = jnp.bfloat16

D_MODEL = 1024
D_FF = 2816
N_META = 16
BLOCK = 128
EPS = 1e-6
NEG_INF = -1e30
MLA_HEADS = 8
MLA_Q_RANK = 256
MLA_KV_RANK = 128
MLA_NOPE = 64
MLA_ROPE = 32
MLA_QK = MLA_NOPE + MLA_ROPE
MLA_V = 64
ROPE_THETA = 10000.0
FOX_HEADS = 8
FOX_DIM = 64
N_HEADS = MLA_HEADS + FOX_HEADS
LANES = 128
HALF = LANES // 2
N_SPLIT = 3
LOG2E = math.log2(math.e)

FF_CHUNK = 256
FFN_TM = 512
PROJ_TM = 512
ATT_TQ = 2048
ATT_TK = 1024
DIAG_COLS = 256
VMEM_LIMIT = 56 * 1024 * 1024

_O_CQ = 0
_O_CKV = _O_CQ + MLA_Q_RANK
_O_KPE = _O_CKV + MLA_KV_RANK
_O_KPE_SW = _O_KPE + LANES
_O_FQ = _O_KPE_SW + LANES
_O_FK = _O_FQ + FOX_HEADS * FOX_DIM
_O_FL = _O_FK + FOX_HEADS * FOX_DIM
_D_IN_PAD = _O_FL + LANES

_NT = (((1,), (1,)), ((), ()))


def _const_spec(shape):
    nd = len(shape)
    return pl.BlockSpec(shape, lambda *_: (0,) * nd, pipeline_mode=pl.Buffered(1))


def _rms_scale(x, width):
    return lax.rsqrt(jnp.sum(x * x, axis=-1, keepdims=True) * (1.0 / width) + EPS)


def _split3(x):
    pieces = []
    r = x
    for _ in range(N_SPLIT):
        p = r.astype(BF16)
        pieces.append(p)
        r = r - p.astype(F32)
    return pieces


def _ffn_body(h, g_ref, wg_ref, wu_ref, wd_ref):
    n = (h * _rms_scale(h, D_MODEL) * g_ref[...]).astype(BF16)
    acc = jnp.zeros(h.shape, F32)
    for c in range(D_FF // FF_CHUNK):
        sl = slice(c * FF_CHUNK, (c + 1) * FF_CHUNK)
        g = jnp.dot(n, wg_ref[:, sl], preferred_element_type=F32)
        u = jnp.dot(n, wu_ref[:, sl], preferred_element_type=F32)
        a = (g * (1.0 / (1.0 + jnp.exp(-g))) * u).astype(BF16)
        acc = acc + jnp.dot(a, wd_ref[sl, :], preferred_element_type=F32)
    return h + 0.5 * acc


def _ffn_kernel(h_ref, g_ref, wg_ref, wu_ref, wd_ref, o_ref):
    o_ref[...] = _ffn_body(h_ref[...], g_ref, wg_ref, wu_ref, wd_ref)


def _mix_ffn_kernel(h_ref, a_ref, wo_ref, g_ref, wg_ref, wu_ref, wd_ref, o_ref):
    h = h_ref[...] + jnp.dot(a_ref[...], wo_ref[...], preferred_element_type=F32)
    o_ref[...] = _ffn_body(h, g_ref, wg_ref, wu_ref, wd_ref)


def _ffn(h, g, wg, wu, wd, attn=None, w_out=None):
    m = h.shape[0]
    tm = min(FFN_TM, m)
    row = pl.BlockSpec((tm, D_MODEL), lambda i: (i, 0))
    weights = [_const_spec((1, D_MODEL)), _const_spec((D_MODEL, D_FF)),
               _const_spec((D_MODEL, D_FF)), _const_spec((D_FF, D_MODEL))]
    if attn is None:
        kern, in_specs, args = _ffn_kernel, [row] + weights, (h, g, wg, wu, wd)
    else:
        kern = _mix_ffn_kernel
        in_specs = [row, row, _const_spec((D_MODEL, D_MODEL))] + weights
        args = (h, attn, w_out, g, wg, wu, wd)
    return pl.pallas_call(
        kern,
        grid=(m // tm,),
        in_specs=in_specs,
        out_specs=row,
        out_shape=jax.ShapeDtypeStruct((m, D_MODEL), F32),
        compiler_params=pltpu.CompilerParams(
            dimension_semantics=("arbitrary",), vmem_limit_bytes=VMEM_LIMIT),
        name="ffn_mix" if attn is not None else "ffn",
    )(*args)


def _half_rms_scale(x, lo):
    x2 = x * x
    s_lo = jnp.sum(jnp.where(lo, x2, 0.0), axis=-1, keepdims=True)
    s_hi = jnp.sum(jnp.where(lo, 0.0, x2), axis=-1, keepdims=True)
    return jnp.where(lo, lax.rsqrt(s_lo * (1.0 / FOX_DIM) + EPS), lax.rsqrt(s_hi * (1.0 / FOX_DIM) + EPS))


def _proj_kernel(h_ref, gmix_ref, win_ref, gcq_ref, wuq_ref, gckv_ref, wuk_ref, wuvt_ref, wfvt_ref,
                 gq_ref, gqs_ref, gk_ref, gks_ref, cc_ref, ss_ref, bf_ref, gfq_ref, gfk_ref,
                 tri_ref, place_ref, *rest, tm, is_meta):
    if is_meta:
        k_out, vt_out = rest
        q_out = carry_sc = None
    else:
        q_out, k_out, vt_out, carry_sc = rest

    lane = lax.broadcasted_iota(jnp.int32, (tm, LANES), 1)
    lo = lane < HALF
    bias_e = ((lane >= HALF) & (lane < HALF + N_SPLIT)).astype(F32)
    bias_o = (lane < N_SPLIT).astype(F32)
    ones_row = (lax.broadcasted_iota(jnp.int32, (HALF, tm), 0) == 0).astype(F32)

    def put_vt(head, blk):
        rows = [blk, ones_row] if head % 2 == 0 else [ones_row, blk]
        vt_out[0, head, 0] = jnp.concatenate(rows, axis=0).astype(BF16)

    h = h_ref[0]
    n = (h * _rms_scale(h, D_MODEL) * gmix_ref[...]).astype(BF16)
    proj = jnp.dot(n, win_ref[...], preferred_element_type=F32)

    c_q = proj[:, _O_CQ:_O_CKV]
    c_kv = proj[:, _O_CKV:_O_KPE]
    kpe = proj[:, _O_KPE:_O_KPE_SW]
    kpe_sw = proj[:, _O_KPE_SW:_O_FQ]

    cc = cc_ref[...]
    ss = ss_ref[...]

    ckvn = (c_kv * _rms_scale(c_kv, MLA_KV_RANK) * gckv_ref[...]).astype(BF16)
    kn = jnp.dot(ckvn, wuk_ref[...], preferred_element_type=F32)
    kc = kpe_sw * gks_ref[...] * ss
    for hd in range(MLA_HEADS):
        x = kn[:, hd * LANES:(hd + 1) * LANES] + kpe
        r = _rms_scale(x, MLA_QK)
        k_out[0, hd] = (r * (x * gk_ref[...] * cc + kc)).astype(BF16)
    vt = lax.dot_general(wuvt_ref[...], ckvn, _NT, preferred_element_type=F32)
    for hd in range(MLA_HEADS):
        put_vt(hd, vt[hd * MLA_V:(hd + 1) * MLA_V, :])
    if not is_meta:
        cqn = (c_q * _rms_scale(c_q, MLA_Q_RANK) * gcq_ref[...]).astype(BF16)
        qq = jnp.dot(cqn, wuq_ref[...], preferred_element_type=F32)
        for hd in range(MLA_HEADS):
            x = qq[:, hd * LANES:(hd + 1) * LANES]
            xs = qq[:, (MLA_HEADS + hd) * LANES:(MLA_HEADS + hd + 1) * LANES]
            r = _rms_scale(x, MLA_QK)
            q_out[0, hd] = (r * (x * gq_ref[...] * cc + xs * gqs_ref[...] * ss)).astype(BF16)

    fl = proj[:, _O_FL:_O_FL + LANES] + bf_ref[...]
    log_f = jnp.minimum(fl, 0.0) - jnp.log1p(jnp.exp(-jnp.abs(fl)))
    tri = tri_ref[...]
    cum = jnp.zeros((tm, LANES), F32)
    for piece in _split3(log_f):
        cum = cum + jnp.dot(tri, piece, preferred_element_type=F32)
    if is_meta:
        cum = cum - cum[tm - 1:tm, :]
    else:
        @pl.when(pl.program_id(1) == 0)
        def _():
            carry_sc[...] = jnp.zeros_like(carry_sc)
        cum = cum + carry_sc[0:1, :]
        carry_sc[...] = jnp.broadcast_to(cum[tm - 1:tm, :], carry_sc.shape)
    bias = jnp.dot(jnp.concatenate(_split3(cum * (-LOG2E)), axis=1), place_ref[...],
                   preferred_element_type=F32)
    fvt = lax.dot_general(wfvt_ref[...], n, _NT, preferred_element_type=F32)
    for p in range(FOX_HEADS // 2):
        he, ho = MLA_HEADS + 2 * p, MLA_HEADS + 2 * p + 1
        x = proj[:, _O_FK + p * LANES:_O_FK + (p + 1) * LANES]
        y = x * _half_rms_scale(x, lo) * gfk_ref[...]
        k_out[0, he] = jnp.where(lo, y, bias[:, (2 * p) * LANES:(2 * p + 1) * LANES]).astype(BF16)
        k_out[0, ho] = jnp.where(lo, bias[:, (2 * p + 1) * LANES:(2 * p + 2) * LANES], y).astype(BF16)
        put_vt(he, fvt[(2 * p) * FOX_DIM:(2 * p + 1) * FOX_DIM, :])
        put_vt(ho, fvt[(2 * p + 1) * FOX_DIM:(2 * p + 2) * FOX_DIM, :])
        if not is_meta:
            x = proj[:, _O_FQ + p * LANES:_O_FQ + (p + 1) * LANES]
            y = x * _half_rms_scale(x, lo) * gfq_ref[...]
            q_out[0, he] = jnp.where(lo, y, bias_e).astype(BF16)
            q_out[0, ho] = jnp.where(lo, bias_o, y).astype(BF16)


def _proj(h, consts, cc, ss, *, is_meta):
    b, s, _ = h.shape
    tm = min(PROJ_TM, s)
    (gmix, win, gcq, wuq, gckv, wuk, wuvt, wfvt, gq, gqs, gk, gks, bf, gfq, gfk, place) = consts
    tri = jnp.tril(jnp.ones((tm, tm), BF16))
    row_in = pl.BlockSpec((1, tm, D_MODEL), lambda bi, i: (bi, i, 0))
    tab = pl.BlockSpec((tm, LANES), lambda bi, i: (i, 0))
    head_out = pl.BlockSpec((1, N_HEADS, tm, LANES), lambda bi, i: (bi, 0, i, 0))
    head_shape = jax.ShapeDtypeStruct((b, N_HEADS, s, LANES), BF16)
    vt_out = pl.BlockSpec((1, N_HEADS, 1, LANES, tm), lambda bi, i: (bi, 0, i, 0, 0))
    vt_shape = jax.ShapeDtypeStruct((b, N_HEADS, s // tm, LANES, tm), BF16)
    vec = _const_spec((1, LANES))
    in_specs = [row_in, _const_spec((1, D_MODEL)), _const_spec(win.shape),
                _const_spec((1, MLA_Q_RANK)), _const_spec(wuq.shape),
                _const_spec((1, MLA_KV_RANK)), _const_spec(wuk.shape),
                _const_spec(wuvt.shape), _const_spec(wfvt.shape),
                vec, vec, vec, vec, tab, tab, vec, vec, vec,
                _const_spec((tm, tm)), _const_spec(place.shape)]
    if is_meta:
        out_specs, out_shape, scratch = [head_out, vt_out], [head_shape, vt_shape], []
    else:
        out_specs, out_shape = [head_out, head_out, vt_out], [head_shape, head_shape, vt_shape]
        scratch = [pltpu.VMEM((8, LANES), F32)]
    return pl.pallas_call(
        functools.partial(_proj_kernel, tm=tm, is_meta=is_meta),
        grid=(b, s // tm),
        in_specs=in_specs,
        out_specs=out_specs,
        out_shape=out_shape,
        scratch_shapes=scratch,
        compiler_params=pltpu.CompilerParams(
            dimension_semantics=("arbitrary", "arbitrary"), vmem_limit_bytes=VMEM_LIMIT),
        name="proj_meta" if is_meta else "proj",
    )(h, gmix, win, gcq, wuq, gckv, wuk, wuvt, wfvt, gq, gqs, gk, gks, cc, ss, bf, gfq, gfk, tri, place)


def _attn_kernel(q_ref, k_ref, vt_ref, km_ref, vmt_ref, o_ref, m_sc, acc_sc, *, tq, tk):
    i = pl.program_id(2)
    vchunk = vt_ref.shape[-1]

    def values(hh, block, n_keys):
        n = n_keys // vchunk
        return jnp.concatenate([vt_ref[0, hh, block * n + c] for c in range(n)], axis=1)

    def scores(hh, kc, cols=slice(None)):
        return lax.dot_general(kc, q_ref[0, hh, cols, :], _NT, preferred_element_type=F32)

    def update(hh, s, vt, cols=slice(None)):
        m_old = m_sc[hh, :, cols]
        m_new = jnp.maximum(m_old, jnp.max(s, axis=0, keepdims=True))
        p = jnp.exp2(s - m_new).astype(BF16)
        pv = jnp.dot(vt, p, preferred_element_type=F32)
        acc_sc[hh, :, cols] = jnp.exp2(m_old - m_new) * acc_sc[hh, :, cols] + pv
        m_sc[hh, :, cols] = m_new

    m_sc[...] = jnp.full(m_sc.shape, NEG_INF, F32)
    acc_sc[...] = jnp.zeros(acc_sc.shape, F32)

    def body(j, carry):
        off = pl.multiple_of(j * tk, tk)
        s = [scores(hh, k_ref[0, hh, pl.ds(off, tk), :]) for hh in range(2)]
        for hh in range(2):
            update(hh, s[hh], values(hh, j, tk))
        return carry

    lax.fori_loop(0, i * (tq // tk), body, 0)

    meta_valid = lax.broadcasted_iota(jnp.int32, (BLOCK, tq), 0) >= BLOCK - N_META
    s_meta = [jnp.where(meta_valid, scores(hh, km_ref[0, hh]), NEG_INF) for hh in range(2)]
    off = pl.multiple_of(i * tq, tq)
    for hh in range(2):
        update(hh, s_meta[hh], vmt_ref[0, hh, 0])
    vt = [values(hh, i, tq) for hh in range(2)]
    def diag_scores(c):
        nk = (c + 1) * DIAG_COLS
        cols = slice(c * DIAG_COLS, nk)
        causal = (lax.broadcasted_iota(jnp.int32, (nk, DIAG_COLS), 0)
                  <= lax.broadcasted_iota(jnp.int32, (nk, DIAG_COLS), 1) + c * DIAG_COLS)
        return [jnp.where(causal, scores(hh, k_ref[0, hh, pl.ds(off, nk), :], cols), NEG_INF)
                for hh in range(2)]

    n_blocks = tq // DIAG_COLS
    s_next = diag_scores(0)
    for c in range(n_blocks):
        s = s_next
        if c + 1 < n_blocks:
            s_next = diag_scores(c + 1)
        nk = (c + 1) * DIAG_COLS
        for hh in range(2):
            update(hh, s[hh], vt[hh][:, :nk], slice(c * DIAG_COLS, nk))

    a_e = acc_sc[0]
    a_o = acc_sc[1]
    top = lax.broadcasted_iota(jnp.int32, (LANES, tq), 0) < HALF
    out_t = jnp.where(top, a_e / a_e[HALF:HALF + 1, :], a_o / a_o[0:1, :])
    o_ref[0] = out_t.T.astype(BF16)


def _attention(q, k, vt, km, vmt):
    b, _, s, _ = q.shape
    tq, tk = ATT_TQ, ATT_TK
    vchunk = vt.shape[-1]
    assert tq % tk == 0 and s % tq == 0 and tk % vchunk == 0
    pairs = N_HEADS // 2
    return pl.pallas_call(
        functools.partial(_attn_kernel, tq=tq, tk=tk),
        grid=(b, pairs, s // tq),
        in_specs=[
            pl.BlockSpec((1, 2, tq, LANES), lambda bi, p, i: (bi, p, i, 0)),
            pl.BlockSpec((1, 2, s, LANES), lambda bi, p, i: (bi, p, 0, 0)),
            pl.BlockSpec((1, 2, s // vchunk, LANES, vchunk), lambda bi, p, i: (bi, p, 0, 0, 0)),
            pl.BlockSpec((1, 2, BLOCK, LANES), lambda bi, p, i: (0, p, 0, 0)),
            pl.BlockSpec((1, 2, 1, LANES, BLOCK), lambda bi, p, i: (0, p, 0, 0, 0)),
        ],
        out_specs=pl.BlockSpec((1, tq, LANES), lambda bi, p, i: (bi, i, p)),
        out_shape=jax.ShapeDtypeStruct((b, s, pairs * LANES), BF16),
        scratch_shapes=[pltpu.VMEM((2, 1, tq), F32), pltpu.VMEM((2, LANES, tq), F32)],
        compiler_params=pltpu.CompilerParams(
            dimension_semantics=("arbitrary", "arbitrary", "arbitrary"),
            vmem_limit_bytes=VMEM_LIMIT),
        name="attention",
    )(q, k, vt, km, vmt)


def _pad_lanes(x, offset=0):
    n = x.shape[-1]
    pad = [(0, 0)] * (x.ndim - 1) + [(offset, LANES - offset - n)]
    return jnp.pad(x, pad)


def _swap_halves(x):
    half = MLA_ROPE // 2
    return jnp.concatenate([x[..., half:], x[..., :half]], axis=-1)


def _rope_lane_gains(g, scale):
    g = g.astype(F32) * scale
    direct = _pad_lanes(g)
    paired = _pad_lanes(_swap_halves(g[MLA_NOPE:]), MLA_NOPE)
    return direct[None], paired[None]


def _rope_tables(pos):
    half = MLA_ROPE // 2
    inv_freq = 1.0 / (ROPE_THETA ** (jnp.arange(half, dtype=F32) / half))
    ang = pos.astype(F32)[:, None] * inv_freq[None, :]
    cos, sin = jnp.cos(ang), jnp.sin(ang)
    ones = jnp.ones((pos.shape[0], MLA_NOPE), F32)
    zeros = jnp.zeros((pos.shape[0], LANES - MLA_QK), F32)
    cc = jnp.concatenate([ones, cos, cos, zeros], axis=1)
    ss = jnp.concatenate([0.0 * ones, -sin, sin, zeros], axis=1)
    return cc, ss


def _layout_params(g_mix, w_in, g_cq, w_uq, g_ckv, w_ukv, g_q_mla, g_k_mla, b_forget,
                   g_q_fox, g_k_fox):
    o_kpe = MLA_Q_RANK + MLA_KV_RANK
    o_fox = o_kpe + MLA_ROPE
    o_fv = o_fox + 2 * FOX_HEADS * FOX_DIM
    o_fl = o_fv + FOX_HEADS * FOX_DIM
    w_kpe = w_in[:, o_kpe:o_fox]
    win = jnp.concatenate([
        w_in[:, :o_kpe],
        _pad_lanes(w_kpe, MLA_NOPE),
        _pad_lanes(_swap_halves(w_kpe), MLA_NOPE),
        w_in[:, o_fox:o_fv],
        _pad_lanes(w_in[:, o_fl:]),
    ], axis=1).astype(BF16)
    assert win.shape[1] == _D_IN_PAD
    wfvt = w_in[:, o_fv:o_fl].T.astype(BF16)

    uq = w_uq.reshape(MLA_Q_RANK, MLA_HEADS, MLA_QK)
    uq_direct = _pad_lanes(uq).reshape(MLA_Q_RANK, MLA_HEADS * LANES)
    uq_paired = _pad_lanes(_swap_halves(uq[..., MLA_NOPE:]), MLA_NOPE).reshape(MLA_Q_RANK, MLA_HEADS * LANES)
    wuq = jnp.concatenate([uq_direct, uq_paired], axis=1).astype(BF16)

    ukv = w_ukv.reshape(MLA_KV_RANK, MLA_HEADS, MLA_NOPE + MLA_V)
    wuk = _pad_lanes(ukv[..., :MLA_NOPE]).reshape(MLA_KV_RANK, MLA_HEADS * LANES).astype(BF16)
    wuvt = ukv[..., MLA_NOPE:].reshape(MLA_KV_RANK, MLA_HEADS * MLA_V).T.astype(BF16)

    gq, gqs = _rope_lane_gains(g_q_mla, LOG2E / math.sqrt(MLA_QK))
    gk, gks = _rope_lane_gains(g_k_mla, 1.0)
    gfq = jnp.tile(g_q_fox.astype(F32) * (LOG2E / math.sqrt(FOX_DIM)), 2)[None]
    gfk = jnp.tile(g_k_fox.astype(F32), 2)[None]
    bf = _pad_lanes(b_forget.astype(F32))[None]

    place = np.zeros((N_SPLIT * LANES, FOX_HEADS * LANES), np.float32)
    for h in range(FOX_HEADS):
        base = HALF if h % 2 == 0 else 0
        for j in range(N_SPLIT):
            place[j * LANES + h, h * LANES + base + j] = 1.0
    place = jnp.asarray(place, BF16)

    return (g_mix[None].astype(F32), win, g_cq[None].astype(F32), wuq, g_ckv[None].astype(F32),
            wuk, wuvt, wfvt, gq, gqs, gk, gks, bf, gfq, gfk, place)


def kernel(x, meta_tokens, g_ffn1, w1_gate, w1_up, w1_down, g_mix, w_in, g_cq, w_uq, g_ckv, w_ukv,
           g_q_mla, g_k_mla, b_forget, g_q_fox, g_k_fox, w_out, g_ffn2, w2_gate, w2_up, w2_down):
    b, s, d = x.shape
    depth = g_ffn1.shape[0]
    assert depth == 1 and d == D_MODEL
    l = 0

    ffn1 = (g_ffn1[l][None], w1_gate[l].astype(BF16), w1_up[l].astype(BF16), w1_down[l].astype(BF16))
    ffn2 = (g_ffn2[l][None], w2_gate[l].astype(BF16), w2_up[l].astype(BF16), w2_down[l].astype(BF16))
    consts = _layout_params(g_mix[l], w_in[l], g_cq[l], w_uq[l], g_ckv[l], w_ukv[l], g_q_mla[l],
                            g_k_mla[l], b_forget[l], g_q_fox[l], g_k_fox[l])

    pad = BLOCK - N_META
    h_meta = jnp.concatenate([jnp.zeros((pad, d), F32), meta_tokens.astype(F32)], axis=0)
    pos_meta = jnp.maximum(jnp.arange(BLOCK) - pad, 0)
    pos_tok = N_META + jnp.arange(s)

    h_meta = _ffn(h_meta, *ffn1)
    km, vmt = _proj(h_meta[None], consts, *_rope_tables(pos_meta), is_meta=True)

    h1 = _ffn(x.reshape(b * s, d), *ffn1)
    q, k, vt = _proj(h1.reshape(b, s, d), consts, *_rope_tables(pos_tok), is_meta=False)
    attn = _attention(q, k, vt, km, vmt)
    out = _ffn(h1, *ffn2, attn=attn.reshape(b * s, d), w_out=w_out[l].astype(BF16))
    return out.reshape(b, s, d)
```

```python
import functools
import math

import jax
import jax.numpy as jnp
import numpy as np
from jax import lax
from jax.experimental import pallas as pl
from jax.experimental.pallas import tpu as pltpu

F32 = jnp.float32
BF16 = jnp.bfloat16

D_MODEL = 1024
D_FF = 2816
N_META = 16
BLOCK = 128
EPS = 1e-6
NEG_INF = -1e30
MLA_HEADS = 8
MLA_Q_RANK = 256
MLA_KV_RANK = 128
MLA_NOPE = 64
MLA_ROPE = 32
MLA_QK = MLA_NOPE + MLA_ROPE
MLA_V = 64
ROPE_THETA = 10000.0
FOX_HEADS = 8
FOX_DIM = 64
N_HEADS = MLA_HEADS + FOX_HEADS
LANES = 128
HALF = LANES // 2
N_SPLIT = 3
LOG2E = math.log2(math.e)

FF_CHUNK = 256
FFN_TM = 512
PROJ_TM = 512
ATT_TQ = 2048
ATT_TK = 1024
DIAG_COLS = 256
VMEM_LIMIT = 56 * 1024 * 1024

_O_CQ = 0
_O_CKV = _O_CQ + MLA_Q_RANK
_O_KPE = _O_CKV + MLA_KV_RANK
_O_KPE_SW = _O_KPE + LANES
_O_FQ = _O_KPE_SW + LANES
_O_FK = _O_FQ + FOX_HEADS * FOX_DIM
_O_FL = _O_FK + FOX_HEADS * FOX_DIM
_D_IN_PAD = _O_FL + LANES

_NT = (((1,), (1,)), ((), ()))


def _const_spec(shape):
    nd = len(shape)
    return pl.BlockSpec(shape, lambda *_: (0,) * nd, pipeline_mode=pl.Buffered(1))


def _rms_scale(x, width):
    return lax.rsqrt(jnp.sum(x * x, axis=-1, keepdims=True) * (1.0 / width) + EPS)


def _split3(x):
    pieces = []
    r = x
    for _ in range(N_SPLIT):
        p = r.astype(BF16)
        pieces.append(p)
        r = r - p.astype(F32)
    return pieces


def _ffn_body(h, g_ref, wg_ref, wu_ref, wd_ref):
    n = (h * _rms_scale(h, D_MODEL) * g_ref[...]).astype(wg_ref.dtype)
    acc = jnp.zeros(h.shape, F32)
    for c in range(D_FF // FF_CHUNK):
        sl = slice(c * FF_CHUNK, (c + 1) * FF_CHUNK)
        g = jnp.dot(n, wg_ref[:, sl], preferred_element_type=F32)
        u = jnp.dot(n, wu_ref[:, sl], preferred_element_type=F32)
        a = (g * (1.0 / (1.0 + jnp.exp(-g))) * u).astype(wd_ref.dtype)
        acc = acc + jnp.dot(a, wd_ref[sl, :], preferred_element_type=F32)
    return h + 0.5 * acc


def _ffn_kernel(h_ref, g_ref, wg_ref, wu_ref, wd_ref, o_ref):
    o_ref[...] = _ffn_body(h_ref[...], g_ref, wg_ref, wu_ref, wd_ref)


def _mix_ffn_kernel(h_ref, a_ref, wo_ref, g_ref, wg_ref, wu_ref, wd_ref, o_ref):
    h = h_ref[...] + jnp.dot(a_ref[...], wo_ref[...], preferred_element_type=F32)
    o_ref[...] = _ffn_body(h, g_ref, wg_ref, wu_ref, wd_ref)


def _ffn(h, g, wg, wu, wd, attn=None, w_out=None):
    m = h.shape[0]
    tm = min(FFN_TM, m)
    row = pl.BlockSpec((tm, D_MODEL), lambda i: (i, 0))
    weights = [_const_spec((1, D_MODEL)), _const_spec((D_MODEL, D_FF)),
               _const_spec((D_MODEL, D_FF)), _const_spec((D_FF, D_MODEL))]
    if attn is None:
        kern, in_specs, args = _ffn_kernel, [row] + weights, (h, g, wg, wu, wd)
    else:
        kern = _mix_ffn_kernel
        in_specs = [row, row, _const_spec((D_MODEL, D_MODEL))] + weights
        args = (h, attn, w_out, g, wg, wu, wd)
    return pl.pallas_call(
        kern,
        grid=(m // tm,),
        in_specs=in_specs,
        out_specs=row,
        out_shape=jax.ShapeDtypeStruct((m, D_MODEL), F32),
        compiler_params=pltpu.CompilerParams(
            dimension_semantics=("arbitrary",), vmem_limit_bytes=VMEM_LIMIT),
        name="ffn_mix" if attn is not None else "ffn",
    )(*args)


def _half_rms_scale(x, lo):
    x2 = x * x
    s_lo = jnp.sum(jnp.where(lo, x2, 0.0), axis=-1, keepdims=True)
    s_hi = jnp.sum(jnp.where(lo, 0.0, x2), axis=-1, keepdims=True)
    return jnp.where(lo, lax.rsqrt(s_lo * (1.0 / FOX_DIM) + EPS), lax.rsqrt(s_hi * (1.0 / FOX_DIM) + EPS))


def _proj_kernel(h_ref, gmix_ref, win_ref, gcq_ref, wuq_ref, gckv_ref, wuk_ref, wuvt_ref, wfvt_ref,
                 gq_ref, gqs_ref, gk_ref, gks_ref, cc_ref, ss_ref, bf_ref, gfq_ref, gfk_ref,
                 tri_ref, place_ref, *rest, tm, is_meta):
    if is_meta:
        k_out, vt_out = rest
        q_out = carry_sc = None
    else:
        q_out, k_out, vt_out, carry_sc = rest

    lane = lax.broadcasted_iota(jnp.int32, (tm, LANES), 1)
    lo = lane < HALF
    bias_e = ((lane >= HALF) & (lane < HALF + N_SPLIT)).astype(F32)
    bias_o = (lane < N_SPLIT).astype(F32)
    ones_row = (lax.broadcasted_iota(jnp.int32, (HALF, tm), 0) == 0).astype(F32)

    def put_vt(head, blk):
        rows = [blk, ones_row] if head % 2 == 0 else [ones_row, blk]
        vt_out[0, head, 0] = jnp.concatenate(rows, axis=0).astype(BF16)

    h = h_ref[0]
    n = (h * _rms_scale(h, D_MODEL) * gmix_ref[...]).astype(BF16)
    proj = jnp.dot(n, win_ref[...], preferred_element_type=F32)

    c_q = proj[:, _O_CQ:_O_CKV]
    c_kv = proj[:, _O_CKV:_O_KPE]
    kpe = proj[:, _O_KPE:_O_KPE_SW]
    kpe_sw = proj[:, _O_KPE_SW:_O_FQ]

    cc = cc_ref[...]
    ss = ss_ref[...]

    ckvn = (c_kv * _rms_scale(c_kv, MLA_KV_RANK) * gckv_ref[...]).astype(BF16)
    kn = jnp.dot(ckvn, wuk_ref[...], preferred_element_type=F32)
    kc = kpe_sw * gks_ref[...] * ss
    for hd in range(MLA_HEADS):
        x = kn[:, hd * LANES:(hd + 1) * LANES] + kpe
        r = _rms_scale(x, MLA_QK)
        k_out[0, hd] = (r * (x * gk_ref[...] * cc + kc)).astype(BF16)
    vt = lax.dot_general(wuvt_ref[...], ckvn, _NT, preferred_element_type=F32)
    for hd in range(MLA_HEADS):
        put_vt(hd, vt[hd * MLA_V:(hd + 1) * MLA_V, :])
    if not is_meta:
        cqn = (c_q * _rms_scale(c_q, MLA_Q_RANK) * gcq_ref[...]).astype(BF16)
        qq = jnp.dot(cqn, wuq_ref[...], preferred_element_type=F32)
        for hd in range(MLA_HEADS):
            x = qq[:, hd * LANES:(hd + 1) * LANES]
            xs = qq[:, (MLA_HEADS + hd) * LANES:(MLA_HEADS + hd + 1) * LANES]
            r = _rms_scale(x, MLA_QK)
            q_out[0, hd] = (r * (x * gq_ref[...] * cc + xs * gqs_ref[...] * ss)).astype(BF16)

    fl = proj[:, _O_FL:_O_FL + LANES] + bf_ref[...]
    log_f = jnp.minimum(fl, 0.0) - jnp.log1p(jnp.exp(-jnp.abs(fl)))
    tri = tri_ref[...]
    cum = jnp.zeros((tm, LANES), F32)
    for piece in _split3(log_f):
        cum = cum + jnp.dot(tri, piece, preferred_element_type=F32)
    if is_meta:
        cum = cum - cum[tm - 1:tm, :]
    else:
        @pl.when(pl.program_id(1) == 0)
        def _():
            carry_sc[...] = jnp.zeros_like(carry_sc)
        cum = cum + carry_sc[0:1, :]
        carry_sc[...] = jnp.broadcast_to(cum[tm - 1:tm, :], carry_sc.shape)
    bias = jnp.dot(jnp.concatenate(_split3(cum * (-LOG2E)), axis=1), place_ref[...],
                   preferred_element_type=F32)
    fvt = lax.dot_general(wfvt_ref[...], n, _NT, preferred_element_type=F32)
    for p in range(FOX_HEADS // 2):
        he, ho = MLA_HEADS + 2 * p, MLA_HEADS + 2 * p + 1
        x = proj[:, _O_FK + p * LANES:_O_FK + (p + 1) * LANES]
        y = x * _half_rms_scale(x, lo) * gfk_ref[...]
        k_out[0, he] = jnp.where(lo, y, bias[:, (2 * p) * LANES:(2 * p + 1) * LANES]).astype(BF16)
        k_out[0, ho] = jnp.where(lo, bias[:, (2 * p + 1) * LANES:(2 * p + 2) * LANES], y).astype(BF16)
        put_vt(he, fvt[(2 * p) * FOX_DIM:(2 * p + 1) * FOX_DIM, :])
        put_vt(ho, fvt[(2 * p + 1) * FOX_DIM:(2 * p + 2) * FOX_DIM, :])
        if not is_meta:
            x = proj[:, _O_FQ + p * LANES:_O_FQ + (p + 1) * LANES]
            y = x * _half_rms_scale(x, lo) * gfq_ref[...]
            q_out[0, he] = jnp.where(lo, y, bias_e).astype(BF16)
            q_out[0, ho] = jnp.where(lo, bias_o, y).astype(BF16)


def _proj(h, consts, cc, ss, *, is_meta):
    b, s, _ = h.shape
    tm = min(PROJ_TM, s)
    (gmix, win, gcq, wuq, gckv, wuk, wuvt, wfvt, gq, gqs, gk, gks, bf, gfq, gfk, place) = consts
    tri = jnp.tril(jnp.ones((tm, tm), BF16))
    row_in = pl.BlockSpec((1, tm, D_MODEL), lambda bi, i: (bi, i, 0))
    tab = pl.BlockSpec((tm, LANES), lambda bi, i: (i, 0))
    head_out = pl.BlockSpec((1, N_HEADS, tm, LANES), lambda bi, i: (bi, 0, i, 0))
    head_shape = jax.ShapeDtypeStruct((b, N_HEADS, s, LANES), BF16)
    vt_out = pl.BlockSpec((1, N_HEADS, 1, LANES, tm), lambda bi, i: (bi, 0, i, 0, 0))
    vt_shape = jax.ShapeDtypeStruct((b, N_HEADS, s // tm, LANES, tm), BF16)
    vec = _const_spec((1, LANES))
    in_specs = [row_in, _const_spec((1, D_MODEL)), _const_spec(win.shape),
                _const_spec((1, MLA_Q_RANK)), _const_spec(wuq.shape),
                _const_spec((1, MLA_KV_RANK)), _const_spec(wuk.shape),
                _const_spec(wuvt.shape), _const_spec(wfvt.shape),
                vec, vec, vec, vec, tab, tab, vec, vec, vec,
                _const_spec((tm, tm)), _const_spec(place.shape)]
    if is_meta:
        out_specs, out_shape, scratch = [head_out, vt_out], [head_shape, vt_shape], []
    else:
        out_specs, out_shape = [head_out, head_out, vt_out], [head_shape, head_shape, vt_shape]
        scratch = [pltpu.VMEM((8, LANES), F32)]
    return pl.pallas_call(
        functools.partial(_proj_kernel, tm=tm, is_meta=is_meta),
        grid=(b, s // tm),
        in_specs=in_specs,
        out_specs=out_specs,
        out_shape=out_shape,
        scratch_shapes=scratch,
        compiler_params=pltpu.CompilerParams(
            dimension_semantics=("arbitrary", "arbitrary"), vmem_limit_bytes=VMEM_LIMIT),
        name="proj_meta" if is_meta else "proj",
    )(h, gmix, win, gcq, wuq, gckv, wuk, wuvt, wfvt, gq, gqs, gk, gks, cc, ss, bf, gfq, gfk, tri, place)


def _attn_kernel(q_ref, k_ref, vt_ref, km_ref, vmt_ref, o_ref, m_sc, acc_sc, *, tq, tk):
    i = pl.program_id(2)
    vchunk = vt_ref.shape[-1]

    def values(hh, block, n_keys):
        n = n_keys // vchunk
        return jnp.concatenate([vt_ref[0, hh, block * n + c] for c in range(n)], axis=1)

    def scores(hh, kc, cols=slice(None)):
        return lax.dot_general(kc, q_ref[0, hh, cols, :], _NT, preferred_element_type=F32)

    def update(hh, s, vt, cols=slice(None)):
        m_old = m_sc[hh, :, cols]
        m_new = jnp.maximum(m_old, jnp.max(s, axis=0, keepdims=True))
        p = jnp.exp2(s - m_new).astype(BF16)
        pv = jnp.dot(vt, p, preferred_element_type=F32)
        acc_sc[hh, :, cols] = jnp.exp2(m_old - m_new) * acc_sc[hh, :, cols] + pv
        m_sc[hh, :, cols] = m_new

    m_sc[...] = jnp.full(m_sc.shape, NEG_INF, F32)
    acc_sc[...] = jnp.zeros(acc_sc.shape, F32)

    def body(j, carry):
        off = pl.multiple_of(j * tk, tk)
        s = [scores(hh, k_ref[0, hh, pl.ds(off, tk), :]) for hh in range(2)]
        for hh in range(2):
            update(hh, s[hh], values(hh, j, tk))
        return carry

    lax.fori_loop(0, i * (tq // tk), body, 0)

    meta_valid = lax.broadcasted_iota(jnp.int32, (BLOCK, tq), 0) >= BLOCK - N_META
    s_meta = [jnp.where(meta_valid, scores(hh, km_ref[0, hh]), NEG_INF) for hh in range(2)]
    off = pl.multiple_of(i * tq, tq)
    for hh in range(2):
        update(hh, s_meta[hh], vmt_ref[0, hh, 0])
    vt = [values(hh, i, tq) for hh in range(2)]
    def diag_scores(c):
        nk = (c + 1) * DIAG_COLS
        cols = slice(c * DIAG_COLS, nk)
        causal = (lax.broadcasted_iota(jnp.int32, (nk, DIAG_COLS), 0)
                  <= lax.broadcasted_iota(jnp.int32, (nk, DIAG_COLS), 1) + c * DIAG_COLS)
        return [jnp.where(causal, scores(hh, k_ref[0, hh, pl.ds(off, nk), :], cols), NEG_INF)
                for hh in range(2)]

    n_blocks = tq // DIAG_COLS
    s_next = diag_scores(0)
    for c in range(n_blocks):
        s = s_next
        if c + 1 < n_blocks:
            s_next = diag_scores(c + 1)
        nk = (c + 1) * DIAG_COLS
        for hh in range(2):
            update(hh, s[hh], vt[hh][:, :nk], slice(c * DIAG_COLS, nk))

    a_e = acc_sc[0]
    a_o = acc_sc[1]
    top = lax.broadcasted_iota(jnp.int32, (LANES, tq), 0) < HALF
    out_t = jnp.where(top, a_e / a_e[HALF:HALF + 1, :], a_o / a_o[0:1, :])
    o_ref[0] = out_t.T.astype(BF16)


def _attention(q, k, vt, km, vmt):
    b, _, s, _ = q.shape
    tq, tk = ATT_TQ, ATT_TK
    vchunk = vt.shape[-1]
    assert tq % tk == 0 and s % tq == 0 and tk % vchunk == 0
    pairs = N_HEADS // 2
    return pl.pallas_call(
        functools.partial(_attn_kernel, tq=tq, tk=tk),
        grid=(b, pairs, s // tq),
        in_specs=[
            pl.BlockSpec((1, 2, tq, LANES), lambda bi, p, i: (bi, p, i, 0)),
            pl.BlockSpec((1, 2, s, LANES), lambda bi, p, i: (bi, p, 0, 0)),
            pl.BlockSpec((1, 2, s // vchunk, LANES, vchunk), lambda bi, p, i: (bi, p, 0, 0, 0)),
            pl.BlockSpec((1, 2, BLOCK, LANES), lambda bi, p, i: (0, p, 0, 0)),
            pl.BlockSpec((1, 2, 1, LANES, BLOCK), lambda bi, p, i: (0, p, 0, 0, 0)),
        ],
        out_specs=pl.BlockSpec((1, tq, LANES), lambda bi, p, i: (bi, i, p)),
        out_shape=jax.ShapeDtypeStruct((b, s, pairs * LANES), BF16),
        scratch_shapes=[pltpu.VMEM((2, 1, tq), F32), pltpu.VMEM((2, LANES, tq), F32)],
        compiler_params=pltpu.CompilerParams(
            dimension_semantics=("arbitrary", "arbitrary", "arbitrary"),
            vmem_limit_bytes=VMEM_LIMIT),
        name="attention",
    )(q, k, vt, km, vmt)


def _pad_lanes(x, offset=0):
    n = x.shape[-1]
    pad = [(0, 0)] * (x.ndim - 1) + [(offset, LANES - offset - n)]
    return jnp.pad(x, pad)


def _swap_halves(x):
    half = MLA_ROPE // 2
    return jnp.concatenate([x[..., half:], x[..., :half]], axis=-1)


def _rope_lane_gains(g, scale):
    g = g.astype(F32) * scale
    direct = _pad_lanes(g)
    paired = _pad_lanes(_swap_halves(g[MLA_NOPE:]), MLA_NOPE)
    return direct[None], paired[None]


def _rope_tables(pos):
    half = MLA_ROPE // 2
    inv_freq = 1.0 / (ROPE_THETA ** (jnp.arange(half, dtype=F32) / half))
    ang = pos.astype(F32)[:, None] * inv_freq[None, :]
    cos, sin = jnp.cos(ang), jnp.sin(ang)
    ones = jnp.ones((pos.shape[0], MLA_NOPE), F32)
    zeros = jnp.zeros((pos.shape[0], LANES - MLA_QK), F32)
    cc = jnp.concatenate([ones, cos, cos, zeros], axis=1)
    ss = jnp.concatenate([0.0 * ones, -sin, sin, zeros], axis=1)
    return cc, ss


def _layout_params(g_mix, w_in, g_cq, w_uq, g_ckv, w_ukv, g_q_mla, g_k_mla, b_forget,
                   g_q_fox, g_k_fox):
    o_kpe = MLA_Q_RANK + MLA_KV_RANK
    o_fox = o_kpe + MLA_ROPE
    o_fv = o_fox + 2 * FOX_HEADS * FOX_DIM
    o_fl = o_fv + FOX_HEADS * FOX_DIM
    w_kpe = w_in[:, o_kpe:o_fox]
    win = jnp.concatenate([
        w_in[:, :o_kpe],
        _pad_lanes(w_kpe, MLA_NOPE),
        _pad_lanes(_swap_halves(w_kpe), MLA_NOPE),
        w_in[:, o_fox:o_fv],
        _pad_lanes(w_in[:, o_fl:]),
    ], axis=1).astype(BF16)
    assert win.shape[1] == _D_IN_PAD
    wfvt = w_in[:, o_fv:o_fl].T.astype(BF16)

    uq = w_uq.reshape(MLA_Q_RANK, MLA_HEADS, MLA_QK)
    uq_direct = _pad_lanes(uq).reshape(MLA_Q_RANK, MLA_HEADS * LANES)
    uq_paired = _pad_lanes(_swap_halves(uq[..., MLA_NOPE:]), MLA_NOPE).reshape(MLA_Q_RANK, MLA_HEADS * LANES)
    wuq = jnp.concatenate([uq_direct, uq_paired], axis=1).astype(BF16)

    ukv = w_ukv.reshape(MLA_KV_RANK, MLA_HEADS, MLA_NOPE + MLA_V)
    wuk = _pad_lanes(ukv[..., :MLA_NOPE]).reshape(MLA_KV_RANK, MLA_HEADS * LANES).astype(BF16)
    wuvt = ukv[..., MLA_NOPE:].reshape(MLA_KV_RANK, MLA_HEADS * MLA_V).T.astype(BF16)

    gq, gqs = _rope_lane_gains(g_q_mla, LOG2E / math.sqrt(MLA_QK))
    gk, gks = _rope_lane_gains(g_k_mla, 1.0)
    gfq = jnp.tile(g_q_fox.astype(F32) * (LOG2E / math.sqrt(FOX_DIM)), 2)[None]
    gfk = jnp.tile(g_k_fox.astype(F32), 2)[None]
    bf = _pad_lanes(b_forget.astype(F32))[None]

    place = np.zeros((N_SPLIT * LANES, FOX_HEADS * LANES), np.float32)
    for h in range(FOX_HEADS):
        base = HALF if h % 2 == 0 else 0
        for j in range(N_SPLIT):
            place[j * LANES + h, h * LANES + base + j] = 1.0
    place = jnp.asarray(place, BF16)

    return (g_mix[None].astype(F32), win, g_cq[None].astype(F32), wuq, g_ckv[None].astype(F32),
            wuk, wuvt, wfvt, gq, gqs, gk, gks, bf, gfq, gfk, place)


def kernel(x, meta_tokens, g_ffn1, w1_gate, w1_up, w1_down, g_mix, w_in, g_cq, w_uq, g_ckv, w_ukv,
           g_q_mla, g_k_mla, b_forget, g_q_fox, g_k_fox, w_out, g_ffn2, w2_gate, w2_up, w2_down):
    b, s, d = x.shape
    depth = g_ffn1.shape[0]
    assert depth == 1 and d == D_MODEL
    l = 0

    ffn1 = (g_ffn1[l][None], w1_gate[l], w1_up[l], w1_down[l])
    ffn2 = (g_ffn2[l][None], w2_gate[l], w2_up[l], w2_down[l])
    consts = _layout_params(g_mix[l], w_in[l], g_cq[l], w_uq[l], g_ckv[l], w_ukv[l], g_q_mla[l],
                            g_k_mla[l], b_forget[l], g_q_fox[l], g_k_fox[l])

    pad = BLOCK - N_META
    h_meta = jnp.concatenate([jnp.zeros((pad, d), F32), meta_tokens.astype(F32)], axis=0)
    pos_meta = jnp.maximum(jnp.arange(BLOCK) - pad, 0)
    pos_tok = N_META + jnp.arange(s)

    h_meta = _ffn(h_meta, *ffn1)
    km, vmt = _proj(h_meta[None], consts, *_rope_tables(pos_meta), is_meta=True)

    h1 = _ffn(x.reshape(b * s, d), *ffn1)
    q, k, vt = _proj(h1.reshape(b, s, d), consts, *_rope_tables(pos_tok), is_meta=False)
    attn = _attention(q, k, vt, km, vmt)
    out = _ffn(h1, *ffn2, attn=attn.reshape(b * s, d), w_out=w_out[l].astype(BF16))
    return out.reshape(b, s, d)
```

```python
import functools
import math

import jax
import jax.numpy as jnp
import numpy as np
from jax import lax
from jax.experimental import pallas as pl
from jax.experimental.pallas import tpu as pltpu

F32 = jnp.float32
BF16 = jnp.bfloat16

D_MODEL = 1024
D_FF = 2816
N_META = 16
BLOCK = 128
EPS = 1e-6
NEG_INF = -1e30
MLA_HEADS = 8
MLA_Q_RANK = 256
MLA_KV_RANK = 128
MLA_NOPE = 64
MLA_ROPE = 32
MLA_QK = MLA_NOPE + MLA_ROPE
MLA_V = 64
ROPE_THETA = 10000.0
FOX_HEADS = 8
FOX_DIM = 64
N_HEADS = MLA_HEADS + FOX_HEADS
LANES = 128
HALF = LANES // 2
N_SPLIT = 3
LOG2E = math.log2(math.e)

FF_CHUNK = 256
FFN_TM = 512
PROJ_TM = 512
ATT_TQ = 2048
ATT_TK = 1024
DIAG_COLS = 256
VMEM_LIMIT = 56 * 1024 * 1024

_O_CQ = 0
_O_CKV = _O_CQ + MLA_Q_RANK
_O_KPE = _O_CKV + MLA_KV_RANK
_O_KPE_SW = _O_KPE + LANES
_O_FQ = _O_KPE_SW + LANES
_O_FK = _O_FQ + FOX_HEADS * FOX_DIM
_O_FL = _O_FK + FOX_HEADS * FOX_DIM
_D_IN_PAD = _O_FL + LANES

_NT = (((1,), (1,)), ((), ()))


def _const_spec(shape):
    nd = len(shape)
    return pl.BlockSpec(shape, lambda *_: (0,) * nd, pipeline_mode=pl.Buffered(1))


def _rms_scale(x, width):
    return lax.rsqrt(jnp.sum(x * x, axis=-1, keepdims=True) * (1.0 / width) + EPS)


def _split3(x):
    pieces = []
    r = x
    for _ in range(N_SPLIT):
        p = r.astype(BF16)
        pieces.append(p)
        r = r - p.astype(F32)
    return pieces


def _ffn_body(h, g_ref, wg_ref, wu_ref, wd_ref):
    n = (h * _rms_scale(h, D_MODEL) * g_ref[...]).astype(wg_ref.dtype)
    acc = jnp.zeros(h.shape, F32)
    for c in range(D_FF // FF_CHUNK):
        sl = slice(c * FF_CHUNK, (c + 1) * FF_CHUNK)
        g = jnp.dot(n, wg_ref[:, sl], preferred_element_type=F32)
        u = jnp.dot(n, wu_ref[:, sl], preferred_element_type=F32)
        a = (g * (1.0 / (1.0 + jnp.exp(-g))) * u).astype(wd_ref.dtype)
        acc = acc + jnp.dot(a, wd_ref[sl, :], preferred_element_type=F32)
    return h + 0.5 * acc


def _ffn_meta_kernel(h_ref, hm_ref, g_ref, wg_ref, wu_ref, wd_ref, o_ref, om_ref):
    o_ref[...] = _ffn_body(h_ref[...], g_ref, wg_ref, wu_ref, wd_ref)

    @pl.when(pl.program_id(0) == 0)
    def _():
        om_ref[...] = _ffn_body(hm_ref[...], g_ref, wg_ref, wu_ref, wd_ref)


def _mix_ffn_kernel(h_ref, a_ref, wo_ref, g_ref, wg_ref, wu_ref, wd_ref, o_ref):
    h = h_ref[...] + jnp.dot(a_ref[...], wo_ref[...], preferred_element_type=F32)
    o_ref[...] = _ffn_body(h, g_ref, wg_ref, wu_ref, wd_ref)


def _ffn_call(kern, name, m, row_args, const_args, n_meta_rows=0):
    tm = FFN_TM
    assert m % tm == 0
    row = pl.BlockSpec((tm, D_MODEL), lambda i: (i, 0))
    out_specs, out_shape = row, jax.ShapeDtypeStruct((m, D_MODEL), F32)
    if n_meta_rows:
        out_specs = [row, pl.BlockSpec((n_meta_rows, D_MODEL), lambda i: (0, 0))]
        out_shape = [out_shape, jax.ShapeDtypeStruct((n_meta_rows, D_MODEL), F32)]
    return pl.pallas_call(
        kern,
        grid=(m // tm,),
        in_specs=[row] * len(row_args) + [_const_spec(a.shape) for a in const_args],
        out_specs=out_specs,
        out_shape=out_shape,
        compiler_params=pltpu.CompilerParams(
            dimension_semantics=("arbitrary",), vmem_limit_bytes=VMEM_LIMIT),
        name=name,
    )(*row_args, *const_args)


def _ffn_with_meta(h, h_meta, g, wg, wu, wd):
    return _ffn_call(_ffn_meta_kernel, "ffn", h.shape[0], (h,), (h_meta, g, wg, wu, wd),
                     n_meta_rows=h_meta.shape[0])


def _mix_ffn(h, attn, w_out, g, wg, wu, wd):
    return _ffn_call(_mix_ffn_kernel, "ffn_mix", h.shape[0], (h, attn), (w_out, g, wg, wu, wd))


def _half_rms_scale(x, lo):
    x2 = x * x
    s_lo = jnp.sum(jnp.where(lo, x2, 0.0), axis=-1, keepdims=True)
    s_hi = jnp.sum(jnp.where(lo, 0.0, x2), axis=-1, keepdims=True)
    return jnp.where(lo, lax.rsqrt(s_lo * (1.0 / FOX_DIM) + EPS), lax.rsqrt(s_hi * (1.0 / FOX_DIM) + EPS))


def _proj_kernel(h_ref, gmix_ref, win_ref, gcq_ref, wuq_ref, gckv_ref, wuk_ref, wuvt_ref, wfvt_ref,
                 gq_ref, gqs_ref, gk_ref, gks_ref, cc_ref, ss_ref, bf_ref, gfq_ref, gfk_ref,
                 tri_ref, place_ref, *rest, tm, is_meta):
    if is_meta:
        k_out, vt_out = rest
        q_out = carry_sc = None
    else:
        q_out, k_out, vt_out, carry_sc = rest

    lane = lax.broadcasted_iota(jnp.int32, (tm, LANES), 1)
    lo = lane < HALF
    bias_e = ((lane >= HALF) & (lane < HALF + N_SPLIT)).astype(F32)
    bias_o = (lane < N_SPLIT).astype(F32)
    ones_row = (lax.broadcasted_iota(jnp.int32, (HALF, tm), 0) == 0).astype(F32)

    def put_vt(head, blk):
        rows = [blk, ones_row] if head % 2 == 0 else [ones_row, blk]
        vt_out[0, head, 0] = jnp.concatenate(rows, axis=0).astype(BF16)

    h = h_ref[0]
    n = (h * _rms_scale(h, D_MODEL) * gmix_ref[...]).astype(BF16)
    proj = jnp.dot(n, win_ref[...], preferred_element_type=F32)

    c_q = proj[:, _O_CQ:_O_CKV]
    c_kv = proj[:, _O_CKV:_O_KPE]
    kpe = proj[:, _O_KPE:_O_KPE_SW]
    kpe_sw = proj[:, _O_KPE_SW:_O_FQ]

    cc = cc_ref[...]
    ss = ss_ref[...]

    ckvn = (c_kv * _rms_scale(c_kv, MLA_KV_RANK) * gckv_ref[...]).astype(BF16)
    kn = jnp.dot(ckvn, wuk_ref[...], preferred_element_type=F32)
    kc = kpe_sw * gks_ref[...] * ss
    for hd in range(MLA_HEADS):
        x = kn[:, hd * LANES:(hd + 1) * LANES] + kpe
        r = _rms_scale(x, MLA_QK)
        k_out[0, hd] = (r * (x * gk_ref[...] * cc + kc)).astype(BF16)
    vt = lax.dot_general(wuvt_ref[...], ckvn, _NT, preferred_element_type=F32)
    for hd in range(MLA_HEADS):
        put_vt(hd, vt[hd * MLA_V:(hd + 1) * MLA_V, :])
    if not is_meta:
        cqn = (c_q * _rms_scale(c_q, MLA_Q_RANK) * gcq_ref[...]).astype(BF16)
        qq = jnp.dot(cqn, wuq_ref[...], preferred_element_type=F32)
        for hd in range(MLA_HEADS):
            x = qq[:, hd * LANES:(hd + 1) * LANES]
            xs = qq[:, (MLA_HEADS + hd) * LANES:(MLA_HEADS + hd + 1) * LANES]
            r = _rms_scale(x, MLA_QK)
            q_out[0, hd] = (r * (x * gq_ref[...] * cc + xs * gqs_ref[...] * ss)).astype(BF16)

    fl = proj[:, _O_FL:_O_FL + LANES] + bf_ref[...]
    log_f = jnp.minimum(fl, 0.0) - jnp.log1p(jnp.exp(-jnp.abs(fl)))
    tri = tri_ref[...]
    pieces = _split3(log_f)
    assert N_SPLIT == 3
    pair = jnp.dot(tri, jnp.concatenate(pieces[:2], axis=1), preferred_element_type=F32)
    cum = pair[:, :LANES] + pair[:, LANES:] + jnp.dot(tri, pieces[2], preferred_element_type=F32)
    if is_meta:
        cum = cum - cum[tm - 1:tm, :]
    else:
        @pl.when(pl.program_id(1) == 0)
        def _():
            carry_sc[...] = jnp.zeros_like(carry_sc)
        cum = cum + carry_sc[0:1, :]
        carry_sc[...] = jnp.broadcast_to(cum[tm - 1:tm, :], carry_sc.shape)
    bias = jnp.dot(jnp.concatenate(_split3(cum * (-LOG2E)), axis=1), place_ref[...],
                   preferred_element_type=F32)
    fvt = lax.dot_general(wfvt_ref[...], n, _NT, preferred_element_type=F32)
    for p in range(FOX_HEADS // 2):
        he, ho = MLA_HEADS + 2 * p, MLA_HEADS + 2 * p + 1
        x = proj[:, _O_FK + p * LANES:_O_FK + (p + 1) * LANES]
        y = x * _half_rms_scale(x, lo) * gfk_ref[...]
        pair_bias = bias[:, p * LANES:(p + 1) * LANES]
        k_out[0, he] = jnp.where(lo, y, pair_bias).astype(BF16)
        k_out[0, ho] = jnp.where(lo, pair_bias, y).astype(BF16)
        put_vt(he, fvt[(2 * p) * FOX_DIM:(2 * p + 1) * FOX_DIM, :])
        put_vt(ho, fvt[(2 * p + 1) * FOX_DIM:(2 * p + 2) * FOX_DIM, :])
        if not is_meta:
            x = proj[:, _O_FQ + p * LANES:_O_FQ + (p + 1) * LANES]
            y = x * _half_rms_scale(x, lo) * gfq_ref[...]
            q_out[0, he] = jnp.where(lo, y, bias_e).astype(BF16)
            q_out[0, ho] = jnp.where(lo, bias_o, y).astype(BF16)


def _proj(h, consts, cc, ss, *, is_meta):
    b, s, _ = h.shape
    tm = min(PROJ_TM, s)
    (gmix, win, gcq, wuq, gckv, wuk, wuvt, wfvt, gq, gqs, gk, gks, bf, gfq, gfk, place) = consts
    tri = jnp.tril(jnp.ones((tm, tm), BF16))
    row_in = pl.BlockSpec((1, tm, D_MODEL), lambda bi, i: (bi, i, 0))
    tab = pl.BlockSpec((tm, LANES), lambda bi, i: (i, 0))
    head_out = pl.BlockSpec((1, N_HEADS, tm, LANES), lambda bi, i: (bi, 0, i, 0))
    head_shape = jax.ShapeDtypeStruct((b, N_HEADS, s, LANES), BF16)
    vt_out = pl.BlockSpec((1, N_HEADS, 1, LANES, tm), lambda bi, i: (bi, 0, i, 0, 0))
    vt_shape = jax.ShapeDtypeStruct((b, N_HEADS, s // tm, LANES, tm), BF16)
    vec = _const_spec((1, LANES))
    in_specs = [row_in, _const_spec((1, D_MODEL)), _const_spec(win.shape),
                _const_spec((1, MLA_Q_RANK)), _const_spec(wuq.shape),
                _const_spec((1, MLA_KV_RANK)), _const_spec(wuk.shape),
                _const_spec(wuvt.shape), _const_spec(wfvt.shape),
                vec, vec, vec, vec, tab, tab, vec, vec, vec,
                _const_spec((tm, tm)), _const_spec(place.shape)]
    if is_meta:
        out_specs, out_shape, scratch = [head_out, vt_out], [head_shape, vt_shape], []
    else:
        out_specs, out_shape = [head_out, head_out, vt_out], [head_shape, head_shape, vt_shape]
        scratch = [pltpu.VMEM((8, LANES), F32)]
    return pl.pallas_call(
        functools.partial(_proj_kernel, tm=tm, is_meta=is_meta),
        grid=(b, s // tm),
        in_specs=in_specs,
        out_specs=out_specs,
        out_shape=out_shape,
        scratch_shapes=scratch,
        compiler_params=pltpu.CompilerParams(
            dimension_semantics=("arbitrary", "arbitrary"), vmem_limit_bytes=VMEM_LIMIT),
        name="proj_meta" if is_meta else "proj",
    )(h, gmix, win, gcq, wuq, gckv, wuk, wuvt, wfvt, gq, gqs, gk, gks, cc, ss, bf, gfq, gfk, tri, place)


def _attn_kernel(q_ref, k_ref, vt_ref, km_ref, vmt_ref, o_ref, m_sc, acc_sc, *, tq, tk):
    i = pl.program_id(2)
    vchunk = vt_ref.shape[-1]

    def values(hh, block, n_keys):
        n = n_keys // vchunk
        return jnp.concatenate([vt_ref[0, hh, block * n + c] for c in range(n)], axis=1)

    def scores(hh, kc, cols=slice(None)):
        return lax.dot_general(kc, q_ref[0, hh, cols, :], _NT, preferred_element_type=F32)

    def update(hh, s, vt, cols=slice(None)):
        m_old = m_sc[hh, :, cols]
        m_new = jnp.maximum(m_old, jnp.max(s, axis=0, keepdims=True))
        p = jnp.exp2(s - m_new).astype(BF16)
        pv = jnp.dot(vt, p, preferred_element_type=F32)
        acc_sc[hh, :, cols] = jnp.exp2(m_old - m_new) * acc_sc[hh, :, cols] + pv
        m_sc[hh, :, cols] = m_new

    m_sc[...] = jnp.full(m_sc.shape, NEG_INF, F32)
    acc_sc[...] = jnp.zeros(acc_sc.shape, F32)

    def body(j, carry):
        off = pl.multiple_of(j * tk, tk)
        s = [scores(hh, k_ref[0, hh, pl.ds(off, tk), :]) for hh in range(2)]
        for hh in range(2):
            update(hh, s[hh], values(hh, j, tk))
        return carry

    lax.fori_loop(0, i * (tq // tk), body, 0)

    meta_valid = lax.broadcasted_iota(jnp.int32, (BLOCK, tq), 0) >= BLOCK - N_META
    s_meta = [jnp.where(meta_valid, scores(hh, km_ref[0, hh]), NEG_INF) for hh in range(2)]
    off = pl.multiple_of(i * tq, tq)
    for hh in range(2):
        update(hh, s_meta[hh], vmt_ref[0, hh, 0])
    vt = [values(hh, i, tq) for hh in range(2)]
    def diag_scores(c):
        nk = (c + 1) * DIAG_COLS
        cols = slice(c * DIAG_COLS, nk)
        causal = (lax.broadcasted_iota(jnp.int32, (nk, DIAG_COLS), 0)
                  <= lax.broadcasted_iota(jnp.int32, (nk, DIAG_COLS), 1) + c * DIAG_COLS)
        return [jnp.where(causal, scores(hh, k_ref[0, hh, pl.ds(off, nk), :], cols), NEG_INF)
                for hh in range(2)]

    n_blocks = tq // DIAG_COLS
    s_next = diag_scores(0)
    for c in range(n_blocks):
        s = s_next
        if c + 1 < n_blocks:
            s_next = diag_scores(c + 1)
        nk = (c + 1) * DIAG_COLS
        for hh in range(2):
            update(hh, s[hh], vt[hh][:, :nk], slice(c * DIAG_COLS, nk))

    a_e = acc_sc[0]
    a_o = acc_sc[1]
    top = lax.broadcasted_iota(jnp.int32, (LANES, tq), 0) < HALF
    out_t = jnp.where(top, a_e / a_e[HALF:HALF + 1, :], a_o / a_o[0:1, :])
    o_ref[0] = out_t.T.astype(BF16)


def _attention(q, k, vt, km, vmt):
    b, _, s, _ = q.shape
    tq, tk = ATT_TQ, ATT_TK
    vchunk = vt.shape[-1]
    assert tq % tk == 0 and s % tq == 0 and tk % vchunk == 0
    pairs = N_HEADS // 2
    return pl.pallas_call(
        functools.partial(_attn_kernel, tq=tq, tk=tk),
        grid=(b, pairs, s // tq),
        in_specs=[
            pl.BlockSpec((1, 2, tq, LANES), lambda bi, p, i: (bi, p, i, 0)),
            pl.BlockSpec((1, 2, s, LANES), lambda bi, p, i: (bi, p, 0, 0)),
            pl.BlockSpec((1, 2, s // vchunk, LANES, vchunk), lambda bi, p, i: (bi, p, 0, 0, 0)),
            pl.BlockSpec((1, 2, BLOCK, LANES), lambda bi, p, i: (0, p, 0, 0)),
            pl.BlockSpec((1, 2, 1, LANES, BLOCK), lambda bi, p, i: (0, p, 0, 0, 0)),
        ],
        out_specs=pl.BlockSpec((1, tq, LANES), lambda bi, p, i: (bi, i, p)),
        out_shape=jax.ShapeDtypeStruct((b, s, pairs * LANES), BF16),
        scratch_shapes=[pltpu.VMEM((2, 1, tq), F32), pltpu.VMEM((2, LANES, tq), F32)],
        compiler_params=pltpu.CompilerParams(
            dimension_semantics=("arbitrary", "arbitrary", "arbitrary"),
            vmem_limit_bytes=VMEM_LIMIT),
        name="attention",
    )(q, k, vt, km, vmt)


def _pad_lanes(x, offset=0):
    n = x.shape[-1]
    pad = [(0, 0)] * (x.ndim - 1) + [(offset, LANES - offset - n)]
    return jnp.pad(x, pad)


def _swap_halves(x):
    half = MLA_ROPE // 2
    return jnp.concatenate([x[..., half:], x[..., :half]], axis=-1)


def _rope_lane_gains(g, scale):
    g = g.astype(F32) * scale
    direct = _pad_lanes(g)
    paired = _pad_lanes(_swap_halves(g[MLA_NOPE:]), MLA_NOPE)
    return direct[None], paired[None]


def _rope_tables(pos):
    half = MLA_ROPE // 2
    inv_freq = 1.0 / (ROPE_THETA ** (jnp.arange(half, dtype=F32) / half))
    ang = inv_freq[:, None] * pos.astype(F32)[None, :]
    cos, sin = jnp.cos(ang), jnp.sin(ang)
    ones = jnp.ones((MLA_NOPE, pos.shape[0]), F32)
    zeros = jnp.zeros((LANES - MLA_QK, pos.shape[0]), F32)
    cc = jnp.concatenate([ones, cos, cos, zeros], axis=0).T
    ss = jnp.concatenate([0.0 * ones, -sin, sin, zeros], axis=0).T
    return cc, ss


def _layout_params(g_mix, w_in, g_cq, w_uq, g_ckv, w_ukv, g_q_mla, g_k_mla, b_forget,
                   g_q_fox, g_k_fox):
    o_kpe = MLA_Q_RANK + MLA_KV_RANK
    o_fox = o_kpe + MLA_ROPE
    o_fv = o_fox + 2 * FOX_HEADS * FOX_DIM
    o_fl = o_fv + FOX_HEADS * FOX_DIM
    w_kpe = w_in[:, o_kpe:o_fox]
    win = jnp.concatenate([
        w_in[:, :o_kpe],
        _pad_lanes(w_kpe, MLA_NOPE),
        _pad_lanes(_swap_halves(w_kpe), MLA_NOPE),
        w_in[:, o_fox:o_fv],
        _pad_lanes(w_in[:, o_fl:]),
    ], axis=1).astype(BF16)
    assert win.shape[1] == _D_IN_PAD
    wfvt = w_in[:, o_fv:o_fl].T.astype(BF16)

    uq = w_uq.reshape(MLA_Q_RANK, MLA_HEADS, MLA_QK)
    uq_direct = _pad_lanes(uq).reshape(MLA_Q_RANK, MLA_HEADS * LANES)
    uq_paired = _pad_lanes(_swap_halves(uq[..., MLA_NOPE:]), MLA_NOPE).reshape(MLA_Q_RANK, MLA_HEADS * LANES)
    wuq = jnp.concatenate([uq_direct, uq_paired], axis=1).astype(BF16)

    ukv = w_ukv.reshape(MLA_KV_RANK, MLA_HEADS, MLA_NOPE + MLA_V)
    wuk = _pad_lanes(ukv[..., :MLA_NOPE]).reshape(MLA_KV_RANK, MLA_HEADS * LANES).astype(BF16)
    wuvt = ukv[..., MLA_NOPE:].reshape(MLA_KV_RANK, MLA_HEADS * MLA_V).T.astype(BF16)

    gq, gqs = _rope_lane_gains(g_q_mla, LOG2E / math.sqrt(MLA_QK))
    gk, gks = _rope_lane_gains(g_k_mla, 1.0)
    gfq = jnp.tile(g_q_fox.astype(F32) * (LOG2E / math.sqrt(FOX_DIM)), 2)[None]
    gfk = jnp.tile(g_k_fox.astype(F32), 2)[None]
    bf = _pad_lanes(b_forget.astype(F32))[None]

    place = np.zeros((N_SPLIT * LANES, FOX_HEADS // 2 * LANES), np.float32)
    for h in range(FOX_HEADS):
        base = HALF if h % 2 == 0 else 0
        for j in range(N_SPLIT):
            place[j * LANES + h, (h // 2) * LANES + base + j] = 1.0
    place = jnp.asarray(place, BF16)

    return (g_mix[None].astype(F32), win, g_cq[None].astype(F32), wuq, g_ckv[None].astype(F32),
            wuk, wuvt, wfvt, gq, gqs, gk, gks, bf, gfq, gfk, place)


def kernel(x, meta_tokens, g_ffn1, w1_gate, w1_up, w1_down, g_mix, w_in, g_cq, w_uq, g_ckv, w_ukv,
           g_q_mla, g_k_mla, b_forget, g_q_fox, g_k_fox, w_out, g_ffn2, w2_gate, w2_up, w2_down):
    b, s, d = x.shape
    depth = g_ffn1.shape[0]
    assert depth == 1 and d == D_MODEL
    l = 0

    ffn1 = (g_ffn1[l][None], w1_gate[l], w1_up[l], w1_down[l])
    ffn2 = (g_ffn2[l][None], w2_gate[l], w2_up[l], w2_down[l])
    consts = _layout_params(g_mix[l], w_in[l], g_cq[l], w_uq[l], g_ckv[l], w_ukv[l], g_q_mla[l],
                            g_k_mla[l], b_forget[l], g_q_fox[l], g_k_fox[l])

    pad = BLOCK - N_META
    h_meta = jnp.concatenate([jnp.zeros((pad, d), F32), meta_tokens.astype(F32)], axis=0)
    pos_meta = jnp.maximum(jnp.arange(BLOCK) - pad, 0)
    pos_tok = N_META + jnp.arange(s)

    h1, h_meta = _ffn_with_meta(x.reshape(b * s, d), h_meta, *ffn1)
    km, vmt = _proj(h_meta[None], consts, *_rope_tables(pos_meta), is_meta=True)
    q, k, vt = _proj(h1.reshape(b, s, d), consts, *_rope_tables(pos_tok), is_meta=False)
    attn = _attention(q, k, vt, km, vmt)
    out = _mix_ffn(h1, attn.reshape(b * s, d), w_out[l].astype(BF16), *ffn2)
    return out.reshape(b, s, d)
```

```python
import functools
import math

import jax
import jax.numpy as jnp
import numpy as np
from jax import lax
from jax.experimental import pallas as pl
from jax.experimental.pallas import tpu as pltpu

F32 = jnp.float32
BF16 = jnp.bfloat16

D_MODEL = 1024
D_FF = 2816
N_META = 16
BLOCK = 128
EPS = 1e-6
NEG_INF = -1e30
MLA_HEADS = 8
MLA_Q_RANK = 256
MLA_KV_RANK = 128
MLA_NOPE = 64
MLA_ROPE = 32
MLA_QK = MLA_NOPE + MLA_ROPE
MLA_V = 64
ROPE_THETA = 10000.0
FOX_HEADS = 8
FOX_DIM = 64
N_HEADS = MLA_HEADS + FOX_HEADS
LANES = 128
HALF = LANES // 2
N_SPLIT = 3
LOG2E = math.log2(math.e)

FF_CHUNK = 256
FFN_TM = 512
PROJ_TM = 512
ATT_TQ = 2048
ATT_TK = 1024
DIAG_COLS = 256
VMEM_LIMIT = 56 * 1024 * 1024

_O_CQ = 0
_O_CKV = _O_CQ + MLA_Q_RANK
_O_KPE = _O_CKV + MLA_KV_RANK
_O_KPE_SW = _O_KPE + LANES
_O_FQ = _O_KPE_SW + LANES
_O_FK = _O_FQ + FOX_HEADS * FOX_DIM
_O_FL = _O_FK + FOX_HEADS * FOX_DIM
_D_IN_PAD = _O_FL + LANES

_NT = (((1,), (1,)), ((), ()))


def _const_spec(shape):
    nd = len(shape)
    return pl.BlockSpec(shape, lambda *_: (0,) * nd, pipeline_mode=pl.Buffered(1))


def _rms_scale(x, width):
    return lax.rsqrt(jnp.sum(x * x, axis=-1, keepdims=True) * (1.0 / width) + EPS)


def _split3(x):
    pieces = []
    r = x
    for _ in range(N_SPLIT):
        p = r.astype(BF16)
        pieces.append(p)
        r = r - p.astype(F32)
    return pieces


def _ffn_body(h, g_ref, wg_ref, wu_ref, wd_ref):
    n = (h * _rms_scale(h, D_MODEL) * g_ref[...]).astype(wg_ref.dtype)
    acc = jnp.zeros(h.shape, F32)
    for c in range(D_FF // FF_CHUNK):
        sl = slice(c * FF_CHUNK, (c + 1) * FF_CHUNK)
        g = jnp.dot(n, wg_ref[:, sl], preferred_element_type=F32)
        u = jnp.dot(n, wu_ref[:, sl], preferred_element_type=F32)
        a = (g * (1.0 / (1.0 + jnp.exp(-g))) * u).astype(wd_ref.dtype)
        acc = acc + jnp.dot(a, wd_ref[sl, :], preferred_element_type=F32)
    return h + 0.5 * acc


def _ffn_meta_kernel(h_ref, hm_ref, g_ref, wg_ref, wu_ref, wd_ref, o_ref, om_ref):
    o_ref[...] = _ffn_body(h_ref[...], g_ref, wg_ref, wu_ref, wd_ref)

    @pl.when(pl.program_id(0) == 0)
    def _():
        om_ref[...] = _ffn_body(hm_ref[...], g_ref, wg_ref, wu_ref, wd_ref)


def _mix_ffn_kernel(h_ref, a_ref, wo_ref, g_ref, wg_ref, wu_ref, wd_ref, o_ref):
    h = h_ref[...] + jnp.dot(a_ref[...], wo_ref[...], preferred_element_type=F32)
    o_ref[...] = _ffn_body(h, g_ref, wg_ref, wu_ref, wd_ref)


def _ffn_call(kern, name, m, row_args, const_args, n_meta_rows=0):
    tm = FFN_TM
    assert m % tm == 0
    row = pl.BlockSpec((tm, D_MODEL), lambda i: (i, 0))
    out_specs, out_shape = row, jax.ShapeDtypeStruct((m, D_MODEL), F32)
    if n_meta_rows:
        out_specs = [row, pl.BlockSpec((n_meta_rows, D_MODEL), lambda i: (0, 0))]
        out_shape = [out_shape, jax.ShapeDtypeStruct((n_meta_rows, D_MODEL), F32)]
    return pl.pallas_call(
        kern,
        grid=(m // tm,),
        in_specs=[row] * len(row_args) + [_const_spec(a.shape) for a in const_args],
        out_specs=out_specs,
        out_shape=out_shape,
        compiler_params=pltpu.CompilerParams(
            dimension_semantics=("arbitrary",), vmem_limit_bytes=VMEM_LIMIT),
        name=name,
    )(*row_args, *const_args)


def _ffn_with_meta(h, h_meta, g, wg, wu, wd):
    return _ffn_call(_ffn_meta_kernel, "ffn", h.shape[0], (h,), (h_meta, g, wg, wu, wd),
                     n_meta_rows=h_meta.shape[0])


def _mix_ffn(h, attn, w_out, g, wg, wu, wd):
    return _ffn_call(_mix_ffn_kernel, "ffn_mix", h.shape[0], (h, attn), (w_out, g, wg, wu, wd))


def _half_rms_scale(x, lo):
    x2 = x * x
    s_lo = jnp.sum(jnp.where(lo, x2, 0.0), axis=-1, keepdims=True)
    s_hi = jnp.sum(jnp.where(lo, 0.0, x2), axis=-1, keepdims=True)
    return jnp.where(lo, lax.rsqrt(s_lo * (1.0 / FOX_DIM) + EPS), lax.rsqrt(s_hi * (1.0 / FOX_DIM) + EPS))


def _proj_kernel(h_ref, gmix_ref, win_ref, gcq_ref, wuq_ref, gckv_ref, wuk_ref, wuvt_ref, wfvt_ref,
                 gq_ref, gqs_ref, gk_ref, gks_ref, cc_ref, ss_ref, bf_ref, gfq_ref, gfk_ref,
                 tri_ref, place_ref, *rest, tm, is_meta):
    if is_meta:
        k_out, vt_out = rest
        q_out = carry_sc = None
    else:
        q_out, k_out, vt_out, carry_sc = rest

    lane = lax.broadcasted_iota(jnp.int32, (tm, LANES), 1)
    lo = lane < HALF
    bias_e = ((lane >= HALF) & (lane < HALF + N_SPLIT)).astype(F32)
    bias_o = (lane < N_SPLIT).astype(F32)
    ones_row = (lax.broadcasted_iota(jnp.int32, (HALF, tm), 0) == 0).astype(F32)

    def put_vt(head, blk):
        rows = [blk, ones_row] if head % 2 == 0 else [ones_row, blk]
        vt_out[0, head, 0] = jnp.concatenate(rows, axis=0).astype(BF16)

    h = h_ref[0]
    n = (h * _rms_scale(h, D_MODEL) * gmix_ref[...]).astype(BF16)
    proj = jnp.dot(n, win_ref[...], preferred_element_type=F32)

    c_q = proj[:, _O_CQ:_O_CKV]
    c_kv = proj[:, _O_CKV:_O_KPE]
    kpe = proj[:, _O_KPE:_O_KPE_SW]
    kpe_sw = proj[:, _O_KPE_SW:_O_FQ]

    cc = cc_ref[...]
    ss = ss_ref[...]

    ckvn = (c_kv * _rms_scale(c_kv, MLA_KV_RANK) * gckv_ref[...]).astype(BF16)
    kn = jnp.dot(ckvn, wuk_ref[...], preferred_element_type=F32)
    kc = kpe_sw * gks_ref[...] * ss
    for hd in range(MLA_HEADS):
        x = kn[:, hd * LANES:(hd + 1) * LANES] + kpe
        r = _rms_scale(x, MLA_QK)
        k_out[0, hd] = (r * (x * gk_ref[...] * cc + kc)).astype(BF16)
    vt = lax.dot_general(wuvt_ref[...], ckvn, _NT, preferred_element_type=F32)
    for hd in range(MLA_HEADS):
        put_vt(hd, vt[hd * MLA_V:(hd + 1) * MLA_V, :])
    if not is_meta:
        cqn = (c_q * _rms_scale(c_q, MLA_Q_RANK) * gcq_ref[...]).astype(BF16)
        qq = jnp.dot(cqn, wuq_ref[...], preferred_element_type=F32)
        for hd in range(MLA_HEADS):
            x = qq[:, hd * LANES:(hd + 1) * LANES]
            xs = qq[:, (MLA_HEADS + hd) * LANES:(MLA_HEADS + hd + 1) * LANES]
            r = _rms_scale(x, MLA_QK)
            q_out[0, hd] = (r * (x * gq_ref[...] * cc + xs * gqs_ref[...] * ss)).astype(BF16)

    fl = proj[:, _O_FL:_O_FL + LANES] + bf_ref[...]
    log_f = jnp.minimum(fl, 0.0) - jnp.log1p(jnp.exp(-jnp.abs(fl)))
    tri = tri_ref[...]
    pieces = _split3(log_f)
    assert N_SPLIT == 3
    pair = jnp.dot(tri, jnp.concatenate(pieces[:2], axis=1), preferred_element_type=F32)
    cum = pair[:, :LANES] + pair[:, LANES:] + jnp.dot(tri, pieces[2], preferred_element_type=F32)
    if is_meta:
        cum = cum - cum[tm - 1:tm, :]
    else:
        @pl.when(pl.program_id(1) == 0)
        def _():
            carry_sc[...] = jnp.zeros_like(carry_sc)
        cum = cum + carry_sc[0:1, :]
        carry_sc[...] = jnp.broadcast_to(cum[tm - 1:tm, :], carry_sc.shape)
    bias = jnp.dot(jnp.concatenate(_split3(cum * (-LOG2E)), axis=1), place_ref[...],
                   preferred_element_type=F32)
    fvt = lax.dot_general(wfvt_ref[...], n, _NT, preferred_element_type=F32)
    for p in range(FOX_HEADS // 2):
        he, ho = MLA_HEADS + 2 * p, MLA_HEADS + 2 * p + 1
        x = proj[:, _O_FK + p * LANES:_O_FK + (p + 1) * LANES]
        y = x * _half_rms_scale(x, lo) * gfk_ref[...]
        pair_bias = bias[:, p * LANES:(p + 1) * LANES]
        k_out[0, he] = jnp.where(lo, y, pair_bias).astype(BF16)
        k_out[0, ho] = jnp.where(lo, pair_bias, y).astype(BF16)
        put_vt(he, fvt[(2 * p) * FOX_DIM:(2 * p + 1) * FOX_DIM, :])
        put_vt(ho, fvt[(2 * p + 1) * FOX_DIM:(2 * p + 2) * FOX_DIM, :])
        if not is_meta:
            x = proj[:, _O_FQ + p * LANES:_O_FQ + (p + 1) * LANES]
            y = x * _half_rms_scale(x, lo) * gfq_ref[...]
            q_out[0, he] = jnp.where(lo, y, bias_e).astype(BF16)
            q_out[0, ho] = jnp.where(lo, bias_o, y).astype(BF16)


def _proj(h, consts, cc, ss, *, is_meta):
    b, s, _ = h.shape
    tm = min(PROJ_TM, s)
    (gmix, win, gcq, wuq, gckv, wuk, wuvt, wfvt, gq, gqs, gk, gks, bf, gfq, gfk, place) = consts
    tri = jnp.tril(jnp.ones((tm, tm), BF16))
    row_in = pl.BlockSpec((1, tm, D_MODEL), lambda bi, i: (bi, i, 0))
    tab = pl.BlockSpec((tm, LANES), lambda bi, i: (i, 0))
    head_out = pl.BlockSpec((1, N_HEADS, tm, LANES), lambda bi, i: (bi, 0, i, 0))
    head_shape = jax.ShapeDtypeStruct((b, N_HEADS, s, LANES), BF16)
    vt_out = pl.BlockSpec((1, N_HEADS, 1, LANES, tm), lambda bi, i: (bi, 0, i, 0, 0))
    vt_shape = jax.ShapeDtypeStruct((b, N_HEADS, s // tm, LANES, tm), BF16)
    vec = _const_spec((1, LANES))
    in_specs = [row_in, _const_spec((1, D_MODEL)), _const_spec(win.shape),
                _const_spec((1, MLA_Q_RANK)), _const_spec(wuq.shape),
                _const_spec((1, MLA_KV_RANK)), _const_spec(wuk.shape),
                _const_spec(wuvt.shape), _const_spec(wfvt.shape),
                vec, vec, vec, vec, tab, tab, vec, vec, vec,
                _const_spec((tm, tm)), _const_spec(place.shape)]
    if is_meta:
        out_specs, out_shape, scratch = [head_out, vt_out], [head_shape, vt_shape], []
    else:
        out_specs, out_shape = [head_out, head_out, vt_out], [head_shape, head_shape, vt_shape]
        scratch = [pltpu.VMEM((8, LANES), F32)]
    return pl.pallas_call(
        functools.partial(_proj_kernel, tm=tm, is_meta=is_meta),
        grid=(b, s // tm),
        in_specs=in_specs,
        out_specs=out_specs,
        out_shape=out_shape,
        scratch_shapes=scratch,
        compiler_params=pltpu.CompilerParams(
            dimension_semantics=("arbitrary", "arbitrary"), vmem_limit_bytes=VMEM_LIMIT),
        name="proj_meta" if is_meta else "proj",
    )(h, gmix, win, gcq, wuq, gckv, wuk, wuvt, wfvt, gq, gqs, gk, gks, cc, ss, bf, gfq, gfk, tri, place)


def _attn_kernel(q_ref, k_ref, vt_ref, km_ref, vmt_ref, o_ref, m_sc, acc_sc, *, tq, tk):
    i = pl.program_id(2)
    vchunk = vt_ref.shape[-1]

    def values(hh, block, n_keys):
        n = n_keys // vchunk
        return jnp.concatenate([vt_ref[0, hh, block * n + c] for c in range(n)], axis=1)

    def scores(hh, kc, cols=slice(None)):
        return lax.dot_general(kc, q_ref[0, hh, cols, :], _NT, preferred_element_type=F32)

    def update(hh, s, vt, cols=slice(None)):
        m_old = m_sc[hh, :, cols]
        m_new = jnp.maximum(m_old, jnp.max(s, axis=0, keepdims=True))
        p = jnp.exp2(s - m_new).astype(BF16)
        pv = jnp.dot(vt, p, preferred_element_type=F32)
        acc_sc[hh, :, cols] = jnp.exp2(m_old - m_new) * acc_sc[hh, :, cols] + pv
        m_sc[hh, :, cols] = m_new

    m_sc[...] = jnp.full(m_sc.shape, NEG_INF, F32)
    acc_sc[...] = jnp.zeros(acc_sc.shape, F32)

    def body(j, carry):
        off = pl.multiple_of(j * tk, tk)
        s = [scores(hh, k_ref[0, hh, pl.ds(off, tk), :]) for hh in range(2)]
        for hh in range(2):
            update(hh, s[hh], values(hh, j, tk))
        return carry

    lax.fori_loop(0, i * (tq // tk), body, 0)

    off = pl.multiple_of(i * tq, tq)
    vt = [values(hh, i, tq) for hh in range(2)]

    def diag_scores(c):
        nk = (c + 1) * DIAG_COLS
        row = lax.broadcasted_iota(jnp.int32, (nk + BLOCK, DIAG_COLS), 0)
        col = lax.broadcasted_iota(jnp.int32, (nk + BLOCK, DIAG_COLS), 1)
        valid = (row <= col + c * DIAG_COLS) | (row >= nk + BLOCK - N_META)
        out = []
        for hh in range(2):
            keys = jnp.concatenate([k_ref[0, hh, pl.ds(off, nk), :], km_ref[0, hh]], axis=0)
            out.append(jnp.where(valid, scores(hh, keys, slice(c * DIAG_COLS, nk)), NEG_INF))
        return out

    n_blocks = tq // DIAG_COLS
    ahead = 2
    pend = {}
    for n in range(n_blocks + ahead):
        if n < n_blocks:
            pend[n] = diag_scores(n)
        if n >= ahead:
            c = n - ahead
            nk = (c + 1) * DIAG_COLS
            s = pend.pop(c)
            for hh in range(2):
                vals = jnp.concatenate([vt[hh][:, :nk], vmt_ref[0, hh, 0]], axis=1)
                update(hh, s[hh], vals, slice(c * DIAG_COLS, nk))

    a_e = acc_sc[0]
    a_o = acc_sc[1]
    top = lax.broadcasted_iota(jnp.int32, (LANES, tq), 0) < HALF
    out_t = jnp.where(top, a_e / a_e[HALF:HALF + 1, :], a_o / a_o[0:1, :])
    o_ref[0] = out_t.T.astype(BF16)


def _attention(q, k, vt, km, vmt):
    b, _, s, _ = q.shape
    tq, tk = ATT_TQ, ATT_TK
    vchunk = vt.shape[-1]
    assert tq % tk == 0 and s % tq == 0 and tk % vchunk == 0
    pairs = N_HEADS // 2
    return pl.pallas_call(
        functools.partial(_attn_kernel, tq=tq, tk=tk),
        grid=(b, pairs, s // tq),
        in_specs=[
            pl.BlockSpec((1, 2, tq, LANES), lambda bi, p, i: (bi, p, i, 0)),
            pl.BlockSpec((1, 2, s, LANES), lambda bi, p, i: (bi, p, 0, 0)),
            pl.BlockSpec((1, 2, s // vchunk, LANES, vchunk), lambda bi, p, i: (bi, p, 0, 0, 0)),
            pl.BlockSpec((1, 2, BLOCK, LANES), lambda bi, p, i: (0, p, 0, 0)),
            pl.BlockSpec((1, 2, 1, LANES, BLOCK), lambda bi, p, i: (0, p, 0, 0, 0)),
        ],
        out_specs=pl.BlockSpec((1, tq, LANES), lambda bi, p, i: (bi, i, p)),
        out_shape=jax.ShapeDtypeStruct((b, s, pairs * LANES), BF16),
        scratch_shapes=[pltpu.VMEM((2, 1, tq), F32), pltpu.VMEM((2, LANES, tq), F32)],
        compiler_params=pltpu.CompilerParams(
            dimension_semantics=("arbitrary", "arbitrary", "arbitrary"),
            vmem_limit_bytes=VMEM_LIMIT),
        name="attention",
    )(q, k, vt, km, vmt)


def _pad_lanes(x, offset=0):
    n = x.shape[-1]
    pad = [(0, 0)] * (x.ndim - 1) + [(offset, LANES - offset - n)]
    return jnp.pad(x, pad)


def _swap_halves(x):
    half = MLA_ROPE // 2
    return jnp.concatenate([x[..., half:], x[..., :half]], axis=-1)


def _rope_lane_gains(g, scale):
    g = g.astype(F32) * scale
    direct = _pad_lanes(g)
    paired = _pad_lanes(_swap_halves(g[MLA_NOPE:]), MLA_NOPE)
    return direct[None], paired[None]


def _rope_tables(pos):
    half = MLA_ROPE // 2
    inv_freq = 1.0 / (ROPE_THETA ** (jnp.arange(half, dtype=F32) / half))
    ang = inv_freq[:, None] * pos.astype(F32)[None, :]
    cos, sin = jnp.cos(ang), jnp.sin(ang)
    ones = jnp.ones((MLA_NOPE, pos.shape[0]), F32)
    zeros = jnp.zeros((LANES - MLA_QK, pos.shape[0]), F32)
    cc = jnp.concatenate([ones, cos, cos, zeros], axis=0).T
    ss = jnp.concatenate([0.0 * ones, -sin, sin, zeros], axis=0).T
    return cc, ss


def _layout_params(g_mix, w_in, g_cq, w_uq, g_ckv, w_ukv, g_q_mla, g_k_mla, b_forget,
                   g_q_fox, g_k_fox):
    o_kpe = MLA_Q_RANK + MLA_KV_RANK
    o_fox = o_kpe + MLA_ROPE
    o_fv = o_fox + 2 * FOX_HEADS * FOX_DIM
    o_fl = o_fv + FOX_HEADS * FOX_DIM
    w_kpe = w_in[:, o_kpe:o_fox]
    win = jnp.concatenate([
        w_in[:, :o_kpe],
        _pad_lanes(w_kpe, MLA_NOPE),
        _pad_lanes(_swap_halves(w_kpe), MLA_NOPE),
        w_in[:, o_fox:o_fv],
        _pad_lanes(w_in[:, o_fl:]),
    ], axis=1).astype(BF16)
    assert win.shape[1] == _D_IN_PAD
    wfvt = w_in[:, o_fv:o_fl].T.astype(BF16)

    uq = w_uq.reshape(MLA_Q_RANK, MLA_HEADS, MLA_QK)
    uq_direct = _pad_lanes(uq).reshape(MLA_Q_RANK, MLA_HEADS * LANES)
    uq_paired = _pad_lanes(_swap_halves(uq[..., MLA_NOPE:]), MLA_NOPE).reshape(MLA_Q_RANK, MLA_HEADS * LANES)
    wuq = jnp.concatenate([uq_direct, uq_paired], axis=1).astype(BF16)

    ukv = w_ukv.reshape(MLA_KV_RANK, MLA_HEADS, MLA_NOPE + MLA_V)
    wuk = _pad_lanes(ukv[..., :MLA_NOPE]).reshape(MLA_KV_RANK, MLA_HEADS * LANES).astype(BF16)
    wuvt = ukv[..., MLA_NOPE:].reshape(MLA_KV_RANK, MLA_HEADS * MLA_V).T.astype(BF16)

    gq, gqs = _rope_lane_gains(g_q_mla, LOG2E / math.sqrt(MLA_QK))
    gk, gks = _rope_lane_gains(g_k_mla, 1.0)
    gfq = jnp.tile(g_q_fox.astype(F32) * (LOG2E / math.sqrt(FOX_DIM)), 2)[None]
    gfk = jnp.tile(g_k_fox.astype(F32), 2)[None]
    bf = _pad_lanes(b_forget.astype(F32))[None]

    place = np.zeros((N_SPLIT * LANES, FOX_HEADS // 2 * LANES), np.float32)
    for h in range(FOX_HEADS):
        base = HALF if h % 2 == 0 else 0
        for j in range(N_SPLIT):
            place[j * LANES + h, (h // 2) * LANES + base + j] = 1.0
    place = jnp.asarray(place, BF16)

    return (g_mix[None].astype(F32), win, g_cq[None].astype(F32), wuq, g_ckv[None].astype(F32),
            wuk, wuvt, wfvt, gq, gqs, gk, gks, bf, gfq, gfk, place)


def kernel(x, meta_tokens, g_ffn1, w1_gate, w1_up, w1_down, g_mix, w_in, g_cq, w_uq, g_ckv, w_ukv,
           g_q_mla, g_k_mla, b_forget, g_q_fox, g_k_fox, w_out, g_ffn2, w2_gate, w2_up, w2_down):
    b, s, d = x.shape
    depth = g_ffn1.shape[0]
    assert depth == 1 and d == D_MODEL
    l = 0

    ffn1 = (g_ffn1[l][None], w1_gate[l], w1_up[l], w1_down[l])
    ffn2 = (g_ffn2[l][None], w2_gate[l], w2_up[l], w2_down[l])
    consts = _layout_params(g_mix[l], w_in[l], g_cq[l], w_uq[l], g_ckv[l], w_ukv[l], g_q_mla[l],
                            g_k_mla[l], b_forget[l], g_q_fox[l], g_k_fox[l])

    pad = BLOCK - N_META
    h_meta = jnp.concatenate([jnp.zeros((pad, d), F32), meta_tokens.astype(F32)], axis=0)
    pos_meta = jnp.maximum(jnp.arange(BLOCK) - pad, 0)
    pos_tok = N_META + jnp.arange(s)

    h1, h_meta = _ffn_with_meta(x.reshape(b * s, d), h_meta, *ffn1)
    km, vmt = _proj(h_meta[None], consts, *_rope_tables(pos_meta), is_meta=True)
    q, k, vt = _proj(h1.reshape(b, s, d), consts, *_rope_tables(pos_tok), is_meta=False)
    attn = _attention(q, k, vt, km, vmt)
    out = _mix_ffn(h1, attn.reshape(b * s, d), w_out[l].astype(BF16), *ffn2)
    return out.reshape(b, s, d)
```

```python
import functools
import math

import jax
import jax.numpy as jnp
import numpy as np
from jax import lax
from jax.experimental import pallas as pl
from jax.experimental.pallas import tpu as pltpu

F32 = jnp.float32
BF16 = jnp.bfloat16

D_MODEL = 1024
D_FF = 2816
N_META = 16
BLOCK = 128
EPS = 1e-6
NEG_INF = -1e30
MLA_HEADS = 8
MLA_Q_RANK = 256
MLA_KV_RANK = 128
MLA_NOPE = 64
MLA_ROPE = 32
MLA_QK = MLA_NOPE + MLA_ROPE
MLA_V = 64
ROPE_THETA = 10000.0
FOX_HEADS = 8
FOX_DIM = 64
N_HEADS = MLA_HEADS + FOX_HEADS
LANES = 128
HALF = LANES // 2
N_SPLIT = 3
LOG2E = math.log2(math.e)

FF_CHUNK = 256
FFN_TM = 512
PROJ_TM = 512
ATT_TQ = 2048
ATT_TK = 1024
DIAG_COLS = 256
VMEM_LIMIT = 56 * 1024 * 1024

_O_CQ = 0
_O_CKV = _O_CQ + MLA_Q_RANK
_O_KPE = _O_CKV + MLA_KV_RANK
_O_KPE_SW = _O_KPE + LANES
_O_FQ = _O_KPE_SW + LANES
_O_FK = _O_FQ + FOX_HEADS * FOX_DIM
_O_FL = _O_FK + FOX_HEADS * FOX_DIM
_D_IN_PAD = _O_FL + LANES

_NT = (((1,), (1,)), ((), ()))


def _const_spec(shape):
    nd = len(shape)
    return pl.BlockSpec(shape, lambda *_: (0,) * nd, pipeline_mode=pl.Buffered(1))


def _rms_scale(x, width):
    return lax.rsqrt(jnp.sum(x * x, axis=-1, keepdims=True) * (1.0 / width) + EPS)


def _split3(x):
    pieces = []
    r = x
    for _ in range(N_SPLIT):
        p = r.astype(BF16)
        pieces.append(p)
        r = r - p.astype(F32)
    return pieces


def _ffn_body(h, g_ref, wg_ref, wu_ref, wd_ref):
    n = (h * _rms_scale(h, D_MODEL) * g_ref[...]).astype(wg_ref.dtype)
    acc = jnp.zeros(h.shape, F32)
    for c in range(D_FF // FF_CHUNK):
        sl = slice(c * FF_CHUNK, (c + 1) * FF_CHUNK)
        g = jnp.dot(n, wg_ref[:, sl], preferred_element_type=F32)
        u = jnp.dot(n, wu_ref[:, sl], preferred_element_type=F32)
        a = (g * (1.0 / (1.0 + jnp.exp(-g))) * u).astype(wd_ref.dtype)
        acc = acc + jnp.dot(a, wd_ref[sl, :], preferred_element_type=F32)
    return h + 0.5 * acc


def _ffn_meta_kernel(h_ref, hm_ref, g_ref, wg_ref, wu_ref, wd_ref, o_ref, om_ref):
    o_ref[...] = _ffn_body(h_ref[...], g_ref, wg_ref, wu_ref, wd_ref)

    @pl.when(pl.program_id(0) == 0)
    def _():
        om_ref[...] = _ffn_body(hm_ref[...], g_ref, wg_ref, wu_ref, wd_ref)


def _mix_ffn_kernel(h_ref, a_ref, wo_ref, g_ref, wg_ref, wu_ref, wd_ref, o_ref):
    h = h_ref[...] + jnp.dot(a_ref[...].astype(wo_ref.dtype), wo_ref[...], preferred_element_type=F32)
    o_ref[...] = _ffn_body(h, g_ref, wg_ref, wu_ref, wd_ref)


def _ffn_call(kern, name, m, row_args, const_args, n_meta_rows=0):
    tm = FFN_TM
    assert m % tm == 0
    row = pl.BlockSpec((tm, D_MODEL), lambda i: (i, 0))
    out_specs, out_shape = row, jax.ShapeDtypeStruct((m, D_MODEL), F32)
    if n_meta_rows:
        out_specs = [row, pl.BlockSpec((n_meta_rows, D_MODEL), lambda i: (0, 0))]
        out_shape = [out_shape, jax.ShapeDtypeStruct((n_meta_rows, D_MODEL), F32)]
    return pl.pallas_call(
        kern,
        grid=(m // tm,),
        in_specs=[row] * len(row_args) + [_const_spec(a.shape) for a in const_args],
        out_specs=out_specs,
        out_shape=out_shape,
        compiler_params=pltpu.CompilerParams(
            dimension_semantics=("arbitrary",), vmem_limit_bytes=VMEM_LIMIT),
        name=name,
    )(*row_args, *const_args)


def _ffn_with_meta(h, h_meta, g, wg, wu, wd):
    return _ffn_call(_ffn_meta_kernel, "ffn", h.shape[0], (h,), (h_meta, g, wg, wu, wd),
                     n_meta_rows=h_meta.shape[0])


def _mix_ffn(h, attn, w_out, g, wg, wu, wd):
    return _ffn_call(_mix_ffn_kernel, "ffn_mix", h.shape[0], (h, attn), (w_out, g, wg, wu, wd))


def _half_rms_scale(x, lo):
    x2 = x * x
    s_lo = jnp.sum(jnp.where(lo, x2, 0.0), axis=-1, keepdims=True)
    s_hi = jnp.sum(jnp.where(lo, 0.0, x2), axis=-1, keepdims=True)
    return jnp.where(lo, lax.rsqrt(s_lo * (1.0 / FOX_DIM) + EPS), lax.rsqrt(s_hi * (1.0 / FOX_DIM) + EPS))


def _proj_kernel(h_ref, gmix_ref, win_ref, gcq_ref, wuq_ref, gckv_ref, wuk_ref, wuvt_ref, wfvt_ref,
                 gq_ref, gqs_ref, gk_ref, gks_ref, cc_ref, ss_ref, bf_ref, gfq_ref, gfk_ref,
                 tri_ref, place_ref, *rest, tm, is_meta):
    if is_meta:
        k_out, vt_out = rest
        q_out = carry_sc = None
    else:
        q_out, k_out, vt_out, carry_sc = rest

    lane = lax.broadcasted_iota(jnp.int32, (tm, LANES), 1)
    lo = lane < HALF
    bias_e = ((lane >= HALF) & (lane < HALF + N_SPLIT)).astype(F32)
    bias_o = (lane < N_SPLIT).astype(F32)
    ones_row = (lax.broadcasted_iota(jnp.int32, (HALF, tm), 0) == 0).astype(F32)

    def put_vt(head, blk):
        rows = [blk, ones_row] if head % 2 == 0 else [ones_row, blk]
        vt_out[0, head, 0] = jnp.concatenate(rows, axis=0).astype(BF16)

    h = h_ref[0]
    n = (h * _rms_scale(h, D_MODEL) * gmix_ref[...]).astype(BF16)
    proj = jnp.dot(n, win_ref[...], preferred_element_type=F32)

    c_q = proj[:, _O_CQ:_O_CKV]
    c_kv = proj[:, _O_CKV:_O_KPE]
    kpe = proj[:, _O_KPE:_O_KPE_SW]
    kpe_sw = proj[:, _O_KPE_SW:_O_FQ]

    cc = cc_ref[...]
    ss = ss_ref[...]

    ckvn = (c_kv * _rms_scale(c_kv, MLA_KV_RANK) * gckv_ref[...]).astype(BF16)
    kn = jnp.dot(ckvn, wuk_ref[...], preferred_element_type=F32)
    kc = kpe_sw * gks_ref[...] * ss
    for hd in range(MLA_HEADS):
        x = kn[:, hd * LANES:(hd + 1) * LANES] + kpe
        r = _rms_scale(x, MLA_QK)
        k_out[0, hd] = (r * (x * gk_ref[...] * cc + kc)).astype(BF16)
    vt = lax.dot_general(wuvt_ref[...], ckvn, _NT, preferred_element_type=F32)
    for hd in range(MLA_HEADS):
        put_vt(hd, vt[hd * MLA_V:(hd + 1) * MLA_V, :])
    if not is_meta:
        cqn = (c_q * _rms_scale(c_q, MLA_Q_RANK) * gcq_ref[...]).astype(BF16)
        qq = jnp.dot(cqn, wuq_ref[...], preferred_element_type=F32)
        for hd in range(MLA_HEADS):
            x = qq[:, hd * LANES:(hd + 1) * LANES]
            xs = qq[:, (MLA_HEADS + hd) * LANES:(MLA_HEADS + hd + 1) * LANES]
            r = _rms_scale(x, MLA_QK)
            q_out[0, hd] = (r * (x * gq_ref[...] * cc + xs * gqs_ref[...] * ss)).astype(BF16)

    fl = proj[:, _O_FL:_O_FL + LANES] + bf_ref[...]
    log_f = jnp.minimum(fl, 0.0) - jnp.log1p(jnp.exp(-jnp.abs(fl)))
    tri = tri_ref[...]
    pieces = _split3(log_f)
    assert N_SPLIT == 3
    pair = jnp.dot(tri, jnp.concatenate(pieces[:2], axis=1), preferred_element_type=F32)
    cum = pair[:, :LANES] + pair[:, LANES:] + jnp.dot(tri, pieces[2], preferred_element_type=F32)
    if is_meta:
        cum = cum - cum[tm - 1:tm, :]
    else:
        @pl.when(pl.program_id(1) == 0)
        def _():
            carry_sc[...] = jnp.zeros_like(carry_sc)
        cum = cum + carry_sc[0:1, :]
        carry_sc[...] = jnp.broadcast_to(cum[tm - 1:tm, :], carry_sc.shape)
    bias = jnp.dot(jnp.concatenate(_split3(cum * (-LOG2E)), axis=1), place_ref[...],
                   preferred_element_type=F32)
    fvt = lax.dot_general(wfvt_ref[...], n, _NT, preferred_element_type=F32)
    for p in range(FOX_HEADS // 2):
        he, ho = MLA_HEADS + 2 * p, MLA_HEADS + 2 * p + 1
        x = proj[:, _O_FK + p * LANES:_O_FK + (p + 1) * LANES]
        y = x * _half_rms_scale(x, lo) * gfk_ref[...]
        pair_bias = bias[:, p * LANES:(p + 1) * LANES]
        k_out[0, he] = jnp.where(lo, y, pair_bias).astype(BF16)
        k_out[0, ho] = jnp.where(lo, pair_bias, y).astype(BF16)
        put_vt(he, fvt[(2 * p) * FOX_DIM:(2 * p + 1) * FOX_DIM, :])
        put_vt(ho, fvt[(2 * p + 1) * FOX_DIM:(2 * p + 2) * FOX_DIM, :])
        if not is_meta:
            x = proj[:, _O_FQ + p * LANES:_O_FQ + (p + 1) * LANES]
            y = x * _half_rms_scale(x, lo) * gfq_ref[...]
            q_out[0, he] = jnp.where(lo, y, bias_e).astype(BF16)
            q_out[0, ho] = jnp.where(lo, bias_o, y).astype(BF16)


def _proj(h, consts, cc, ss, *, is_meta):
    b, s, _ = h.shape
    tm = min(PROJ_TM, s)
    (gmix, win, gcq, wuq, gckv, wuk, wuvt, wfvt, gq, gqs, gk, gks, bf, gfq, gfk, place) = consts
    tri = jnp.tril(jnp.ones((tm, tm), BF16))
    row_in = pl.BlockSpec((1, tm, D_MODEL), lambda bi, i: (bi, i, 0))
    tab = pl.BlockSpec((tm, LANES), lambda bi, i: (i, 0))
    head_out = pl.BlockSpec((1, N_HEADS, tm, LANES), lambda bi, i: (bi, 0, i, 0))
    head_shape = jax.ShapeDtypeStruct((b, N_HEADS, s, LANES), BF16)
    vt_out = pl.BlockSpec((1, N_HEADS, 1, LANES, tm), lambda bi, i: (bi, 0, i, 0, 0))
    vt_shape = jax.ShapeDtypeStruct((b, N_HEADS, s // tm, LANES, tm), BF16)
    vec = _const_spec((1, LANES))
    in_specs = [row_in, _const_spec((1, D_MODEL)), _const_spec(win.shape),
                _const_spec((1, MLA_Q_RANK)), _const_spec(wuq.shape),
                _const_spec((1, MLA_KV_RANK)), _const_spec(wuk.shape),
                _const_spec(wuvt.shape), _const_spec(wfvt.shape),
                vec, vec, vec, vec, tab, tab, vec, vec, vec,
                _const_spec((tm, tm)), _const_spec(place.shape)]
    if is_meta:
        out_specs, out_shape, scratch = [head_out, vt_out], [head_shape, vt_shape], []
    else:
        out_specs, out_shape = [head_out, head_out, vt_out], [head_shape, head_shape, vt_shape]
        scratch = [pltpu.VMEM((8, LANES), F32)]
    return pl.pallas_call(
        functools.partial(_proj_kernel, tm=tm, is_meta=is_meta),
        grid=(b, s // tm),
        in_specs=in_specs,
        out_specs=out_specs,
        out_shape=out_shape,
        scratch_shapes=scratch,
        compiler_params=pltpu.CompilerParams(
            dimension_semantics=("arbitrary", "arbitrary"), vmem_limit_bytes=VMEM_LIMIT),
        name="proj_meta" if is_meta else "proj",
    )(h, gmix, win, gcq, wuq, gckv, wuk, wuvt, wfvt, gq, gqs, gk, gks, cc, ss, bf, gfq, gfk, tri, place)


def _attn_kernel(q_ref, k_ref, vt_ref, km_ref, vmt_ref, o_ref, m_sc, acc_sc, *, tq, tk):
    i = pl.program_id(2)
    vchunk = vt_ref.shape[-1]

    def values(hh, block, n_keys):
        n = n_keys // vchunk
        return jnp.concatenate([vt_ref[0, hh, block * n + c] for c in range(n)], axis=1)

    def scores(hh, kc, cols=slice(None)):
        return lax.dot_general(kc, q_ref[0, hh, cols, :], _NT, preferred_element_type=F32)

    def update(hh, s, vt, cols=slice(None)):
        m_old = m_sc[hh, :, cols]
        m_new = jnp.maximum(m_old, jnp.max(s, axis=0, keepdims=True))
        p = jnp.exp2(s - m_new).astype(BF16)
        pv = jnp.dot(vt, p, preferred_element_type=F32)
        acc_sc[hh, :, cols] = jnp.exp2(m_old - m_new) * acc_sc[hh, :, cols] + pv
        m_sc[hh, :, cols] = m_new

    m_sc[...] = jnp.full(m_sc.shape, NEG_INF, F32)
    acc_sc[...] = jnp.zeros(acc_sc.shape, F32)

    def body(j, carry):
        off = pl.multiple_of(j * tk, tk)
        s = [scores(hh, k_ref[0, hh, pl.ds(off, tk), :]) for hh in range(2)]
        for hh in range(2):
            update(hh, s[hh], values(hh, j, tk))
        return carry

    lax.fori_loop(0, i * (tq // tk), body, 0)

    off = pl.multiple_of(i * tq, tq)
    vt = [values(hh, i, tq) for hh in range(2)]

    def diag_scores(c):
        nk = (c + 1) * DIAG_COLS
        row = lax.broadcasted_iota(jnp.int32, (nk + BLOCK, DIAG_COLS), 0)
        col = lax.broadcasted_iota(jnp.int32, (nk + BLOCK, DIAG_COLS), 1)
        valid = (row <= col + c * DIAG_COLS) | (row >= nk + BLOCK - N_META)
        out = []
        for hh in range(2):
            keys = jnp.concatenate([k_ref[0, hh, pl.ds(off, nk), :], km_ref[0, hh]], axis=0)
            out.append(jnp.where(valid, scores(hh, keys, slice(c * DIAG_COLS, nk)), NEG_INF))
        return out

    n_blocks = tq // DIAG_COLS
    ahead = 2
    pend = {}
    for n in range(n_blocks + ahead):
        if n < n_blocks:
            pend[n] = diag_scores(n)
        if n >= ahead:
            c = n - ahead
            nk = (c + 1) * DIAG_COLS
            s = pend.pop(c)
            for hh in range(2):
                vals = jnp.concatenate([vt[hh][:, :nk], vmt_ref[0, hh, 0]], axis=1)
                update(hh, s[hh], vals, slice(c * DIAG_COLS, nk))

    a_e = acc_sc[0]
    a_o = acc_sc[1]
    top = lax.broadcasted_iota(jnp.int32, (LANES, tq), 0) < HALF
    out_t = jnp.where(top, a_e / a_e[HALF:HALF + 1, :], a_o / a_o[0:1, :])
    o_ref[0] = out_t.T.astype(BF16)


def _attention(q, k, vt, km, vmt):
    b, _, s, _ = q.shape
    tq, tk = ATT_TQ, ATT_TK
    vchunk = vt.shape[-1]
    assert tq % tk == 0 and s % tq == 0 and tk % vchunk == 0
    pairs = N_HEADS // 2
    return pl.pallas_call(
        functools.partial(_attn_kernel, tq=tq, tk=tk),
        grid=(b, pairs, s // tq),
        in_specs=[
            pl.BlockSpec((1, 2, tq, LANES), lambda bi, p, i: (bi, p, i, 0)),
            pl.BlockSpec((1, 2, s, LANES), lambda bi, p, i: (bi, p, 0, 0)),
            pl.BlockSpec((1, 2, s // vchunk, LANES, vchunk), lambda bi, p, i: (bi, p, 0, 0, 0)),
            pl.BlockSpec((1, 2, BLOCK, LANES), lambda bi, p, i: (0, p, 0, 0)),
            pl.BlockSpec((1, 2, 1, LANES, BLOCK), lambda bi, p, i: (0, p, 0, 0, 0)),
        ],
        out_specs=pl.BlockSpec((1, tq, LANES), lambda bi, p, i: (bi, i, p)),
        out_shape=jax.ShapeDtypeStruct((b, s, pairs * LANES), BF16),
        scratch_shapes=[pltpu.VMEM((2, 1, tq), F32), pltpu.VMEM((2, LANES, tq), F32)],
        compiler_params=pltpu.CompilerParams(
            dimension_semantics=("arbitrary", "arbitrary", "arbitrary"),
            vmem_limit_bytes=VMEM_LIMIT),
        name="attention",
    )(q, k, vt, km, vmt)


def _pad_lanes(x, offset=0):
    n = x.shape[-1]
    pad = [(0, 0)] * (x.ndim - 1) + [(offset, LANES - offset - n)]
    return jnp.pad(x, pad)


def _swap_halves(x):
    half = MLA_ROPE // 2
    return jnp.concatenate([x[..., half:], x[..., :half]], axis=-1)


def _rope_lane_gains(g, scale):
    g = g.astype(F32) * scale
    direct = _pad_lanes(g)
    paired = _pad_lanes(_swap_halves(g[MLA_NOPE:]), MLA_NOPE)
    return direct[None], paired[None]


def _rope_tables(pos_hi, pos_lo):
    half = MLA_ROPE // 2
    inv_freq = 1.0 / (ROPE_THETA ** (jnp.arange(half, dtype=F32) / half))
    off = jnp.zeros((MLA_NOPE,), F32)
    freq = jnp.concatenate([off, inv_freq, inv_freq, jnp.zeros((LANES - MLA_QK,), F32)])
    sign = jnp.concatenate([off, -jnp.ones((half,), F32), jnp.ones((half,), F32),
                            jnp.zeros((LANES - MLA_QK,), F32)])
    ang_hi = pos_hi.astype(F32)[:, None] * freq[None, :]
    ang_lo = pos_lo.astype(F32)[:, None] * freq[None, :]
    c_hi, s_hi = jnp.cos(ang_hi)[:, None, :], jnp.sin(ang_hi)[:, None, :]
    c_lo, s_lo = jnp.cos(ang_lo)[None, :, :], jnp.sin(ang_lo)[None, :, :]
    cc = (c_hi * c_lo - s_hi * s_lo).reshape(-1, LANES)
    ss = ((s_hi * c_lo + c_hi * s_lo) * sign).reshape(-1, LANES)
    return cc, ss


def _layout_params(g_mix, w_in, g_cq, w_uq, g_ckv, w_ukv, g_q_mla, g_k_mla, b_forget,
                   g_q_fox, g_k_fox):
    o_kpe = MLA_Q_RANK + MLA_KV_RANK
    o_fox = o_kpe + MLA_ROPE
    o_fv = o_fox + 2 * FOX_HEADS * FOX_DIM
    o_fl = o_fv + FOX_HEADS * FOX_DIM
    w_kpe = w_in[:, o_kpe:o_fox]
    win = jnp.concatenate([
        w_in[:, :o_kpe],
        _pad_lanes(w_kpe, MLA_NOPE),
        _pad_lanes(_swap_halves(w_kpe), MLA_NOPE),
        w_in[:, o_fox:o_fv],
        _pad_lanes(w_in[:, o_fl:]),
    ], axis=1).astype(BF16)
    assert win.shape[1] == _D_IN_PAD
    wfvt = w_in[:, o_fv:o_fl].T.astype(BF16)

    uq = w_uq.reshape(MLA_Q_RANK, MLA_HEADS, MLA_QK)
    uq_direct = _pad_lanes(uq).reshape(MLA_Q_RANK, MLA_HEADS * LANES)
    uq_paired = _pad_lanes(_swap_halves(uq[..., MLA_NOPE:]), MLA_NOPE).reshape(MLA_Q_RANK, MLA_HEADS * LANES)
    wuq = jnp.concatenate([uq_direct, uq_paired], axis=1).astype(BF16)

    ukv = w_ukv.reshape(MLA_KV_RANK, MLA_HEADS, MLA_NOPE + MLA_V)
    wuk = _pad_lanes(ukv[..., :MLA_NOPE]).reshape(MLA_KV_RANK, MLA_HEADS * LANES).astype(BF16)
    wuvt = ukv[..., MLA_NOPE:].reshape(MLA_KV_RANK, MLA_HEADS * MLA_V).T.astype(BF16)

    gq, gqs = _rope_lane_gains(g_q_mla, LOG2E / math.sqrt(MLA_QK))
    gk, gks = _rope_lane_gains(g_k_mla, 1.0)
    gfq = jnp.tile(g_q_fox.astype(F32) * (LOG2E / math.sqrt(FOX_DIM)), 2)[None]
    gfk = jnp.tile(g_k_fox.astype(F32), 2)[None]
    bf = _pad_lanes(b_forget.astype(F32))[None]

    place = np.zeros((N_SPLIT * LANES, FOX_HEADS // 2 * LANES), np.float32)
    for h in range(FOX_HEADS):
        base = HALF if h % 2 == 0 else 0
        for j in range(N_SPLIT):
            place[j * LANES + h, (h // 2) * LANES + base + j] = 1.0
    place = jnp.asarray(place, BF16)

    return (g_mix[None].astype(F32), win, g_cq[None].astype(F32), wuq, g_ckv[None].astype(F32),
            wuk, wuvt, wfvt, gq, gqs, gk, gks, bf, gfq, gfk, place)


def kernel(x, meta_tokens, g_ffn1, w1_gate, w1_up, w1_down, g_mix, w_in, g_cq, w_uq, g_ckv, w_ukv,
           g_q_mla, g_k_mla, b_forget, g_q_fox, g_k_fox, w_out, g_ffn2, w2_gate, w2_up, w2_down):
    b, s, d = x.shape
    depth = g_ffn1.shape[0]
    assert depth == 1 and d == D_MODEL
    l = 0

    ffn1 = (g_ffn1[l][None], w1_gate[l], w1_up[l], w1_down[l])
    ffn2 = (g_ffn2[l][None], w2_gate[l], w2_up[l], w2_down[l])
    consts = _layout_params(g_mix[l], w_in[l], g_cq[l], w_uq[l], g_ckv[l], w_ukv[l], g_q_mla[l],
                            g_k_mla[l], b_forget[l], g_q_fox[l], g_k_fox[l])

    pad = BLOCK - N_META
    h_meta = jnp.concatenate([jnp.zeros((pad, d), F32), meta_tokens.astype(F32)], axis=0)
    tab_meta = _rope_tables(jnp.maximum(jnp.arange(BLOCK) - pad, 0), jnp.zeros((1,), jnp.int32))
    assert s % BLOCK == 0
    tab_tok = _rope_tables(N_META + BLOCK * jnp.arange(s // BLOCK), jnp.arange(BLOCK))

    h1, h_meta = _ffn_with_meta(x.reshape(b * s, d), h_meta, *ffn1)
    km, vmt = _proj(h_meta[None], consts, *tab_meta, is_meta=True)
    q, k, vt = _proj(h1.reshape(b, s, d), consts, *tab_tok, is_meta=False)
    attn = _attention(q, k, vt, km, vmt)
    out = _mix_ffn(h1, attn.reshape(b * s, d), w_out[l], *ffn2)
    return out.reshape(b, s, d)
```

```python
import functools
import math

import jax
import jax.numpy as jnp
import numpy as np
from jax import lax
from jax.experimental import pallas as pl
from jax.experimental.pallas import tpu as pltpu

F32 = jnp.float32
BF16 = jnp.bfloat16

D_MODEL = 1024
D_FF = 2816
N_META = 16
BLOCK = 128
EPS = 1e-6
NEG_INF = -1e30
MLA_HEADS = 8
MLA_Q_RANK = 256
MLA_KV_RANK = 128
MLA_NOPE = 64
MLA_ROPE = 32
MLA_QK = MLA_NOPE + MLA_ROPE
MLA_V = 64
ROPE_THETA = 10000.0
FOX_HEADS = 8
FOX_DIM = 64
N_HEADS = MLA_HEADS + FOX_HEADS
LANES = 128
HALF = LANES // 2
N_SPLIT = 3
LOG2E = math.log2(math.e)

FF_CHUNK = 256
FFN_TM = 512
PROJ_TM = 512
ATT_TQ = 2048
ATT_TK = 1024
DIAG_COLS = 256
VMEM_LIMIT = 56 * 1024 * 1024

_O_CQ = 0
_O_CKV = _O_CQ + MLA_Q_RANK
_O_KPE = _O_CKV + MLA_KV_RANK
_O_KPE_SW = _O_KPE + LANES
_O_FQ = _O_KPE_SW + LANES
_O_FK = _O_FQ + FOX_HEADS * FOX_DIM
_O_FL = _O_FK + FOX_HEADS * FOX_DIM
_D_IN_PAD = _O_FL + LANES

_NT = (((1,), (1,)), ((), ()))


def _const_spec(shape):
    nd = len(shape)
    return pl.BlockSpec(shape, lambda *_: (0,) * nd, pipeline_mode=pl.Buffered(1))


def _rms_scale(x, width):
    return lax.rsqrt(jnp.sum(x * x, axis=-1, keepdims=True) * (1.0 / width) + EPS)


def _split3(x):
    pieces = []
    r = x
    for _ in range(N_SPLIT):
        p = r.astype(BF16)
        pieces.append(p)
        r = r - p.astype(F32)
    return pieces


def _ffn_body(h, g_ref, wg_ref, wu_ref, wd_ref):
    n = (h * _rms_scale(h, D_MODEL) * g_ref[...]).astype(wg_ref.dtype)
    acc = jnp.zeros(h.shape, F32)
    for c in range(D_FF // FF_CHUNK):
        sl = slice(c * FF_CHUNK, (c + 1) * FF_CHUNK)
        g = jnp.dot(n, wg_ref[:, sl], preferred_element_type=F32)
        u = jnp.dot(n, wu_ref[:, sl], preferred_element_type=F32)
        a = (g * (1.0 / (1.0 + jnp.exp(-g))) * u).astype(wd_ref.dtype)
        acc = acc + jnp.dot(a, wd_ref[sl, :], preferred_element_type=F32)
    return h + 0.5 * acc


def _ffn_meta_kernel(h_ref, hm_ref, g_ref, wg_ref, wu_ref, wd_ref, o_ref, om_ref):
    o_ref[...] = _ffn_body(h_ref[...], g_ref, wg_ref, wu_ref, wd_ref)

    @pl.when(pl.program_id(0) == 0)
    def _():
        om_ref[...] = _ffn_body(hm_ref[...], g_ref, wg_ref, wu_ref, wd_ref)


def _mix_ffn_kernel(h_ref, a_ref, wo_ref, g_ref, wg_ref, wu_ref, wd_ref, o_ref):
    h = h_ref[...] + jnp.dot(a_ref[...].astype(wo_ref.dtype), wo_ref[...], preferred_element_type=F32)
    o_ref[...] = _ffn_body(h, g_ref, wg_ref, wu_ref, wd_ref)


def _ffn_call(kern, name, m, row_args, const_args, n_meta_rows=0):
    tm = FFN_TM
    assert m % tm == 0
    row = pl.BlockSpec((tm, D_MODEL), lambda i: (i, 0))
    out_specs, out_shape = row, jax.ShapeDtypeStruct((m, D_MODEL), F32)
    if n_meta_rows:
        out_specs = [row, pl.BlockSpec((n_meta_rows, D_MODEL), lambda i: (0, 0))]
        out_shape = [out_shape, jax.ShapeDtypeStruct((n_meta_rows, D_MODEL), F32)]
    return pl.pallas_call(
        kern,
        grid=(m // tm,),
        in_specs=[row] * len(row_args) + [_const_spec(a.shape) for a in const_args],
        out_specs=out_specs,
        out_shape=out_shape,
        compiler_params=pltpu.CompilerParams(
            dimension_semantics=("arbitrary",), vmem_limit_bytes=VMEM_LIMIT),
        name=name,
    )(*row_args, *const_args)


def _ffn_with_meta(h, h_meta, g, wg, wu, wd):
    return _ffn_call(_ffn_meta_kernel, "ffn", h.shape[0], (h,), (h_meta, g, wg, wu, wd),
                     n_meta_rows=h_meta.shape[0])


def _mix_ffn(h, attn, w_out, g, wg, wu, wd):
    return _ffn_call(_mix_ffn_kernel, "ffn_mix", h.shape[0], (h, attn), (w_out, g, wg, wu, wd))


def _half_rms_scale(x, lo):
    x2 = x * x
    s_lo = jnp.sum(jnp.where(lo, x2, 0.0), axis=-1, keepdims=True)
    s_hi = jnp.sum(jnp.where(lo, 0.0, x2), axis=-1, keepdims=True)
    return jnp.where(lo, lax.rsqrt(s_lo * (1.0 / FOX_DIM) + EPS), lax.rsqrt(s_hi * (1.0 / FOX_DIM) + EPS))


def _proj_kernel(h_ref, gmix_ref, win_ref, gcq_ref, wuq_ref, gckv_ref, wuk_ref, wuvt_ref, wfvt_ref,
                 gq_ref, gqs_ref, gk_ref, gks_ref, cc_ref, ss_ref, bf_ref, gfq_ref, gfk_ref,
                 tri_ref, place_ref, *rest, tm, is_meta):
    if is_meta:
        k_out, vt_out = rest
        q_out = carry_sc = None
    else:
        q_out, k_out, vt_out, cum_end_out, carry_sc = rest

    lane = lax.broadcasted_iota(jnp.int32, (tm, LANES), 1)
    lo = lane < HALF
    bias_e = ((lane >= HALF) & (lane < HALF + N_SPLIT)).astype(F32)
    bias_o = (lane < N_SPLIT).astype(F32)
    ones_row = (lax.broadcasted_iota(jnp.int32, (HALF, tm), 0) == 0).astype(F32)

    def put_vt(head, blk):
        rows = [blk, ones_row] if head % 2 == 0 else [ones_row, blk]
        vt_out[0, head, 0] = jnp.concatenate(rows, axis=0).astype(BF16)

    h = h_ref[0]
    n = (h * _rms_scale(h, D_MODEL) * gmix_ref[...]).astype(BF16)
    proj = jnp.dot(n, win_ref[...], preferred_element_type=F32)

    c_q = proj[:, _O_CQ:_O_CKV]
    c_kv = proj[:, _O_CKV:_O_KPE]
    kpe = proj[:, _O_KPE:_O_KPE_SW]
    kpe_sw = proj[:, _O_KPE_SW:_O_FQ]

    cc = cc_ref[...]
    ss = ss_ref[...]

    ckvn = (c_kv * _rms_scale(c_kv, MLA_KV_RANK) * gckv_ref[...]).astype(BF16)
    kn = jnp.dot(ckvn, wuk_ref[...], preferred_element_type=F32)
    kc = kpe_sw * gks_ref[...] * ss
    for hd in range(MLA_HEADS):
        x = kn[:, hd * LANES:(hd + 1) * LANES] + kpe
        r = _rms_scale(x, MLA_QK)
        k_out[0, hd] = (r * (x * gk_ref[...] * cc + kc)).astype(BF16)
    vt = lax.dot_general(wuvt_ref[...], ckvn, _NT, preferred_element_type=F32)
    for hd in range(MLA_HEADS):
        put_vt(hd, vt[hd * MLA_V:(hd + 1) * MLA_V, :])
    if not is_meta:
        cqn = (c_q * _rms_scale(c_q, MLA_Q_RANK) * gcq_ref[...]).astype(BF16)
        qq = jnp.dot(cqn, wuq_ref[...], preferred_element_type=F32)
        for hd in range(MLA_HEADS):
            x = qq[:, hd * LANES:(hd + 1) * LANES]
            xs = qq[:, (MLA_HEADS + hd) * LANES:(MLA_HEADS + hd + 1) * LANES]
            r = _rms_scale(x, MLA_QK)
            q_out[0, hd] = (r * (x * gq_ref[...] * cc + xs * gqs_ref[...] * ss)).astype(BF16)

    fl = proj[:, _O_FL:_O_FL + LANES] + bf_ref[...]
    log_f = jnp.minimum(fl, 0.0) - jnp.log1p(jnp.exp(-jnp.abs(fl)))
    tri = tri_ref[...]
    pieces = _split3(log_f)
    assert N_SPLIT == 3
    pair = jnp.dot(tri, jnp.concatenate(pieces[:2], axis=1), preferred_element_type=F32)
    cum = pair[:, :LANES] + pair[:, LANES:] + jnp.dot(tri, pieces[2], preferred_element_type=F32)
    if is_meta:
        cum = cum - cum[tm - 1:tm, :]
    else:
        @pl.when(pl.program_id(1) == 0)
        def _():
            carry_sc[...] = jnp.zeros_like(carry_sc)
        cum = cum + carry_sc[0:1, :]
        carry_sc[...] = jnp.broadcast_to(cum[tm - 1:tm, :], carry_sc.shape)
        cum_end_out[0, 0] = carry_sc[...]
    bias = jnp.dot(jnp.concatenate(_split3(cum * (-LOG2E)), axis=1), place_ref[...],
                   preferred_element_type=F32)
    fvt = lax.dot_general(wfvt_ref[...], n, _NT, preferred_element_type=F32)
    for p in range(FOX_HEADS // 2):
        he, ho = MLA_HEADS + 2 * p, MLA_HEADS + 2 * p + 1
        x = proj[:, _O_FK + p * LANES:_O_FK + (p + 1) * LANES]
        y = x * _half_rms_scale(x, lo) * gfk_ref[...]
        pair_bias = bias[:, p * LANES:(p + 1) * LANES]
        k_out[0, he] = jnp.where(lo, y, pair_bias).astype(BF16)
        k_out[0, ho] = jnp.where(lo, pair_bias, y).astype(BF16)
        put_vt(he, fvt[(2 * p) * FOX_DIM:(2 * p + 1) * FOX_DIM, :])
        put_vt(ho, fvt[(2 * p + 1) * FOX_DIM:(2 * p + 2) * FOX_DIM, :])
        if not is_meta:
            x = proj[:, _O_FQ + p * LANES:_O_FQ + (p + 1) * LANES]
            y = x * _half_rms_scale(x, lo) * gfq_ref[...]
            q_out[0, he] = jnp.where(lo, y, bias_e).astype(BF16)
            q_out[0, ho] = jnp.where(lo, bias_o, y).astype(BF16)


def _proj(h, consts, cc, ss, *, is_meta):
    b, s, _ = h.shape
    tm = min(PROJ_TM, s)
    (gmix, win, gcq, wuq, gckv, wuk, wuvt, wfvt, gq, gqs, gk, gks, bf, gfq, gfk, place) = consts
    tri = jnp.tril(jnp.ones((tm, tm), BF16))
    row_in = pl.BlockSpec((1, tm, D_MODEL), lambda bi, i: (bi, i, 0))
    tab = pl.BlockSpec((tm, LANES), lambda bi, i: (i, 0))
    head_out = pl.BlockSpec((1, N_HEADS, tm, LANES), lambda bi, i: (bi, 0, i, 0))
    head_shape = jax.ShapeDtypeStruct((b, N_HEADS, s, LANES), BF16)
    vt_out = pl.BlockSpec((1, N_HEADS, 1, LANES, tm), lambda bi, i: (bi, 0, i, 0, 0))
    vt_shape = jax.ShapeDtypeStruct((b, N_HEADS, s // tm, LANES, tm), BF16)
    vec = _const_spec((1, LANES))
    in_specs = [row_in, _const_spec((1, D_MODEL)), _const_spec(win.shape),
                _const_spec((1, MLA_Q_RANK)), _const_spec(wuq.shape),
                _const_spec((1, MLA_KV_RANK)), _const_spec(wuk.shape),
                _const_spec(wuvt.shape), _const_spec(wfvt.shape),
                vec, vec, vec, vec, tab, tab, vec, vec, vec,
                _const_spec((tm, tm)), _const_spec(place.shape)]
    if is_meta:
        out_specs, out_shape, scratch = [head_out, vt_out], [head_shape, vt_shape], []
    else:
        cum_end = pl.BlockSpec((1, 1, 8, LANES), lambda bi, i: (bi, i, 0, 0))
        out_specs = [head_out, head_out, vt_out, cum_end]
        out_shape = [head_shape, head_shape, vt_shape, jax.ShapeDtypeStruct((b, s // tm, 8, LANES), F32)]
        scratch = [pltpu.VMEM((8, LANES), F32)]
    return pl.pallas_call(
        functools.partial(_proj_kernel, tm=tm, is_meta=is_meta),
        grid=(b, s // tm),
        in_specs=in_specs,
        out_specs=out_specs,
        out_shape=out_shape,
        scratch_shapes=scratch,
        compiler_params=pltpu.CompilerParams(
            dimension_semantics=("arbitrary", "arbitrary"), vmem_limit_bytes=VMEM_LIMIT),
        name="proj_meta" if is_meta else "proj",
    )(h, gmix, win, gcq, wuq, gckv, wuk, wuvt, wfvt, gq, gqs, gk, gks, cc, ss, bf, gfq, gfk, tri, place)


def _attn_kernel(first_chunk_ref, q_ref, k_ref, vt_ref, km_ref, vmt_ref, o_ref, m_sc, acc_sc, *, tq, tk):
    i = pl.program_id(2)
    step = (pl.program_id(0) * pl.num_programs(1) + pl.program_id(1)) * pl.num_programs(2) + i
    vchunk = vt_ref.shape[-1]

    def values(hh, block, n_keys):
        n = n_keys // vchunk
        return jnp.concatenate([vt_ref[0, hh, block * n + c] for c in range(n)], axis=1)

    def scores(hh, kc, cols=slice(None)):
        return lax.dot_general(kc, q_ref[0, hh, cols, :], _NT, preferred_element_type=F32)

    def update(hh, s, vt, cols=slice(None)):
        m_old = m_sc[hh, :, cols]
        m_new = jnp.maximum(m_old, jnp.max(s, axis=0, keepdims=True))
        p = jnp.exp2(s - m_new).astype(BF16)
        pv = jnp.dot(vt, p, preferred_element_type=F32)
        acc_sc[hh, :, cols] = jnp.exp2(m_old - m_new) * acc_sc[hh, :, cols] + pv
        m_sc[hh, :, cols] = m_new

    m_sc[...] = jnp.full(m_sc.shape, NEG_INF, F32)
    acc_sc[...] = jnp.zeros(acc_sc.shape, F32)

    def body(j, carry):
        off = pl.multiple_of(j * tk, tk)
        s = [scores(hh, k_ref[0, hh, pl.ds(off, tk), :]) for hh in range(2)]
        for hh in range(2):
            update(hh, s[hh], values(hh, j, tk))
        return carry

    lax.fori_loop(first_chunk_ref[step], i * (tq // tk), body, 0)

    off = pl.multiple_of(i * tq, tq)
    vt = [values(hh, i, tq) for hh in range(2)]

    def diag_scores(c):
        nk = (c + 1) * DIAG_COLS
        row = lax.broadcasted_iota(jnp.int32, (nk + BLOCK, DIAG_COLS), 0)
        col = lax.broadcasted_iota(jnp.int32, (nk + BLOCK, DIAG_COLS), 1)
        valid = (row <= col + c * DIAG_COLS) | (row >= nk + BLOCK - N_META)
        out = []
        for hh in range(2):
            keys = jnp.concatenate([k_ref[0, hh, pl.ds(off, nk), :], km_ref[0, hh]], axis=0)
            out.append(jnp.where(valid, scores(hh, keys, slice(c * DIAG_COLS, nk)), NEG_INF))
        return out

    n_blocks = tq // DIAG_COLS
    ahead = 2
    pend = {}
    for n in range(n_blocks + ahead):
        if n < n_blocks:
            pend[n] = diag_scores(n)
        if n >= ahead:
            c = n - ahead
            nk = (c + 1) * DIAG_COLS
            s = pend.pop(c)
            for hh in range(2):
                vals = jnp.concatenate([vt[hh][:, :nk], vmt_ref[0, hh, 0]], axis=1)
                update(hh, s[hh], vals, slice(c * DIAG_COLS, nk))

    a_e = acc_sc[0]
    a_o = acc_sc[1]
    top = lax.broadcasted_iota(jnp.int32, (LANES, tq), 0) < HALF
    out_t = jnp.where(top, a_e / a_e[HALF:HALF + 1, :], a_o / a_o[0:1, :])
    o_ref[0] = out_t.T.astype(BF16)


def _first_live_chunk(cum_end, g_q_fox, g_k_fox, s):
    b = cum_end.shape[0]
    per = s // cum_end.shape[1]
    cum = cum_end[:, :, 0, :FOX_HEADS]
    n_q, n_k, ratio = s // ATT_TQ, s // ATT_TK, ATT_TQ // ATT_TK
    chunk_end = cum[:, ATT_TK // per - 1::ATT_TK // per]
    tile_start = jnp.concatenate([jnp.zeros((b, 1, FOX_HEADS), F32),
                                  cum[:, ATT_TQ // per - 1::ATT_TQ // per][:, :n_q - 1]], axis=1)
    decay = LOG2E * (chunk_end[:, None, :, :] - tile_start[:, :, None, :])
    bound = (64.0 * (1.0 + 2.0 ** -9) ** 2 * (LOG2E / math.sqrt(FOX_DIM))
             * jnp.max(jnp.abs(g_q_fox)) * jnp.max(jnp.abs(g_k_fox)))
    dead = decay > 2.0 * bound + 150.0
    dead = dead[..., 0::2] & dead[..., 1::2]
    before_tile = jnp.arange(n_k)[None, :] < ratio * jnp.arange(n_q)[:, None]
    first = jnp.sum(dead & before_tile[None, :, :, None], axis=2).astype(jnp.int32)
    first = jnp.concatenate([jnp.zeros((b, n_q, MLA_HEADS // 2), jnp.int32), first], axis=2)
    return jnp.transpose(first, (0, 2, 1)).reshape(-1)


def _attention(q, k, vt, km, vmt, first_chunk):
    b, _, s, _ = q.shape
    tq, tk = ATT_TQ, ATT_TK
    vchunk = vt.shape[-1]
    assert tq % tk == 0 and s % tq == 0 and tk % vchunk == 0
    pairs = N_HEADS // 2
    grid_spec = pltpu.PrefetchScalarGridSpec(
        num_scalar_prefetch=1,
        grid=(b, pairs, s // tq),
        in_specs=[
            pl.BlockSpec((1, 2, tq, LANES), lambda bi, p, i, fc: (bi, p, i, 0)),
            pl.BlockSpec((1, 2, s, LANES), lambda bi, p, i, fc: (bi, p, 0, 0)),
            pl.BlockSpec((1, 2, s // vchunk, LANES, vchunk), lambda bi, p, i, fc: (bi, p, 0, 0, 0)),
            pl.BlockSpec((1, 2, BLOCK, LANES), lambda bi, p, i, fc: (0, p, 0, 0)),
            pl.BlockSpec((1, 2, 1, LANES, BLOCK), lambda bi, p, i, fc: (0, p, 0, 0, 0)),
        ],
        out_specs=pl.BlockSpec((1, tq, LANES), lambda bi, p, i, fc: (bi, i, p)),
        scratch_shapes=[pltpu.VMEM((2, 1, tq), F32), pltpu.VMEM((2, LANES, tq), F32)],
    )
    return pl.pallas_call(
        functools.partial(_attn_kernel, tq=tq, tk=tk),
        grid_spec=grid_spec,
        out_shape=jax.ShapeDtypeStruct((b, s, pairs * LANES), BF16),
        compiler_params=pltpu.CompilerParams(
            dimension_semantics=("arbitrary", "arbitrary", "arbitrary"),
            vmem_limit_bytes=VMEM_LIMIT),
        name="attention",
    )(first_chunk, q, k, vt, km, vmt)


def _pad_lanes(x, offset=0):
    n = x.shape[-1]
    pad = [(0, 0)] * (x.ndim - 1) + [(offset, LANES - offset - n)]
    return jnp.pad(x, pad)


def _swap_halves(x):
    half = MLA_ROPE // 2
    return jnp.concatenate([x[..., half:], x[..., :half]], axis=-1)


def _rope_lane_gains(g, scale):
    g = g.astype(F32) * scale
    direct = _pad_lanes(g)
    paired = _pad_lanes(_swap_halves(g[MLA_NOPE:]), MLA_NOPE)
    return direct[None], paired[None]


def _rope_tables(pos_hi, pos_lo):
    half = MLA_ROPE // 2
    inv_freq = 1.0 / (ROPE_THETA ** (jnp.arange(half, dtype=F32) / half))
    off = jnp.zeros((MLA_NOPE,), F32)
    freq = jnp.concatenate([off, inv_freq, inv_freq, jnp.zeros((LANES - MLA_QK,), F32)])
    sign = jnp.concatenate([off, -jnp.ones((half,), F32), jnp.ones((half,), F32),
                            jnp.zeros((LANES - MLA_QK,), F32)])
    ang_hi = pos_hi.astype(F32)[:, None] * freq[None, :]
    ang_lo = pos_lo.astype(F32)[:, None] * freq[None, :]
    c_hi, s_hi = jnp.cos(ang_hi)[:, None, :], jnp.sin(ang_hi)[:, None, :]
    c_lo, s_lo = jnp.cos(ang_lo)[None, :, :], jnp.sin(ang_lo)[None, :, :]
    cc = (c_hi * c_lo - s_hi * s_lo).reshape(-1, LANES)
    ss = ((s_hi * c_lo + c_hi * s_lo) * sign).reshape(-1, LANES)
    return cc, ss


def _layout_params(g_mix, w_in, g_cq, w_uq, g_ckv, w_ukv, g_q_mla, g_k_mla, b_forget,
                   g_q_fox, g_k_fox):
    o_kpe = MLA_Q_RANK + MLA_KV_RANK
    o_fox = o_kpe + MLA_ROPE
    o_fv = o_fox + 2 * FOX_HEADS * FOX_DIM
    o_fl = o_fv + FOX_HEADS * FOX_DIM
    w_kpe = w_in[:, o_kpe:o_fox]
    win = jnp.concatenate([
        w_in[:, :o_kpe],
        _pad_lanes(w_kpe, MLA_NOPE),
        _pad_lanes(_swap_halves(w_kpe), MLA_NOPE),
        w_in[:, o_fox:o_fv],
        _pad_lanes(w_in[:, o_fl:]),
    ], axis=1).astype(BF16)
    assert win.shape[1] == _D_IN_PAD
    wfvt = w_in[:, o_fv:o_fl].T.astype(BF16)

    uq = w_uq.reshape(MLA_Q_RANK, MLA_HEADS, MLA_QK)
    uq_direct = _pad_lanes(uq).reshape(MLA_Q_RANK, MLA_HEADS * LANES)
    uq_paired = _pad_lanes(_swap_halves(uq[..., MLA_NOPE:]), MLA_NOPE).reshape(MLA_Q_RANK, MLA_HEADS * LANES)
    wuq = jnp.concatenate([uq_direct, uq_paired], axis=1).astype(BF16)

    ukv = w_ukv.reshape(MLA_KV_RANK, MLA_HEADS, MLA_NOPE + MLA_V)
    wuk = _pad_lanes(ukv[..., :MLA_NOPE]).reshape(MLA_KV_RANK, MLA_HEADS * LANES).astype(BF16)
    wuvt = ukv[..., MLA_NOPE:].reshape(MLA_KV_RANK, MLA_HEADS * MLA_V).T.astype(BF16)

    gq, gqs = _rope_lane_gains(g_q_mla, LOG2E / math.sqrt(MLA_QK))
    gk, gks = _rope_lane_gains(g_k_mla, 1.0)
    gfq = jnp.tile(g_q_fox.astype(F32) * (LOG2E / math.sqrt(FOX_DIM)), 2)[None]
    gfk = jnp.tile(g_k_fox.astype(F32), 2)[None]
    bf = _pad_lanes(b_forget.astype(F32))[None]

    place = np.zeros((N_SPLIT * LANES, FOX_HEADS // 2 * LANES), np.float32)
    for h in range(FOX_HEADS):
        base = HALF if h % 2 == 0 else 0
        for j in range(N_SPLIT):
            place[j * LANES + h, (h // 2) * LANES + base + j] = 1.0
    place = jnp.asarray(place, BF16)

    return (g_mix[None].astype(F32), win, g_cq[None].astype(F32), wuq, g_ckv[None].astype(F32),
            wuk, wuvt, wfvt, gq, gqs, gk, gks, bf, gfq, gfk, place)


def kernel(x, meta_tokens, g_ffn1, w1_gate, w1_up, w1_down, g_mix, w_in, g_cq, w_uq, g_ckv, w_ukv,
           g_q_mla, g_k_mla, b_forget, g_q_fox, g_k_fox, w_out, g_ffn2, w2_gate, w2_up, w2_down):
    b, s, d = x.shape
    depth = g_ffn1.shape[0]
    assert depth == 1 and d == D_MODEL
    l = 0

    ffn1 = (g_ffn1[l][None], w1_gate[l], w1_up[l], w1_down[l])
    ffn2 = (g_ffn2[l][None], w2_gate[l], w2_up[l], w2_down[l])
    consts = _layout_params(g_mix[l], w_in[l], g_cq[l], w_uq[l], g_ckv[l], w_ukv[l], g_q_mla[l],
                            g_k_mla[l], b_forget[l], g_q_fox[l], g_k_fox[l])

    pad = BLOCK - N_META
    h_meta = jnp.concatenate([jnp.zeros((pad, d), F32), meta_tokens.astype(F32)], axis=0)
    tab_meta = _rope_tables(jnp.maximum(jnp.arange(BLOCK) - pad, 0), jnp.zeros((1,), jnp.int32))
    assert s % BLOCK == 0
    tab_tok = _rope_tables(N_META + BLOCK * jnp.arange(s // BLOCK), jnp.arange(BLOCK))

    h1, h_meta = _ffn_with_meta(x.reshape(b * s, d), h_meta, *ffn1)
    km, vmt = _proj(h_meta[None], consts, *tab_meta, is_meta=True)
    q, k, vt, cum_end = _proj(h1.reshape(b, s, d), consts, *tab_tok, is_meta=False)
    attn = _attention(q, k, vt, km, vmt, _first_live_chunk(cum_end, g_q_fox[l], g_k_fox[l], s))
    out = _mix_ffn(h1, attn.reshape(b * s, d), w_out[l], *ffn2)
    return out.reshape(b, s, d)
```

```python
import functools
import math

import jax
import jax.numpy as jnp
import numpy as np
from jax import lax
from jax.experimental import pallas as pl
from jax.experimental.pallas import tpu as pltpu

F32 = jnp.float32
BF16 = jnp.bfloat16

D_MODEL = 1024
D_FF = 2816
N_META = 16
BLOCK = 128
EPS = 1e-6
NEG_INF = -1e30
MLA_HEADS = 8
MLA_Q_RANK = 256
MLA_KV_RANK = 128
MLA_NOPE = 64
MLA_ROPE = 32
MLA_QK = MLA_NOPE + MLA_ROPE
MLA_V = 64
ROPE_THETA = 10000.0
FOX_HEADS = 8
FOX_DIM = 64
N_HEADS = MLA_HEADS + FOX_HEADS
LANES = 128
HALF = LANES // 2
N_SPLIT = 3
LOG2E = math.log2(math.e)

FF_CHUNK = 256
FFN_TM = 512
PROJ_TM = 512
ATT_TQ = 2048
ATT_TK = 1024
DIAG_COLS = 256
VMEM_LIMIT = 56 * 1024 * 1024

_O_CQ = 0
_O_CKV = _O_CQ + MLA_Q_RANK
_O_KPE = _O_CKV + MLA_KV_RANK
_O_KPE_SW = _O_KPE + LANES
_O_FQ = _O_KPE_SW + LANES
_O_FK = _O_FQ + FOX_HEADS * FOX_DIM
_O_FL = _O_FK + FOX_HEADS * FOX_DIM
_D_IN_PAD = _O_FL + LANES

_NT = (((1,), (1,)), ((), ()))


def _const_spec(shape):
    nd = len(shape)
    return pl.BlockSpec(shape, lambda *_: (0,) * nd, pipeline_mode=pl.Buffered(1))


def _rms_scale(x, width):
    return lax.rsqrt(jnp.sum(x * x, axis=-1, keepdims=True) * (1.0 / width) + EPS)


def _split3(x):
    pieces = []
    r = x
    for _ in range(N_SPLIT):
        p = r.astype(BF16)
        pieces.append(p)
        r = r - p.astype(F32)
    return pieces


def _ffn_body(h, g_ref, wg_ref, wu_ref, wd_ref):
    n = (h * _rms_scale(h, D_MODEL) * g_ref[...]).astype(wg_ref.dtype)
    acc = jnp.zeros(h.shape, F32)
    for c in range(D_FF // FF_CHUNK):
        sl = slice(c * FF_CHUNK, (c + 1) * FF_CHUNK)
        g = jnp.dot(n, wg_ref[:, sl], preferred_element_type=F32)
        u = jnp.dot(n, wu_ref[:, sl], preferred_element_type=F32)
        a = (g * (1.0 / (1.0 + jnp.exp(-g))) * u).astype(wd_ref.dtype)
        acc = acc + jnp.dot(a, wd_ref[sl, :], preferred_element_type=F32)
    return h + 0.5 * acc


def _ffn_meta_kernel(h_ref, hm_ref, g_ref, wg_ref, wu_ref, wd_ref, o_ref, om_ref):
    o_ref[...] = _ffn_body(h_ref[...], g_ref, wg_ref, wu_ref, wd_ref)

    @pl.when(pl.program_id(0) == 0)
    def _():
        om_ref[...] = _ffn_body(hm_ref[...], g_ref, wg_ref, wu_ref, wd_ref)


def _mix_ffn_kernel(h_ref, a_ref, wo_ref, g_ref, wg_ref, wu_ref, wd_ref, o_ref):
    h = h_ref[...] + jnp.dot(a_ref[...].astype(wo_ref.dtype), wo_ref[...], preferred_element_type=F32)
    o_ref[...] = _ffn_body(h, g_ref, wg_ref, wu_ref, wd_ref)


def _ffn_call(kern, name, m, row_args, const_args, n_meta_rows=0):
    tm = FFN_TM
    assert m % tm == 0
    row = pl.BlockSpec((tm, D_MODEL), lambda i: (i, 0))
    out_specs, out_shape = row, jax.ShapeDtypeStruct((m, D_MODEL), F32)
    if n_meta_rows:
        out_specs = [row, pl.BlockSpec((n_meta_rows, D_MODEL), lambda i: (0, 0))]
        out_shape = [out_shape, jax.ShapeDtypeStruct((n_meta_rows, D_MODEL), F32)]
    return pl.pallas_call(
        kern,
        grid=(m // tm,),
        in_specs=[row] * len(row_args) + [_const_spec(a.shape) for a in const_args],
        out_specs=out_specs,
        out_shape=out_shape,
        compiler_params=pltpu.CompilerParams(
            dimension_semantics=("arbitrary",), vmem_limit_bytes=VMEM_LIMIT),
        name=name,
    )(*row_args, *const_args)


def _ffn_with_meta(h, h_meta, g, wg, wu, wd):
    return _ffn_call(_ffn_meta_kernel, "ffn", h.shape[0], (h,), (h_meta, g, wg, wu, wd),
                     n_meta_rows=h_meta.shape[0])


def _mix_ffn(h, attn, w_out, g, wg, wu, wd):
    return _ffn_call(_mix_ffn_kernel, "ffn_mix", h.shape[0], (h, attn), (w_out, g, wg, wu, wd))


def _half_rms_scale(x, lo):
    x2 = x * x
    s_lo = jnp.sum(jnp.where(lo, x2, 0.0), axis=-1, keepdims=True)
    s_hi = jnp.sum(jnp.where(lo, 0.0, x2), axis=-1, keepdims=True)
    return jnp.where(lo, lax.rsqrt(s_lo * (1.0 / FOX_DIM) + EPS), lax.rsqrt(s_hi * (1.0 / FOX_DIM) + EPS))


def _proj_kernel(h_ref, gmix_ref, win_ref, gcq_ref, wuq_ref, gckv_ref, wuk_ref, wuvt_ref, wfvt_ref,
                 gq_ref, gqs_ref, gk_ref, gks_ref, cc_ref, ss_ref, bf_ref, gfq_ref, gfk_ref,
                 tri_ref, place_ref, *rest, tm, is_meta):
    if is_meta:
        k_out, vt_out = rest
        q_out = carry_sc = None
    else:
        q_out, k_out, vt_out, cum_end_out, carry_sc = rest

    lane = lax.broadcasted_iota(jnp.int32, (tm, LANES), 1)
    lo = lane < HALF
    bias_e = ((lane >= HALF) & (lane < HALF + N_SPLIT)).astype(F32)
    bias_o = (lane < N_SPLIT).astype(F32)
    ones_row = (lax.broadcasted_iota(jnp.int32, (HALF, tm), 0) == 0).astype(F32)

    def put_vt(head, blk):
        rows = [blk, ones_row] if head % 2 == 0 else [ones_row, blk]
        vt_out[0, head, 0] = jnp.concatenate(rows, axis=0).astype(BF16)

    h = h_ref[0]
    n = (h * _rms_scale(h, D_MODEL) * gmix_ref[...]).astype(BF16)
    proj = jnp.dot(n, win_ref[...], preferred_element_type=F32)

    c_q = proj[:, _O_CQ:_O_CKV]
    c_kv = proj[:, _O_CKV:_O_KPE]
    kpe = proj[:, _O_KPE:_O_KPE_SW]
    kpe_sw = proj[:, _O_KPE_SW:_O_FQ]

    cc = cc_ref[...]
    ss = ss_ref[...]

    ckvn = (c_kv * _rms_scale(c_kv, MLA_KV_RANK) * gckv_ref[...]).astype(BF16)
    kn = jnp.dot(ckvn, wuk_ref[...], preferred_element_type=F32)
    kc = kpe_sw * gks_ref[...] * ss
    for hd in range(MLA_HEADS):
        x = kn[:, hd * LANES:(hd + 1) * LANES] + kpe
        r = _rms_scale(x, MLA_QK)
        k_out[0, hd] = (r * (x * gk_ref[...] * cc + kc)).astype(BF16)
    vt = lax.dot_general(wuvt_ref[...], ckvn, _NT, preferred_element_type=F32)
    for hd in range(MLA_HEADS):
        put_vt(hd, vt[hd * MLA_V:(hd + 1) * MLA_V, :])
    if not is_meta:
        cqn = (c_q * _rms_scale(c_q, MLA_Q_RANK) * gcq_ref[...]).astype(BF16)
        qq = jnp.dot(cqn, wuq_ref[...], preferred_element_type=F32)
        for hd in range(MLA_HEADS):
            x = qq[:, hd * LANES:(hd + 1) * LANES]
            xs = qq[:, (MLA_HEADS + hd) * LANES:(MLA_HEADS + hd + 1) * LANES]
            r = _rms_scale(x, MLA_QK)
            q_out[0, hd] = (r * (x * gq_ref[...] * cc + xs * gqs_ref[...] * ss)).astype(BF16)

    fl = proj[:, _O_FL:_O_FL + LANES] + bf_ref[...]
    log_f = jnp.minimum(fl, 0.0) - jnp.log1p(jnp.exp(-jnp.abs(fl)))
    tri = tri_ref[...]
    pieces = _split3(log_f)
    assert N_SPLIT == 3
    pair = jnp.dot(tri, jnp.concatenate(pieces[:2], axis=1), preferred_element_type=F32)
    cum = pair[:, :LANES] + pair[:, LANES:] + jnp.dot(tri, pieces[2], preferred_element_type=F32)
    if is_meta:
        cum = cum - cum[tm - 1:tm, :]
    else:
        @pl.when(pl.program_id(1) == 0)
        def _():
            carry_sc[...] = jnp.zeros_like(carry_sc)
        cum = cum + carry_sc[0:1, :]
        carry_sc[...] = jnp.broadcast_to(cum[tm - 1:tm, :], carry_sc.shape)
        cum_end_out[0, 0] = carry_sc[...]
    bias = jnp.dot(jnp.concatenate(_split3(cum * (-LOG2E)), axis=1), place_ref[...],
                   preferred_element_type=F32)
    fvt = lax.dot_general(wfvt_ref[...], n, _NT, preferred_element_type=F32)
    for p in range(FOX_HEADS // 2):
        he, ho = MLA_HEADS + 2 * p, MLA_HEADS + 2 * p + 1
        x = proj[:, _O_FK + p * LANES:_O_FK + (p + 1) * LANES]
        y = x * _half_rms_scale(x, lo) * gfk_ref[...]
        pair_bias = bias[:, p * LANES:(p + 1) * LANES]
        k_out[0, he] = jnp.where(lo, y, pair_bias).astype(BF16)
        k_out[0, ho] = jnp.where(lo, pair_bias, y).astype(BF16)
        put_vt(he, fvt[(2 * p) * FOX_DIM:(2 * p + 1) * FOX_DIM, :])
        put_vt(ho, fvt[(2 * p + 1) * FOX_DIM:(2 * p + 2) * FOX_DIM, :])
        if not is_meta:
            x = proj[:, _O_FQ + p * LANES:_O_FQ + (p + 1) * LANES]
            y = x * _half_rms_scale(x, lo) * gfq_ref[...]
            q_out[0, he] = jnp.where(lo, y, bias_e).astype(BF16)
            q_out[0, ho] = jnp.where(lo, bias_o, y).astype(BF16)


def _proj(h, consts, cc, ss, *, is_meta):
    b, s, _ = h.shape
    tm = min(PROJ_TM, s)
    (gmix, win, gcq, wuq, gckv, wuk, wuvt, wfvt, gq, gqs, gk, gks, bf, gfq, gfk, place) = consts
    tri = jnp.tril(jnp.ones((tm, tm), BF16))
    row_in = pl.BlockSpec((1, tm, D_MODEL), lambda bi, i: (bi, i, 0))
    tab = pl.BlockSpec((tm, LANES), lambda bi, i: (i, 0))
    head_out = pl.BlockSpec((1, N_HEADS, tm, LANES), lambda bi, i: (bi, 0, i, 0))
    head_shape = jax.ShapeDtypeStruct((b, N_HEADS, s, LANES), BF16)
    vt_out = pl.BlockSpec((1, N_HEADS, 1, LANES, tm), lambda bi, i: (bi, 0, i, 0, 0))
    vt_shape = jax.ShapeDtypeStruct((b, N_HEADS, s // tm, LANES, tm), BF16)
    vec = _const_spec((1, LANES))
    in_specs = [row_in, _const_spec((1, D_MODEL)), _const_spec(win.shape),
                _const_spec((1, MLA_Q_RANK)), _const_spec(wuq.shape),
                _const_spec((1, MLA_KV_RANK)), _const_spec(wuk.shape),
                _const_spec(wuvt.shape), _const_spec(wfvt.shape),
                vec, vec, vec, vec, tab, tab, vec, vec, vec,
                _const_spec((tm, tm)), _const_spec(place.shape)]
    if is_meta:
        out_specs, out_shape, scratch = [head_out, vt_out], [head_shape, vt_shape], []
    else:
        cum_end = pl.BlockSpec((1, 1, 8, LANES), lambda bi, i: (bi, i, 0, 0))
        out_specs = [head_out, head_out, vt_out, cum_end]
        out_shape = [head_shape, head_shape, vt_shape, jax.ShapeDtypeStruct((b, s // tm, 8, LANES), F32)]
        scratch = [pltpu.VMEM((8, LANES), F32)]
    return pl.pallas_call(
        functools.partial(_proj_kernel, tm=tm, is_meta=is_meta),
        grid=(b, s // tm),
        in_specs=in_specs,
        out_specs=out_specs,
        out_shape=out_shape,
        scratch_shapes=scratch,
        compiler_params=pltpu.CompilerParams(
            dimension_semantics=("arbitrary", "arbitrary"), vmem_limit_bytes=VMEM_LIMIT),
        name="proj_meta" if is_meta else "proj",
    )(h, gmix, win, gcq, wuq, gckv, wuk, wuvt, wfvt, gq, gqs, gk, gks, cc, ss, bf, gfq, gfk, tri, place)


def _attn_kernel(first_chunk_ref, q_ref, k_ref, vt_ref, km_ref, vmt_ref, o_ref, m_sc, acc_sc, *, tq, tk):
    i = pl.program_id(2)
    head0 = 2 * (pl.program_id(0) * pl.num_programs(1) + pl.program_id(1))
    vchunk = vt_ref.shape[-1]

    def values(hh, block, n_keys):
        n = n_keys // vchunk
        return jnp.concatenate([vt_ref[0, hh, block * n + c] for c in range(n)], axis=1)

    def scores(hh, kc, cols=slice(None)):
        return lax.dot_general(kc, q_ref[0, hh, cols, :], _NT, preferred_element_type=F32)

    def update(hh, s, vt, cols=slice(None)):
        m_old = m_sc[hh, :, cols]
        m_new = jnp.maximum(m_old, jnp.max(s, axis=0, keepdims=True))
        p = jnp.exp2(s - m_new).astype(BF16)
        pv = jnp.dot(vt, p, preferred_element_type=F32)
        acc_sc[hh, :, cols] = jnp.exp2(m_old - m_new) * acc_sc[hh, :, cols] + pv
        m_sc[hh, :, cols] = m_new

    m_sc[...] = jnp.full(m_sc.shape, NEG_INF, F32)
    acc_sc[...] = jnp.zeros(acc_sc.shape, F32)

    def chunk_body(heads):
        def body(j, carry):
            off = pl.multiple_of(j * tk, tk)
            s = [scores(hh, k_ref[0, hh, pl.ds(off, tk), :]) for hh in heads]
            for hh, s_hh in zip(heads, s):
                update(hh, s_hh, values(hh, j, tk))
            return carry
        return body

    first = [first_chunk_ref[(head0 + hh) * pl.num_programs(2) + i] for hh in range(2)]
    both = jnp.maximum(first[0], first[1])
    for hh in range(2):
        lax.fori_loop(first[hh], both, chunk_body((hh,)), 0)
    lax.fori_loop(both, i * (tq // tk), chunk_body((0, 1)), 0)

    off = pl.multiple_of(i * tq, tq)
    vt = [values(hh, i, tq) for hh in range(2)]

    def diag_scores(c):
        nk = (c + 1) * DIAG_COLS
        row = lax.broadcasted_iota(jnp.int32, (nk + BLOCK, DIAG_COLS), 0)
        col = lax.broadcasted_iota(jnp.int32, (nk + BLOCK, DIAG_COLS), 1)
        valid = (row <= col + c * DIAG_COLS) | (row >= nk + BLOCK - N_META)
        out = []
        for hh in range(2):
            keys = jnp.concatenate([k_ref[0, hh, pl.ds(off, nk), :], km_ref[0, hh]], axis=0)
            out.append(jnp.where(valid, scores(hh, keys, slice(c * DIAG_COLS, nk)), NEG_INF))
        return out

    n_blocks = tq // DIAG_COLS
    ahead = 2
    pend = {}
    for n in range(n_blocks + ahead):
        if n < n_blocks:
            pend[n] = diag_scores(n)
        if n >= ahead:
            c = n - ahead
            nk = (c + 1) * DIAG_COLS
            s = pend.pop(c)
            for hh in range(2):
                vals = jnp.concatenate([vt[hh][:, :nk], vmt_ref[0, hh, 0]], axis=1)
                update(hh, s[hh], vals, slice(c * DIAG_COLS, nk))

    a_e = acc_sc[0]
    a_o = acc_sc[1]
    top = lax.broadcasted_iota(jnp.int32, (LANES, tq), 0) < HALF
    out_t = jnp.where(top, a_e / a_e[HALF:HALF + 1, :], a_o / a_o[0:1, :])
    o_ref[0] = out_t.T.astype(BF16)


def _first_live_chunk(cum_end, g_q_fox, g_k_fox, s):
    b = cum_end.shape[0]
    per = s // cum_end.shape[1]
    cum = cum_end[:, :, 0, :FOX_HEADS]
    n_q, n_k, ratio = s // ATT_TQ, s // ATT_TK, ATT_TQ // ATT_TK
    chunk_end = cum[:, ATT_TK // per - 1::ATT_TK // per]
    tile_start = jnp.concatenate([jnp.zeros((b, 1, FOX_HEADS), F32),
                                  cum[:, ATT_TQ // per - 1::ATT_TQ // per][:, :n_q - 1]], axis=1)
    decay = LOG2E * (chunk_end[:, None, :, :] - tile_start[:, :, None, :])
    bound = (64.0 * (1.0 + 2.0 ** -9) ** 2 * (LOG2E / math.sqrt(FOX_DIM))
             * jnp.max(jnp.abs(g_q_fox)) * jnp.max(jnp.abs(g_k_fox)))
    dead = decay > 2.0 * bound + 150.0
    before_tile = jnp.arange(n_k)[None, :] < ratio * jnp.arange(n_q)[:, None]
    first = jnp.sum(dead & before_tile[None, :, :, None], axis=2).astype(jnp.int32)
    first = jnp.concatenate([jnp.zeros((b, n_q, MLA_HEADS), jnp.int32), first], axis=2)
    return jnp.transpose(first, (0, 2, 1)).reshape(-1)


def _attention(q, k, vt, km, vmt, first_chunk):
    b, _, s, _ = q.shape
    tq, tk = ATT_TQ, ATT_TK
    vchunk = vt.shape[-1]
    assert tq % tk == 0 and s % tq == 0 and tk % vchunk == 0
    pairs = N_HEADS // 2
    grid_spec = pltpu.PrefetchScalarGridSpec(
        num_scalar_prefetch=1,
        grid=(b, pairs, s // tq),
        in_specs=[
            pl.BlockSpec((1, 2, tq, LANES), lambda bi, p, i, fc: (bi, p, i, 0)),
            pl.BlockSpec((1, 2, s, LANES), lambda bi, p, i, fc: (bi, p, 0, 0)),
            pl.BlockSpec((1, 2, s // vchunk, LANES, vchunk), lambda bi, p, i, fc: (bi, p, 0, 0, 0)),
            pl.BlockSpec((1, 2, BLOCK, LANES), lambda bi, p, i, fc: (0, p, 0, 0)),
            pl.BlockSpec((1, 2, 1, LANES, BLOCK), lambda bi, p, i, fc: (0, p, 0, 0, 0)),
        ],
        out_specs=pl.BlockSpec((1, tq, LANES), lambda bi, p, i, fc: (bi, i, p)),
        scratch_shapes=[pltpu.VMEM((2, 1, tq), F32), pltpu.VMEM((2, LANES, tq), F32)],
    )
    return pl.pallas_call(
        functools.partial(_attn_kernel, tq=tq, tk=tk),
        grid_spec=grid_spec,
        out_shape=jax.ShapeDtypeStruct((b, s, pairs * LANES), BF16),
        compiler_params=pltpu.CompilerParams(
            dimension_semantics=("arbitrary", "arbitrary", "arbitrary"),
            vmem_limit_bytes=VMEM_LIMIT),
        name="attention",
    )(first_chunk, q, k, vt, km, vmt)


def _pad_lanes(x, offset=0):
    n = x.shape[-1]
    pad = [(0, 0)] * (x.ndim - 1) + [(offset, LANES - offset - n)]
    return jnp.pad(x, pad)


def _swap_halves(x):
    half = MLA_ROPE // 2
    return jnp.concatenate([x[..., half:], x[..., :half]], axis=-1)


def _rope_lane_gains(g, scale):
    g = g.astype(F32) * scale
    direct = _pad_lanes(g)
    paired = _pad_lanes(_swap_halves(g[MLA_NOPE:]), MLA_NOPE)
    return direct[None], paired[None]


def _rope_tables(pos_hi, pos_lo):
    half = MLA_ROPE // 2
    inv_freq = 1.0 / (ROPE_THETA ** (jnp.arange(half, dtype=F32) / half))
    off = jnp.zeros((MLA_NOPE,), F32)
    freq = jnp.concatenate([off, inv_freq, inv_freq, jnp.zeros((LANES - MLA_QK,), F32)])
    sign = jnp.concatenate([off, -jnp.ones((half,), F32), jnp.ones((half,), F32),
                            jnp.zeros((LANES - MLA_QK,), F32)])
    ang_hi = pos_hi.astype(F32)[:, None] * freq[None, :]
    ang_lo = pos_lo.astype(F32)[:, None] * freq[None, :]
    c_hi, s_hi = jnp.cos(ang_hi)[:, None, :], jnp.sin(ang_hi)[:, None, :]
    c_lo, s_lo = jnp.cos(ang_lo)[None, :, :], jnp.sin(ang_lo)[None, :, :]
    cc = (c_hi * c_lo - s_hi * s_lo).reshape(-1, LANES)
    ss = ((s_hi * c_lo + c_hi * s_lo) * sign).reshape(-1, LANES)
    return cc, ss


def _layout_params(g_mix, w_in, g_cq, w_uq, g_ckv, w_ukv, g_q_mla, g_k_mla, b_forget,
                   g_q_fox, g_k_fox):
    o_kpe = MLA_Q_RANK + MLA_KV_RANK
    o_fox = o_kpe + MLA_ROPE
    o_fv = o_fox + 2 * FOX_HEADS * FOX_DIM
    o_fl = o_fv + FOX_HEADS * FOX_DIM
    w_kpe = w_in[:, o_kpe:o_fox]
    win = jnp.concatenate([
        w_in[:, :o_kpe],
        _pad_lanes(w_kpe, MLA_NOPE),
        _pad_lanes(_swap_halves(w_kpe), MLA_NOPE),
        w_in[:, o_fox:o_fv],
        _pad_lanes(w_in[:, o_fl:]),
    ], axis=1).astype(BF16)
    assert win.shape[1] == _D_IN_PAD
    wfvt = w_in[:, o_fv:o_fl].T.astype(BF16)

    uq = w_uq.reshape(MLA_Q_RANK, MLA_HEADS, MLA_QK)
    uq_direct = _pad_lanes(uq).reshape(MLA_Q_RANK, MLA_HEADS * LANES)
    uq_paired = _pad_lanes(_swap_halves(uq[..., MLA_NOPE:]), MLA_NOPE).reshape(MLA_Q_RANK, MLA_HEADS * LANES)
    wuq = jnp.concatenate([uq_direct, uq_paired], axis=1).astype(BF16)

    ukv = w_ukv.reshape(MLA_KV_RANK, MLA_HEADS, MLA_NOPE + MLA_V)
    wuk = _pad_lanes(ukv[..., :MLA_NOPE]).reshape(MLA_KV_RANK, MLA_HEADS * LANES).astype(BF16)
    wuvt = ukv[..., MLA_NOPE:].reshape(MLA_KV_RANK, MLA_HEADS * MLA_V).T.astype(BF16)

    gq, gqs = _rope_lane_gains(g_q_mla, LOG2E / math.sqrt(MLA_QK))
    gk, gks = _rope_lane_gains(g_k_mla, 1.0)
    gfq = jnp.tile(g_q_fox.astype(F32) * (LOG2E / math.sqrt(FOX_DIM)), 2)[None]
    gfk = jnp.tile(g_k_fox.astype(F32), 2)[None]
    bf = _pad_lanes(b_forget.astype(F32))[None]

    place = np.zeros((N_SPLIT * LANES, FOX_HEADS // 2 * LANES), np.float32)
    for h in range(FOX_HEADS):
        base = HALF if h % 2 == 0 else 0
        for j in range(N_SPLIT):
            place[j * LANES + h, (h // 2) * LANES + base + j] = 1.0
    place = jnp.asarray(place, BF16)

    return (g_mix[None].astype(F32), win, g_cq[None].astype(F32), wuq, g_ckv[None].astype(F32),
            wuk, wuvt, wfvt, gq, gqs, gk, gks, bf, gfq, gfk, place)


def kernel(x, meta_tokens, g_ffn1, w1_gate, w1_up, w1_down, g_mix, w_in, g_cq, w_uq, g_ckv, w_ukv,
           g_q_mla, g_k_mla, b_forget, g_q_fox, g_k_fox, w_out, g_ffn2, w2_gate, w2_up, w2_down):
    b, s, d = x.shape
    depth = g_ffn1.shape[0]
    assert depth == 1 and d == D_MODEL
    l = 0

    ffn1 = (g_ffn1[l][None], w1_gate[l], w1_up[l], w1_down[l])
    ffn2 = (g_ffn2[l][None], w2_gate[l], w2_up[l], w2_down[l])
    consts = _layout_params(g_mix[l], w_in[l], g_cq[l], w_uq[l], g_ckv[l], w_ukv[l], g_q_mla[l],
                            g_k_mla[l], b_forget[l], g_q_fox[l], g_k_fox[l])

    pad = BLOCK - N_META
    h_meta = jnp.concatenate([jnp.zeros((pad, d), F32), meta_tokens.astype(F32)], axis=0)
    tab_meta = _rope_tables(jnp.maximum(jnp.arange(BLOCK) - pad, 0), jnp.zeros((1,), jnp.int32))
    assert s % BLOCK == 0
    tab_tok = _rope_tables(N_META + BLOCK * jnp.arange(s // BLOCK), jnp.arange(BLOCK))

    h1, h_meta = _ffn_with_meta(x.reshape(b * s, d), h_meta, *ffn1)
    km, vmt = _proj(h_meta[None], consts, *tab_meta, is_meta=True)
    q, k, vt, cum_end = _proj(h1.reshape(b, s, d), consts, *tab_tok, is_meta=False)
    attn = _attention(q, k, vt, km, vmt, _first_live_chunk(cum_end, g_q_fox[l], g_k_fox[l], s))
    out = _mix_ffn(h1, attn.reshape(b * s, d), w_out[l], *ffn2)
    return out.reshape(b, s, d)
```

```python
import functools
import math

import jax
import jax.numpy as jnp
import numpy as np
from jax import lax
from jax.experimental import pallas as pl
from jax.experimental.pallas import tpu as pltpu

F32 = jnp.float32
BF16 = jnp.bfloat16

D_MODEL = 1024
D_FF = 2816
N_META = 16
BLOCK = 128
EPS = 1e-6
NEG_INF = -1e30
MLA_HEADS = 8
MLA_Q_RANK = 256
MLA_KV_RANK = 128
MLA_NOPE = 64
MLA_ROPE = 32
MLA_QK = MLA_NOPE + MLA_ROPE
MLA_V = 64
ROPE_THETA = 10000.0
FOX_HEADS = 8
FOX_DIM = 64
N_HEADS = MLA_HEADS + FOX_HEADS
LANES = 128
HALF = LANES // 2
N_SPLIT = 3
LOG2E = math.log2(math.e)

FF_CHUNK = 256
FFN_TM = 512
PROJ_TM = 512
ATT_TQ = 2048
ATT_TK = 1024
DIAG_COLS = 256
VMEM_LIMIT = 56 * 1024 * 1024

_O_CQ = 0
_O_CKV = _O_CQ + MLA_Q_RANK
_O_KPE = _O_CKV + MLA_KV_RANK
_O_KPE_SW = _O_KPE + LANES
_O_FQ = _O_KPE_SW + LANES
_O_FK = _O_FQ + FOX_HEADS * FOX_DIM
_O_FL = _O_FK + FOX_HEADS * FOX_DIM
_D_IN_PAD = _O_FL + LANES

_NT = (((1,), (1,)), ((), ()))


def _const_spec(shape):
    nd = len(shape)
    return pl.BlockSpec(shape, lambda *_: (0,) * nd, pipeline_mode=pl.Buffered(1))


def _rms_scale(x, width):
    return lax.rsqrt(jnp.sum(x * x, axis=-1, keepdims=True) * (1.0 / width) + EPS)


def _split3(x):
    pieces = []
    r = x
    for _ in range(N_SPLIT):
        p = r.astype(BF16)
        pieces.append(p)
        r = r - p.astype(F32)
    return pieces


def _ffn_body(h, g_ref, wg_ref, wu_ref, wd_ref):
    n = (h * _rms_scale(h, D_MODEL) * g_ref[...]).astype(wg_ref.dtype)
    acc = jnp.zeros(h.shape, F32)
    for c in range(D_FF // FF_CHUNK):
        sl = slice(c * FF_CHUNK, (c + 1) * FF_CHUNK)
        g = jnp.dot(n, wg_ref[:, sl], preferred_element_type=F32)
        u = jnp.dot(n, wu_ref[:, sl], preferred_element_type=F32)
        a = (g * (1.0 / (1.0 + jnp.exp(-g))) * u).astype(wd_ref.dtype)
        acc = acc + jnp.dot(a, wd_ref[sl, :], preferred_element_type=F32)
    return h + 0.5 * acc


def _ffn_meta_kernel(h_ref, hm_ref, g_ref, wg_ref, wu_ref, wd_ref, o_ref, om_ref):
    o_ref[...] = _ffn_body(h_ref[...], g_ref, wg_ref, wu_ref, wd_ref)

    @pl.when(pl.program_id(0) == 0)
    def _():
        om_ref[...] = _ffn_body(hm_ref[...], g_ref, wg_ref, wu_ref, wd_ref)


def _mix_ffn_kernel(h_ref, a_ref, wo_ref, g_ref, wg_ref, wu_ref, wd_ref, o_ref):
    h = h_ref[...] + jnp.dot(a_ref[...].astype(wo_ref.dtype), wo_ref[...], preferred_element_type=F32)
    o_ref[...] = _ffn_body(h, g_ref, wg_ref, wu_ref, wd_ref)


def _ffn_call(kern, name, m, row_args, const_args, n_meta_rows=0):
    tm = FFN_TM
    assert m % tm == 0
    row = pl.BlockSpec((tm, D_MODEL), lambda i: (i, 0))
    out_specs, out_shape = row, jax.ShapeDtypeStruct((m, D_MODEL), F32)
    if n_meta_rows:
        out_specs = [row, pl.BlockSpec((n_meta_rows, D_MODEL), lambda i: (0, 0))]
        out_shape = [out_shape, jax.ShapeDtypeStruct((n_meta_rows, D_MODEL), F32)]
    return pl.pallas_call(
        kern,
        grid=(m // tm,),
        in_specs=[row] * len(row_args) + [_const_spec(a.shape) for a in const_args],
        out_specs=out_specs,
        out_shape=out_shape,
        compiler_params=pltpu.CompilerParams(
            dimension_semantics=("arbitrary",), vmem_limit_bytes=VMEM_LIMIT),
        name=name,
    )(*row_args, *const_args)


def _ffn_with_meta(h, h_meta, g, wg, wu, wd):
    return _ffn_call(_ffn_meta_kernel, "ffn", h.shape[0], (h,), (h_meta, g, wg, wu, wd),
                     n_meta_rows=h_meta.shape[0])


def _mix_ffn(h, attn, w_out, g, wg, wu, wd):
    return _ffn_call(_mix_ffn_kernel, "ffn_mix", h.shape[0], (h, attn), (w_out, g, wg, wu, wd))


def _half_rms_scale(x, lo):
    x2 = x * x
    s_lo = jnp.sum(jnp.where(lo, x2, 0.0), axis=-1, keepdims=True)
    s_hi = jnp.sum(jnp.where(lo, 0.0, x2), axis=-1, keepdims=True)
    return jnp.where(lo, lax.rsqrt(s_lo * (1.0 / FOX_DIM) + EPS), lax.rsqrt(s_hi * (1.0 / FOX_DIM) + EPS))


def _proj_kernel(h_ref, gmix_ref, win_ref, gcq_ref, wuq_ref, gckv_ref, wuk_ref, wuvt_ref, wfvt_ref,
                 gq_ref, gqs_ref, gk_ref, gks_ref, cc_ref, ss_ref, bf_ref, gfq_ref, gfk_ref,
                 tri_ref, place_ref, *rest, tm, is_meta):
    if is_meta:
        k_out, vt_out = rest
        q_out = carry_sc = None
    else:
        q_out, k_out, vt_out, cum_end_out, carry_sc = rest

    lane = lax.broadcasted_iota(jnp.int32, (tm, LANES), 1)
    lo = lane < HALF
    bias_e = ((lane >= HALF) & (lane < HALF + N_SPLIT)).astype(F32)
    bias_o = (lane < N_SPLIT).astype(F32)
    ones_row = (lax.broadcasted_iota(jnp.int32, (HALF, tm), 0) == 0).astype(F32)

    def put_vt(head, blk):
        rows = [blk, ones_row] if head % 2 == 0 else [ones_row, blk]
        vt_out[0, head, 0] = jnp.concatenate(rows, axis=0).astype(BF16)

    h = h_ref[0]
    n = (h * _rms_scale(h, D_MODEL) * gmix_ref[...]).astype(BF16)
    proj = jnp.dot(n, win_ref[...], preferred_element_type=F32)

    c_q = proj[:, _O_CQ:_O_CKV]
    c_kv = proj[:, _O_CKV:_O_KPE]
    kpe = proj[:, _O_KPE:_O_KPE_SW]
    kpe_sw = proj[:, _O_KPE_SW:_O_FQ]

    cc = cc_ref[...]
    ss = ss_ref[...]

    ckvn = (c_kv * _rms_scale(c_kv, MLA_KV_RANK) * gckv_ref[...]).astype(BF16)
    kn = jnp.dot(ckvn, wuk_ref[...], preferred_element_type=F32)
    kc = kpe_sw * gks_ref[...] * ss
    for hd in range(MLA_HEADS):
        x = kn[:, hd * LANES:(hd + 1) * LANES] + kpe
        r = _rms_scale(x, MLA_QK)
        k_out[0, hd] = (r * (x * gk_ref[...] * cc + kc)).astype(BF16)
    vt = lax.dot_general(wuvt_ref[...], ckvn, _NT, preferred_element_type=F32)
    for hd in range(MLA_HEADS):
        put_vt(hd, vt[hd * MLA_V:(hd + 1) * MLA_V, :])
    if not is_meta:
        cqn = (c_q * _rms_scale(c_q, MLA_Q_RANK) * gcq_ref[...]).astype(BF16)
        qq = jnp.dot(cqn, wuq_ref[...], preferred_element_type=F32)
        for hd in range(MLA_HEADS):
            x = qq[:, hd * LANES:(hd + 1) * LANES]
            xs = qq[:, (MLA_HEADS + hd) * LANES:(MLA_HEADS + hd + 1) * LANES]
            r = _rms_scale(x, MLA_QK)
            q_out[0, hd] = (r * (x * gq_ref[...] * cc + xs * gqs_ref[...] * ss)).astype(BF16)

    fl = proj[:, _O_FL:_O_FL + LANES] + bf_ref[...]
    log_f = jnp.minimum(fl, 0.0) - jnp.log1p(jnp.exp(-jnp.abs(fl)))
    tri = tri_ref[...]
    pieces = _split3(log_f)
    assert N_SPLIT == 3
    pair = jnp.dot(tri, jnp.concatenate(pieces[:2], axis=1), preferred_element_type=F32)
    cum = pair[:, :LANES] + pair[:, LANES:] + jnp.dot(tri, pieces[2], preferred_element_type=F32)
    if is_meta:
        cum = cum - cum[tm - 1:tm, :]
    else:
        @pl.when(pl.program_id(1) == 0)
        def _():
            carry_sc[...] = jnp.zeros_like(carry_sc)
        cum = cum + carry_sc[0:1, :]
        carry_sc[...] = jnp.broadcast_to(cum[tm - 1:tm, :], carry_sc.shape)
        cum_end_out[0, 0] = carry_sc[...]
    bias = jnp.dot(jnp.concatenate(_split3(cum * (-LOG2E)), axis=1), place_ref[...],
                   preferred_element_type=F32)
    fvt = lax.dot_general(wfvt_ref[...], n, _NT, preferred_element_type=F32)
    for p in range(FOX_HEADS // 2):
        he, ho = MLA_HEADS + 2 * p, MLA_HEADS + 2 * p + 1
        x = proj[:, _O_FK + p * LANES:_O_FK + (p + 1) * LANES]
        y = x * _half_rms_scale(x, lo) * gfk_ref[...]
        pair_bias = bias[:, p * LANES:(p + 1) * LANES]
        k_out[0, he] = jnp.where(lo, y, pair_bias).astype(BF16)
        k_out[0, ho] = jnp.where(lo, pair_bias, y).astype(BF16)
        put_vt(he, fvt[(2 * p) * FOX_DIM:(2 * p + 1) * FOX_DIM, :])
        put_vt(ho, fvt[(2 * p + 1) * FOX_DIM:(2 * p + 2) * FOX_DIM, :])
        if not is_meta:
            x = proj[:, _O_FQ + p * LANES:_O_FQ + (p + 1) * LANES]
            y = x * _half_rms_scale(x, lo) * gfq_ref[...]
            q_out[0, he] = jnp.where(lo, y, bias_e).astype(BF16)
            q_out[0, ho] = jnp.where(lo, bias_o, y).astype(BF16)


def _proj(h, consts, cc, ss, *, is_meta):
    b, s, _ = h.shape
    tm = min(PROJ_TM, s)
    (gmix, win, gcq, wuq, gckv, wuk, wuvt, wfvt, gq, gqs, gk, gks, bf, gfq, gfk, place) = consts
    tri = jnp.tril(jnp.ones((tm, tm), BF16))
    row_in = pl.BlockSpec((1, tm, D_MODEL), lambda bi, i: (bi, i, 0))
    tab = pl.BlockSpec((tm, LANES), lambda bi, i: (i, 0))
    head_out = pl.BlockSpec((1, N_HEADS, tm, LANES), lambda bi, i: (bi, 0, i, 0))
    head_shape = jax.ShapeDtypeStruct((b, N_HEADS, s, LANES), BF16)
    vt_out = pl.BlockSpec((1, N_HEADS, 1, LANES, tm), lambda bi, i: (bi, 0, i, 0, 0))
    vt_shape = jax.ShapeDtypeStruct((b, N_HEADS, s // tm, LANES, tm), BF16)
    vec = _const_spec((1, LANES))
    in_specs = [row_in, _const_spec((1, D_MODEL)), _const_spec(win.shape),
                _const_spec((1, MLA_Q_RANK)), _const_spec(wuq.shape),
                _const_spec((1, MLA_KV_RANK)), _const_spec(wuk.shape),
                _const_spec(wuvt.shape), _const_spec(wfvt.shape),
                vec, vec, vec, vec, tab, tab, vec, vec, vec,
                _const_spec((tm, tm)), _const_spec(place.shape)]
    if is_meta:
        out_specs, out_shape, scratch = [head_out, vt_out], [head_shape, vt_shape], []
    else:
        cum_end = pl.BlockSpec((1, 1, 8, LANES), lambda bi, i: (bi, i, 0, 0))
        out_specs = [head_out, head_out, vt_out, cum_end]
        out_shape = [head_shape, head_shape, vt_shape, jax.ShapeDtypeStruct((b, s // tm, 8, LANES), F32)]
        scratch = [pltpu.VMEM((8, LANES), F32)]
    return pl.pallas_call(
        functools.partial(_proj_kernel, tm=tm, is_meta=is_meta),
        grid=(b, s // tm),
        in_specs=in_specs,
        out_specs=out_specs,
        out_shape=out_shape,
        scratch_shapes=scratch,
        compiler_params=pltpu.CompilerParams(
            dimension_semantics=("arbitrary", "arbitrary"), vmem_limit_bytes=VMEM_LIMIT),
        name="proj_meta" if is_meta else "proj",
    )(h, gmix, win, gcq, wuq, gckv, wuk, wuvt, wfvt, gq, gqs, gk, gks, cc, ss, bf, gfq, gfk, tri, place)


def _attn_kernel(first_chunk_ref, q_ref, k_ref, vt_ref, km_ref, vmt_ref, o_ref, m_sc, acc_sc, *, tq, tk):
    i = pl.program_id(2)
    head0 = 2 * (pl.program_id(0) * pl.num_programs(1) + pl.program_id(1))
    vchunk = vt_ref.shape[-1]

    def values(hh, block, n_keys):
        n = n_keys // vchunk
        return jnp.concatenate([vt_ref[0, hh, block * n + c] for c in range(n)], axis=1)

    def scores(hh, kc, cols=slice(None)):
        return lax.dot_general(kc, q_ref[0, hh, cols, :], _NT, preferred_element_type=F32)

    def update(hh, s, vt, cols=slice(None)):
        m_old = m_sc[hh, :, cols]
        m_new = jnp.maximum(m_old, jnp.max(s, axis=0, keepdims=True))
        p = jnp.exp2(s - m_new).astype(BF16)
        pv = jnp.dot(vt, p, preferred_element_type=F32)
        acc_sc[hh, :, cols] = jnp.exp2(m_old - m_new) * acc_sc[hh, :, cols] + pv
        m_sc[hh, :, cols] = m_new

    m_sc[...] = jnp.full(m_sc.shape, NEG_INF, F32)
    acc_sc[...] = jnp.zeros(acc_sc.shape, F32)

    def chunk_body(heads):
        def body(j, carry):
            off = pl.multiple_of(j * tk, tk)
            s = [scores(hh, k_ref[0, hh, pl.ds(off, tk), :]) for hh in heads]
            for hh, s_hh in zip(heads, s):
                update(hh, s_hh, values(hh, j, tk))
            return carry
        return body

    first = [first_chunk_ref[(head0 + hh) * pl.num_programs(2) + i] for hh in range(2)]
    both = jnp.maximum(first[0], first[1])
    for hh in range(2):
        lax.fori_loop(first[hh], both, chunk_body((hh,)), 0)
    lax.fori_loop(both, i * (tq // tk), chunk_body((0, 1)), 0)

    off = pl.multiple_of(i * tq, tq)
    vt = [values(hh, i, tq) for hh in range(2)]

    def diag_scores(c):
        nk = (c + 1) * DIAG_COLS
        row = lax.broadcasted_iota(jnp.int32, (nk + BLOCK, DIAG_COLS), 0)
        col = lax.broadcasted_iota(jnp.int32, (nk + BLOCK, DIAG_COLS), 1)
        valid = (row <= col + c * DIAG_COLS) | (row >= nk + BLOCK - N_META)
        out = []
        for hh in range(2):
            keys = jnp.concatenate([k_ref[0, hh, pl.ds(off, nk), :], km_ref[0, hh]], axis=0)
            out.append(jnp.where(valid, scores(hh, keys, slice(c * DIAG_COLS, nk)), NEG_INF))
        return out

    n_blocks = tq // DIAG_COLS
    ahead = 2
    pend = {}
    for n in range(n_blocks + ahead):
        if n < n_blocks:
            pend[n] = diag_scores(n)
        if n >= ahead:
            c = n - ahead
            nk = (c + 1) * DIAG_COLS
            s = pend.pop(c)
            for hh in range(2):
                vals = jnp.concatenate([vt[hh][:, :nk], vmt_ref[0, hh, 0]], axis=1)
                update(hh, s[hh], vals, slice(c * DIAG_COLS, nk))

    a_e = acc_sc[0]
    a_o = acc_sc[1]
    top = lax.broadcasted_iota(jnp.int32, (LANES, tq), 0) < HALF
    out_t = jnp.where(top, a_e / a_e[HALF:HALF + 1, :], a_o / a_o[0:1, :])
    o_ref[0] = out_t.T.astype(BF16)


def _first_live_chunk(cum_end, g_q_fox, g_k_fox, s):
    b = cum_end.shape[0]
    per = s // cum_end.shape[1]
    cum = cum_end[:, :, 0, :FOX_HEADS]
    n_q, n_k, ratio = s // ATT_TQ, s // ATT_TK, ATT_TQ // ATT_TK
    chunk_end = cum[:, ATT_TK // per - 1::ATT_TK // per]
    tile_start = jnp.concatenate([jnp.zeros((b, 1, FOX_HEADS), F32),
                                  cum[:, ATT_TQ // per - 1::ATT_TQ // per][:, :n_q - 1]], axis=1)
    decay = LOG2E * (chunk_end[:, None, :, :] - tile_start[:, :, None, :])
    bound = (64.0 * (1.0 + 2.0 ** -9) ** 2 * (LOG2E / math.sqrt(FOX_DIM))
             * jnp.max(jnp.abs(g_q_fox)) * jnp.max(jnp.abs(g_k_fox)))
    dead = decay > 2.0 * bound + 150.0
    before_tile = jnp.arange(n_k)[None, :] < ratio * jnp.arange(n_q)[:, None]
    lead = jnp.cumprod((dead & before_tile[None, :, :, None]).astype(jnp.int32), axis=2)
    first = jnp.sum(lead, axis=2)
    first = jnp.concatenate([jnp.zeros((b, n_q, MLA_HEADS), jnp.int32), first], axis=2)
    return jnp.transpose(first, (0, 2, 1)).reshape(-1)


def _attention(q, k, vt, km, vmt, first_chunk):
    b, _, s, _ = q.shape
    tq, tk = ATT_TQ, ATT_TK
    vchunk = vt.shape[-1]
    assert tq % tk == 0 and s % tq == 0 and tk % vchunk == 0
    pairs = N_HEADS // 2
    grid_spec = pltpu.PrefetchScalarGridSpec(
        num_scalar_prefetch=1,
        grid=(b, pairs, s // tq),
        in_specs=[
            pl.BlockSpec((1, 2, tq, LANES), lambda bi, p, i, fc: (bi, p, i, 0)),
            pl.BlockSpec((1, 2, s, LANES), lambda bi, p, i, fc: (bi, p, 0, 0)),
            pl.BlockSpec((1, 2, s // vchunk, LANES, vchunk), lambda bi, p, i, fc: (bi, p, 0, 0, 0)),
            pl.BlockSpec((1, 2, BLOCK, LANES), lambda bi, p, i, fc: (0, p, 0, 0)),
            pl.BlockSpec((1, 2, 1, LANES, BLOCK), lambda bi, p, i, fc: (0, p, 0, 0, 0)),
        ],
        out_specs=pl.BlockSpec((1, tq, LANES), lambda bi, p, i, fc: (bi, i, p)),
        scratch_shapes=[pltpu.VMEM((2, 1, tq), F32), pltpu.VMEM((2, LANES, tq), F32)],
    )
    return pl.pallas_call(
        functools.partial(_attn_kernel, tq=tq, tk=tk),
        grid_spec=grid_spec,
        out_shape=jax.ShapeDtypeStruct((b, s, pairs * LANES), BF16),
        compiler_params=pltpu.CompilerParams(
            dimension_semantics=("arbitrary", "arbitrary", "arbitrary"),
            vmem_limit_bytes=VMEM_LIMIT),
        name="attention",
    )(first_chunk, q, k, vt, km, vmt)


def _pad_lanes(x, offset=0):
    n = x.shape[-1]
    pad = [(0, 0)] * (x.ndim - 1) + [(offset, LANES - offset - n)]
    return jnp.pad(x, pad)


def _swap_halves(x):
    half = MLA_ROPE // 2
    return jnp.concatenate([x[..., half:], x[..., :half]], axis=-1)


def _rope_lane_gains(g, scale):
    g = g.astype(F32) * scale
    direct = _pad_lanes(g)
    paired = _pad_lanes(_swap_halves(g[MLA_NOPE:]), MLA_NOPE)
    return direct[None], paired[None]


def _rope_tables(pos_hi, pos_lo):
    half = MLA_ROPE // 2
    inv_freq = 1.0 / (ROPE_THETA ** (jnp.arange(half, dtype=F32) / half))
    off = jnp.zeros((MLA_NOPE,), F32)
    freq = jnp.concatenate([off, inv_freq, inv_freq, jnp.zeros((LANES - MLA_QK,), F32)])
    sign = jnp.concatenate([off, -jnp.ones((half,), F32), jnp.ones((half,), F32),
                            jnp.zeros((LANES - MLA_QK,), F32)])
    ang_hi = pos_hi.astype(F32)[:, None] * freq[None, :]
    ang_lo = pos_lo.astype(F32)[:, None] * freq[None, :]
    c_hi, s_hi = jnp.cos(ang_hi)[:, None, :], jnp.sin(ang_hi)[:, None, :]
    c_lo, s_lo = jnp.cos(ang_lo)[None, :, :], jnp.sin(ang_lo)[None, :, :]
    cc = (c_hi * c_lo - s_hi * s_lo).reshape(-1, LANES)
    ss = ((s_hi * c_lo + c_hi * s_lo) * sign).reshape(-1, LANES)
    return cc, ss


def _layout_params(g_mix, w_in, g_cq, w_uq, g_ckv, w_ukv, g_q_mla, g_k_mla, b_forget,
                   g_q_fox, g_k_fox):
    o_kpe = MLA_Q_RANK + MLA_KV_RANK
    o_fox = o_kpe + MLA_ROPE
    o_fv = o_fox + 2 * FOX_HEADS * FOX_DIM
    o_fl = o_fv + FOX_HEADS * FOX_DIM
    w_kpe = w_in[:, o_kpe:o_fox]
    win = jnp.concatenate([
        w_in[:, :o_kpe],
        _pad_lanes(w_kpe, MLA_NOPE),
        _pad_lanes(_swap_halves(w_kpe), MLA_NOPE),
        w_in[:, o_fox:o_fv],
        _pad_lanes(w_in[:, o_fl:]),
    ], axis=1).astype(BF16)
    assert win.shape[1] == _D_IN_PAD
    wfvt = w_in[:, o_fv:o_fl].T.astype(BF16)

    uq = w_uq.reshape(MLA_Q_RANK, MLA_HEADS, MLA_QK)
    uq_direct = _pad_lanes(uq).reshape(MLA_Q_RANK, MLA_HEADS * LANES)
    uq_paired = _pad_lanes(_swap_halves(uq[..., MLA_NOPE:]), MLA_NOPE).reshape(MLA_Q_RANK, MLA_HEADS * LANES)
    wuq = jnp.concatenate([uq_direct, uq_paired], axis=1).astype(BF16)

    ukv = w_ukv.reshape(MLA_KV_RANK, MLA_HEADS, MLA_NOPE + MLA_V)
    wuk = _pad_lanes(ukv[..., :MLA_NOPE]).reshape(MLA_KV_RANK, MLA_HEADS * LANES).astype(BF16)
    wuvt = ukv[..., MLA_NOPE:].reshape(MLA_KV_RANK, MLA_HEADS * MLA_V).T.astype(BF16)

    gq, gqs = _rope_lane_gains(g_q_mla, LOG2E / math.sqrt(MLA_QK))
    gk, gks = _rope_lane_gains(g_k_mla, 1.0)
    gfq = jnp.tile(g_q_fox.astype(F32) * (LOG2E / math.sqrt(FOX_DIM)), 2)[None]
    gfk = jnp.tile(g_k_fox.astype(F32), 2)[None]
    bf = _pad_lanes(b_forget.astype(F32))[None]

    place = np.zeros((N_SPLIT * LANES, FOX_HEADS // 2 * LANES), np.float32)
    for h in range(FOX_HEADS):
        base = HALF if h % 2 == 0 else 0
        for j in range(N_SPLIT):
            place[j * LANES + h, (h // 2) * LANES + base + j] = 1.0
    place = jnp.asarray(place, BF16)

    return (g_mix[None].astype(F32), win, g_cq[None].astype(F32), wuq, g_ckv[None].astype(F32),
            wuk, wuvt, wfvt, gq, gqs, gk, gks, bf, gfq, gfk, place)


def kernel(x, meta_tokens, g_ffn1, w1_gate, w1_up, w1_down, g_mix, w_in, g_cq, w_uq, g_ckv, w_ukv,
           g_q_mla, g_k_mla, b_forget, g_q_fox, g_k_fox, w_out, g_ffn2, w2_gate, w2_up, w2_down):
    b, s, d = x.shape
    depth = g_ffn1.shape[0]
    assert depth == 1 and d == D_MODEL
    l = 0

    ffn1 = (g_ffn1[l][None], w1_gate[l], w1_up[l], w1_down[l])
    ffn2 = (g_ffn2[l][None], w2_gate[l], w2_up[l], w2_down[l])
    consts = _layout_params(g_mix[l], w_in[l], g_cq[l], w_uq[l], g_ckv[l], w_ukv[l], g_q_mla[l],
                            g_k_mla[l], b_forget[l], g_q_fox[l], g_k_fox[l])

    pad = BLOCK - N_META
    h_meta = jnp.concatenate([jnp.zeros((pad, d), F32), meta_tokens.astype(F32)], axis=0)
    tab_meta = _rope_tables(jnp.maximum(jnp.arange(BLOCK) - pad, 0), jnp.zeros((1,), jnp.int32))
    assert s % BLOCK == 0
    tab_tok = _rope_tables(N_META + BLOCK * jnp.arange(s // BLOCK), jnp.arange(BLOCK))

    h1, h_meta = _ffn_with_meta(x.reshape(b * s, d), h_meta, *ffn1)
    km, vmt = _proj(h_meta[None], consts, *tab_meta, is_meta=True)
    q, k, vt, cum_end = _proj(h1.reshape(b, s, d), consts, *tab_tok, is_meta=False)
    attn = _attention(q, k, vt, km, vmt, _first_live_chunk(cum_end, g_q_fox[l], g_k_fox[l], s))
    out = _mix_ffn(h1, attn.reshape(b * s, d), w_out[l], *ffn2)
    return out.reshape(b, s, d)
```

```python
import functools
import math

import jax
import jax.numpy as jnp
import numpy as np
from jax import lax
from jax.experimental import pallas as pl
from jax.experimental.pallas import tpu as pltpu

F32 = jnp.float32
BF16 = jnp.bfloat16

D_MODEL = 1024
D_FF = 2816
N_META = 16
BLOCK = 128
EPS = 1e-6
NEG_INF = -1e30
MLA_HEADS = 8
MLA_Q_RANK = 256
MLA_KV_RANK = 128
MLA_NOPE = 64
MLA_ROPE = 32
MLA_QK = MLA_NOPE + MLA_ROPE
MLA_V = 64
ROPE_THETA = 10000.0
FOX_HEADS = 8
FOX_DIM = 64
N_HEADS = MLA_HEADS + FOX_HEADS
LANES = 128
HALF = LANES // 2
N_SPLIT = 3
LOG2E = math.log2(math.e)

FF_CHUNK = 256
FFN_TM = 512
PROJ_TM = 512
ATT_TQ = 2048
ATT_TK = 1024
DIAG_COLS = 256
VMEM_LIMIT = 56 * 1024 * 1024

_O_CQ = 0
_O_CKV = _O_CQ + MLA_Q_RANK
_O_KPE = _O_CKV + MLA_KV_RANK
_O_KPE_SW = _O_KPE + LANES
_O_FQ = _O_KPE_SW + LANES
_O_FK = _O_FQ + FOX_HEADS * FOX_DIM
_O_FL = _O_FK + FOX_HEADS * FOX_DIM
_D_IN_PAD = _O_FL + LANES

_NT = (((1,), (1,)), ((), ()))


def _const_spec(shape):
    nd = len(shape)
    return pl.BlockSpec(shape, lambda *_: (0,) * nd, pipeline_mode=pl.Buffered(1))


def _rms_scale(x, width):
    return lax.rsqrt(jnp.sum(x * x, axis=-1, keepdims=True) * (1.0 / width) + EPS)


def _split3(x):
    pieces = []
    r = x
    for _ in range(N_SPLIT):
        p = r.astype(BF16)
        pieces.append(p)
        r = r - p.astype(F32)
    return pieces


def _ffn_body(h, g_ref, wg_ref, wu_ref, wd_ref):
    n = (h * _rms_scale(h, D_MODEL) * g_ref[...]).astype(wg_ref.dtype)
    acc = jnp.zeros(h.shape, F32)
    for c in range(D_FF // FF_CHUNK):
        sl = slice(c * FF_CHUNK, (c + 1) * FF_CHUNK)
        g = jnp.dot(n, wg_ref[:, sl], preferred_element_type=F32)
        u = jnp.dot(n, wu_ref[:, sl], preferred_element_type=F32)
        a = (g * (1.0 / (1.0 + jnp.exp(-g))) * u).astype(wd_ref.dtype)
        acc = acc + jnp.dot(a, wd_ref[sl, :], preferred_element_type=F32)
    return h + 0.5 * acc


def _ffn_meta_kernel(h_ref, hm_ref, g_ref, wg_ref, wu_ref, wd_ref, o_ref, om_ref):
    o_ref[...] = _ffn_body(h_ref[...], g_ref, wg_ref, wu_ref, wd_ref)

    @pl.when(pl.program_id(0) == 0)
    def _():
        om_ref[...] = _ffn_body(hm_ref[...], g_ref, wg_ref, wu_ref, wd_ref)


def _mix_ffn_kernel(h_ref, a_ref, wo_ref, g_ref, wg_ref, wu_ref, wd_ref, o_ref):
    h = h_ref[...] + jnp.dot(a_ref[...].astype(wo_ref.dtype), wo_ref[...], preferred_element_type=F32)
    o_ref[...] = _ffn_body(h, g_ref, wg_ref, wu_ref, wd_ref)


def _ffn_call(kern, name, m, row_args, const_args, n_meta_rows=0):
    tm = FFN_TM
    assert m % tm == 0
    row = pl.BlockSpec((tm, D_MODEL), lambda i: (i, 0))
    out_specs, out_shape = row, jax.ShapeDtypeStruct((m, D_MODEL), F32)
    if n_meta_rows:
        out_specs = [row, pl.BlockSpec((n_meta_rows, D_MODEL), lambda i: (0, 0))]
        out_shape = [out_shape, jax.ShapeDtypeStruct((n_meta_rows, D_MODEL), F32)]
    return pl.pallas_call(
        kern,
        grid=(m // tm,),
        in_specs=[row] * len(row_args) + [_const_spec(a.shape) for a in const_args],
        out_specs=out_specs,
        out_shape=out_shape,
        compiler_params=pltpu.CompilerParams(
            dimension_semantics=("arbitrary",), vmem_limit_bytes=VMEM_LIMIT),
        name=name,
    )(*row_args, *const_args)


def _ffn_with_meta(h, h_meta, g, wg, wu, wd):
    return _ffn_call(_ffn_meta_kernel, "ffn", h.shape[0], (h,), (h_meta, g, wg, wu, wd),
                     n_meta_rows=h_meta.shape[0])


def _mix_ffn(h, attn, w_out, g, wg, wu, wd):
    return _ffn_call(_mix_ffn_kernel, "ffn_mix", h.shape[0], (h, attn), (w_out, g, wg, wu, wd))


def _half_rms_scale(x, lo):
    x2 = x * x
    s_lo = jnp.sum(jnp.where(lo, x2, 0.0), axis=-1, keepdims=True)
    s_hi = jnp.sum(jnp.where(lo, 0.0, x2), axis=-1, keepdims=True)
    return jnp.where(lo, lax.rsqrt(s_lo * (1.0 / FOX_DIM) + EPS), lax.rsqrt(s_hi * (1.0 / FOX_DIM) + EPS))


def _proj_kernel(h_ref, gmix_ref, win_ref, gcq_ref, wuq_ref, gckv_ref, wuk_ref, wuvt_ref, wfvt_ref,
                 gq_ref, gqs_ref, gk_ref, gks_ref, cc_ref, ss_ref, bf_ref, gfq_ref, gfk_ref,
                 tri_ref, place_ref, *rest, tm, is_meta):
    if is_meta:
        k_out, vt_out = rest
        q_out = carry_sc = None
    else:
        q_out, k_out, vt_out, cum_end_out, carry_sc = rest

    lane = lax.broadcasted_iota(jnp.int32, (tm, LANES), 1)
    lo = lane < HALF
    bias_e = ((lane >= HALF) & (lane < HALF + N_SPLIT)).astype(F32)
    bias_o = (lane < N_SPLIT).astype(F32)
    ones_row = (lax.broadcasted_iota(jnp.int32, (HALF, tm), 0) == 0).astype(F32)

    def put_vt(head, blk):
        rows = [blk, ones_row] if head % 2 == 0 else [ones_row, blk]
        vt_out[0, head, 0] = jnp.concatenate(rows, axis=0).astype(BF16)

    h = h_ref[0]
    n = (h * _rms_scale(h, D_MODEL) * gmix_ref[...]).astype(BF16)
    proj = jnp.dot(n, win_ref[...], preferred_element_type=F32)

    c_q = proj[:, _O_CQ:_O_CKV]
    c_kv = proj[:, _O_CKV:_O_KPE]
    kpe = proj[:, _O_KPE:_O_KPE_SW]
    kpe_sw = proj[:, _O_KPE_SW:_O_FQ]

    cc = cc_ref[...]
    ss = ss_ref[...]

    ckvn = (c_kv * _rms_scale(c_kv, MLA_KV_RANK) * gckv_ref[...]).astype(BF16)
    kn = jnp.dot(ckvn, wuk_ref[...], preferred_element_type=F32)
    kc = kpe_sw * gks_ref[...] * ss
    for hd in range(MLA_HEADS):
        x = kn[:, hd * LANES:(hd + 1) * LANES] + kpe
        r = _rms_scale(x, MLA_QK)
        k_out[0, hd] = (r * (x * gk_ref[...] * cc + kc)).astype(BF16)
    vt = lax.dot_general(wuvt_ref[...], ckvn, _NT, preferred_element_type=F32)
    for hd in range(MLA_HEADS):
        put_vt(hd, vt[hd * MLA_V:(hd + 1) * MLA_V, :])
    if not is_meta:
        cqn = (c_q * _rms_scale(c_q, MLA_Q_RANK) * gcq_ref[...]).astype(BF16)
        qq = jnp.dot(cqn, wuq_ref[...], preferred_element_type=F32)
        for hd in range(MLA_HEADS):
            x = qq[:, hd * LANES:(hd + 1) * LANES]
            xs = qq[:, (MLA_HEADS + hd) * LANES:(MLA_HEADS + hd + 1) * LANES]
            r = _rms_scale(x, MLA_QK)
            q_out[0, hd] = (r * (x * gq_ref[...] * cc + xs * gqs_ref[...] * ss)).astype(BF16)

    fl = proj[:, _O_FL:_O_FL + LANES] + bf_ref[...]
    log_f = jnp.minimum(fl, 0.0) - jnp.log1p(jnp.exp(-jnp.abs(fl)))
    tri = tri_ref[...]
    pieces = _split3(log_f)
    assert N_SPLIT == 3
    pair = jnp.dot(tri, jnp.concatenate(pieces[:2], axis=1), preferred_element_type=F32)
    cum = pair[:, :LANES] + pair[:, LANES:] + jnp.dot(tri, pieces[2], preferred_element_type=F32)
    if is_meta:
        cum = cum - cum[tm - 1:tm, :]
    else:
        @pl.when(pl.program_id(1) == 0)
        def _():
            carry_sc[...] = jnp.zeros_like(carry_sc)
        cum = cum + carry_sc[0:1, :]
        carry_sc[...] = jnp.broadcast_to(cum[tm - 1:tm, :], carry_sc.shape)
        cum_end_out[0, 0] = carry_sc[...]
    bias = jnp.dot(jnp.concatenate(_split3(cum * (-LOG2E)), axis=1), place_ref[...],
                   preferred_element_type=F32)
    fvt = lax.dot_general(wfvt_ref[...], n, _NT, preferred_element_type=F32)
    for p in range(FOX_HEADS // 2):
        he, ho = MLA_HEADS + 2 * p, MLA_HEADS + 2 * p + 1
        x = proj[:, _O_FK + p * LANES:_O_FK + (p + 1) * LANES]
        y = x * _half_rms_scale(x, lo) * gfk_ref[...]
        pair_bias = bias[:, p * LANES:(p + 1) * LANES]
        k_out[0, he] = jnp.where(lo, y, pair_bias).astype(BF16)
        k_out[0, ho] = jnp.where(lo, pair_bias, y).astype(BF16)
        put_vt(he, fvt[(2 * p) * FOX_DIM:(2 * p + 1) * FOX_DIM, :])
        put_vt(ho, fvt[(2 * p + 1) * FOX_DIM:(2 * p + 2) * FOX_DIM, :])
        if not is_meta:
            x = proj[:, _O_FQ + p * LANES:_O_FQ + (p + 1) * LANES]
            y = x * _half_rms_scale(x, lo) * gfq_ref[...]
            q_out[0, he] = jnp.where(lo, y, bias_e).astype(BF16)
            q_out[0, ho] = jnp.where(lo, bias_o, y).astype(BF16)


def _proj(h, consts, cc, ss, *, is_meta):
    b, s, _ = h.shape
    tm = min(PROJ_TM, s)
    (gmix, win, gcq, wuq, gckv, wuk, wuvt, wfvt, gq, gqs, gk, gks, bf, gfq, gfk, place) = consts
    tri = jnp.tril(jnp.ones((tm, tm), BF16))
    row_in = pl.BlockSpec((1, tm, D_MODEL), lambda bi, i: (bi, i, 0))
    tab = pl.BlockSpec((tm, LANES), lambda bi, i: (i, 0))
    head_out = pl.BlockSpec((1, N_HEADS, tm, LANES), lambda bi, i: (bi, 0, i, 0))
    head_shape = jax.ShapeDtypeStruct((b, N_HEADS, s, LANES), BF16)
    vt_out = pl.BlockSpec((1, N_HEADS, 1, LANES, tm), lambda bi, i: (bi, 0, i, 0, 0))
    vt_shape = jax.ShapeDtypeStruct((b, N_HEADS, s // tm, LANES, tm), BF16)
    vec = _const_spec((1, LANES))
    in_specs = [row_in, _const_spec((1, D_MODEL)), _const_spec(win.shape),
                _const_spec((1, MLA_Q_RANK)), _const_spec(wuq.shape),
                _const_spec((1, MLA_KV_RANK)), _const_spec(wuk.shape),
                _const_spec(wuvt.shape), _const_spec(wfvt.shape),
                vec, vec, vec, vec, tab, tab, vec, vec, vec,
                _const_spec((tm, tm)), _const_spec(place.shape)]
    if is_meta:
        out_specs, out_shape, scratch = [head_out, vt_out], [head_shape, vt_shape], []
    else:
        cum_end = pl.BlockSpec((1, 1, 8, LANES), lambda bi, i: (bi, i, 0, 0))
        out_specs = [head_out, head_out, vt_out, cum_end]
        out_shape = [head_shape, head_shape, vt_shape, jax.ShapeDtypeStruct((b, s // tm, 8, LANES), F32)]
        scratch = [pltpu.VMEM((8, LANES), F32)]
    return pl.pallas_call(
        functools.partial(_proj_kernel, tm=tm, is_meta=is_meta),
        grid=(b, s // tm),
        in_specs=in_specs,
        out_specs=out_specs,
        out_shape=out_shape,
        scratch_shapes=scratch,
        compiler_params=pltpu.CompilerParams(
            dimension_semantics=("arbitrary", "arbitrary"), vmem_limit_bytes=VMEM_LIMIT),
        name="proj_meta" if is_meta else "proj",
    )(h, gmix, win, gcq, wuq, gckv, wuk, wuvt, wfvt, gq, gqs, gk, gks, cc, ss, bf, gfq, gfk, tri, place)


def _attn_kernel(first_chunk_ref, q_ref, k_ref, vt_ref, km_ref, vmt_ref, o_ref, m_sc, acc_sc, *, tq, tk):
    i = pl.program_id(2)
    head0 = 2 * (pl.program_id(0) * pl.num_programs(1) + pl.program_id(1))
    vchunk = vt_ref.shape[-1]

    def values(hh, block, n_keys):
        n = n_keys // vchunk
        return jnp.concatenate([vt_ref[0, hh, block * n + c] for c in range(n)], axis=1)

    def scores(hh, kc, cols=slice(None)):
        return lax.dot_general(kc, q_ref[0, hh, cols, :], _NT, preferred_element_type=F32)

    def update(hh, s, vt, cols=slice(None)):
        m_old = m_sc[hh, :, cols]
        m_new = jnp.maximum(m_old, jnp.max(s, axis=0, keepdims=True))
        p = jnp.exp2(s - m_new).astype(BF16)
        pv = jnp.dot(vt, p, preferred_element_type=F32)
        acc_sc[hh, :, cols] = jnp.exp2(m_old - m_new) * acc_sc[hh, :, cols] + pv
        m_sc[hh, :, cols] = m_new

    m_sc[...] = jnp.full(m_sc.shape, NEG_INF, F32)
    acc_sc[...] = jnp.zeros(acc_sc.shape, F32)

    def chunk_body(heads):
        def body(j, carry):
            off = pl.multiple_of(j * tk, tk)
            s = [scores(hh, k_ref[0, hh, pl.ds(off, tk), :]) for hh in heads]
            for hh, s_hh in zip(heads, s):
                update(hh, s_hh, values(hh, j, tk))
            return carry
        return body

    first = [first_chunk_ref[(head0 + hh) * pl.num_programs(2) + i] for hh in range(2)]
    both = jnp.maximum(first[0], first[1])
    for hh in range(2):
        lax.fori_loop(first[hh], both, chunk_body((hh,)), 0)
    lax.fori_loop(both, i * (tq // tk), chunk_body((0, 1)), 0)

    off = pl.multiple_of(i * tq, tq)
    vt = [values(hh, i, tq) for hh in range(2)]

    def diag_scores(c):
        nk = (c + 1) * DIAG_COLS
        row = lax.broadcasted_iota(jnp.int32, (nk + BLOCK, DIAG_COLS), 0)
        col = lax.broadcasted_iota(jnp.int32, (nk + BLOCK, DIAG_COLS), 1)
        valid = (row <= col + c * DIAG_COLS) | (row >= nk + BLOCK - N_META)
        out = []
        for hh in range(2):
            keys = jnp.concatenate([k_ref[0, hh, pl.ds(off, nk), :], km_ref[0, hh]], axis=0)
            out.append(jnp.where(valid, scores(hh, keys, slice(c * DIAG_COLS, nk)), NEG_INF))
        return out

    n_blocks = tq // DIAG_COLS
    ahead = 2
    pend = {}
    for n in range(n_blocks + ahead):
        if n < n_blocks:
            pend[n] = diag_scores(n)
        if n >= ahead:
            c = n - ahead
            nk = (c + 1) * DIAG_COLS
            s = pend.pop(c)
            for hh in range(2):
                vals = jnp.concatenate([vt[hh][:, :nk], vmt_ref[0, hh, 0]], axis=1)
                update(hh, s[hh], vals, slice(c * DIAG_COLS, nk))

    a_e = acc_sc[0]
    a_o = acc_sc[1]
    top = lax.broadcasted_iota(jnp.int32, (LANES, tq), 0) < HALF
    out_t = jnp.where(top, a_e / a_e[HALF:HALF + 1, :], a_o / a_o[0:1, :])
    o_ref[0] = out_t.T.astype(BF16)


def _first_live_chunk(cum_end, g_q_fox, g_k_fox, s):
    b = cum_end.shape[0]
    per = s // cum_end.shape[1]
    cum = cum_end[:, :, 0, :FOX_HEADS]
    n_q, n_k, ratio = s // ATT_TQ, s // ATT_TK, ATT_TQ // ATT_TK
    chunk_end = cum[:, ATT_TK // per - 1::ATT_TK // per]
    tile_start = jnp.concatenate([jnp.zeros((b, 1, FOX_HEADS), F32),
                                  cum[:, ATT_TQ // per - 1::ATT_TQ // per][:, :n_q - 1]], axis=1)
    decay = LOG2E * (chunk_end[:, None, :, :] - tile_start[:, :, None, :])
    bound = (64.0 * (1.0 + 2.0 ** -9) ** 2 * (LOG2E / math.sqrt(FOX_DIM))
             * jnp.max(jnp.abs(g_q_fox)) * jnp.max(jnp.abs(g_k_fox)))
    dead = decay > 2.0 * bound + 150.0
    before_tile = jnp.arange(n_k)[None, :] < ratio * jnp.arange(n_q)[:, None]
    flags = (dead & before_tile[None, :, :, None]).astype(F32)
    upto = (jnp.arange(n_k)[:, None] <= jnp.arange(n_k)[None, :]).astype(F32)
    lead = jnp.einsum("bqjh,jk->bqkh", flags, upto) == (1.0 + jnp.arange(n_k, dtype=F32))[None, None, :, None]
    first = jnp.sum(lead, axis=2).astype(jnp.int32)
    first = jnp.concatenate([jnp.zeros((b, n_q, MLA_HEADS), jnp.int32), first], axis=2)
    return jnp.transpose(first, (0, 2, 1)).reshape(-1)


def _attention(q, k, vt, km, vmt, first_chunk):
    b, _, s, _ = q.shape
    tq, tk = ATT_TQ, ATT_TK
    vchunk = vt.shape[-1]
    assert tq % tk == 0 and s % tq == 0 and tk % vchunk == 0
    pairs = N_HEADS // 2
    grid_spec = pltpu.PrefetchScalarGridSpec(
        num_scalar_prefetch=1,
        grid=(b, pairs, s // tq),
        in_specs=[
            pl.BlockSpec((1, 2, tq, LANES), lambda bi, p, i, fc: (bi, p, i, 0)),
            pl.BlockSpec((1, 2, s, LANES), lambda bi, p, i, fc: (bi, p, 0, 0)),
            pl.BlockSpec((1, 2, s // vchunk, LANES, vchunk), lambda bi, p, i, fc: (bi, p, 0, 0, 0)),
            pl.BlockSpec((1, 2, BLOCK, LANES), lambda bi, p, i, fc: (0, p, 0, 0)),
            pl.BlockSpec((1, 2, 1, LANES, BLOCK), lambda bi, p, i, fc: (0, p, 0, 0, 0)),
        ],
        out_specs=pl.BlockSpec((1, tq, LANES), lambda bi, p, i, fc: (bi, i, p)),
        scratch_shapes=[pltpu.VMEM((2, 1, tq), F32), pltpu.VMEM((2, LANES, tq), F32)],
    )
    return pl.pallas_call(
        functools.partial(_attn_kernel, tq=tq, tk=tk),
        grid_spec=grid_spec,
        out_shape=jax.ShapeDtypeStruct((b, s, pairs * LANES), BF16),
        compiler_params=pltpu.CompilerParams(
            dimension_semantics=("arbitrary", "arbitrary", "arbitrary"),
            vmem_limit_bytes=VMEM_LIMIT),
        name="attention",
    )(first_chunk, q, k, vt, km, vmt)


def _pad_lanes(x, offset=0):
    n = x.shape[-1]
    pad = [(0, 0)] * (x.ndim - 1) + [(offset, LANES - offset - n)]
    return jnp.pad(x, pad)


def _swap_halves(x):
    half = MLA_ROPE // 2
    return jnp.concatenate([x[..., half:], x[..., :half]], axis=-1)


def _rope_lane_gains(g, scale):
    g = g.astype(F32) * scale
    direct = _pad_lanes(g)
    paired = _pad_lanes(_swap_halves(g[MLA_NOPE:]), MLA_NOPE)
    return direct[None], paired[None]


def _rope_tables(pos_hi, pos_lo):
    half = MLA_ROPE // 2
    inv_freq = 1.0 / (ROPE_THETA ** (jnp.arange(half, dtype=F32) / half))
    off = jnp.zeros((MLA_NOPE,), F32)
    freq = jnp.concatenate([off, inv_freq, inv_freq, jnp.zeros((LANES - MLA_QK,), F32)])
    sign = jnp.concatenate([off, -jnp.ones((half,), F32), jnp.ones((half,), F32),
                            jnp.zeros((LANES - MLA_QK,), F32)])
    ang_hi = pos_hi.astype(F32)[:, None] * freq[None, :]
    ang_lo = pos_lo.astype(F32)[:, None] * freq[None, :]
    c_hi, s_hi = jnp.cos(ang_hi)[:, None, :], jnp.sin(ang_hi)[:, None, :]
    c_lo, s_lo = jnp.cos(ang_lo)[None, :, :], jnp.sin(ang_lo)[None, :, :]
    cc = (c_hi * c_lo - s_hi * s_lo).reshape(-1, LANES)
    ss = ((s_hi * c_lo + c_hi * s_lo) * sign).reshape(-1, LANES)
    return cc, ss


def _layout_params(g_mix, w_in, g_cq, w_uq, g_ckv, w_ukv, g_q_mla, g_k_mla, b_forget,
                   g_q_fox, g_k_fox):
    o_kpe = MLA_Q_RANK + MLA_KV_RANK
    o_fox = o_kpe + MLA_ROPE
    o_fv = o_fox + 2 * FOX_HEADS * FOX_DIM
    o_fl = o_fv + FOX_HEADS * FOX_DIM
    w_kpe = w_in[:, o_kpe:o_fox]
    win = jnp.concatenate([
        w_in[:, :o_kpe],
        _pad_lanes(w_kpe, MLA_NOPE),
        _pad_lanes(_swap_halves(w_kpe), MLA_NOPE),
        w_in[:, o_fox:o_fv],
        _pad_lanes(w_in[:, o_fl:]),
    ], axis=1).astype(BF16)
    assert win.shape[1] == _D_IN_PAD
    wfvt = w_in[:, o_fv:o_fl].T.astype(BF16)

    uq = w_uq.reshape(MLA_Q_RANK, MLA_HEADS, MLA_QK)
    uq_direct = _pad_lanes(uq).reshape(MLA_Q_RANK, MLA_HEADS * LANES)
    uq_paired = _pad_lanes(_swap_halves(uq[..., MLA_NOPE:]), MLA_NOPE).reshape(MLA_Q_RANK, MLA_HEADS * LANES)
    wuq = jnp.concatenate([uq_direct, uq_paired], axis=1).astype(BF16)

    ukv = w_ukv.reshape(MLA_KV_RANK, MLA_HEADS, MLA_NOPE + MLA_V)
    wuk = _pad_lanes(ukv[..., :MLA_NOPE]).reshape(MLA_KV_RANK, MLA_HEADS * LANES).astype(BF16)
    wuvt = ukv[..., MLA_NOPE:].reshape(MLA_KV_RANK, MLA_HEADS * MLA_V).T.astype(BF16)

    gq, gqs = _rope_lane_gains(g_q_mla, LOG2E / math.sqrt(MLA_QK))
    gk, gks = _rope_lane_gains(g_k_mla, 1.0)
    gfq = jnp.tile(g_q_fox.astype(F32) * (LOG2E / math.sqrt(FOX_DIM)), 2)[None]
    gfk = jnp.tile(g_k_fox.astype(F32), 2)[None]
    bf = _pad_lanes(b_forget.astype(F32))[None]

    place = np.zeros((N_SPLIT * LANES, FOX_HEADS // 2 * LANES), np.float32)
    for h in range(FOX_HEADS):
        base = HALF if h % 2 == 0 else 0
        for j in range(N_SPLIT):
            place[j * LANES + h, (h // 2) * LANES + base + j] = 1.0
    place = jnp.asarray(place, BF16)

    return (g_mix[None].astype(F32), win, g_cq[None].astype(F32), wuq, g_ckv[None].astype(F32),
            wuk, wuvt, wfvt, gq, gqs, gk, gks, bf, gfq, gfk, place)


def kernel(x, meta_tokens, g_ffn1, w1_gate, w1_up, w1_down, g_mix, w_in, g_cq, w_uq, g_ckv, w_ukv,
           g_q_mla, g_k_mla, b_forget, g_q_fox, g_k_fox, w_out, g_ffn2, w2_gate, w2_up, w2_down):
    b, s, d = x.shape
    depth = g_ffn1.shape[0]
    assert depth == 1 and d == D_MODEL
    l = 0

    ffn1 = (g_ffn1[l][None], w1_gate[l], w1_up[l], w1_down[l])
    ffn2 = (g_ffn2[l][None], w2_gate[l], w2_up[l], w2_down[l])
    consts = _layout_params(g_mix[l], w_in[l], g_cq[l], w_uq[l], g_ckv[l], w_ukv[l], g_q_mla[l],
                            g_k_mla[l], b_forget[l], g_q_fox[l], g_k_fox[l])

    pad = BLOCK - N_META
    h_meta = jnp.concatenate([jnp.zeros((pad, d), F32), meta_tokens.astype(F32)], axis=0)
    tab_meta = _rope_tables(jnp.maximum(jnp.arange(BLOCK) - pad, 0), jnp.zeros((1,), jnp.int32))
    assert s % BLOCK == 0
    tab_tok = _rope_tables(N_META + BLOCK * jnp.arange(s // BLOCK), jnp.arange(BLOCK))

    h1, h_meta = _ffn_with_meta(x.reshape(b * s, d), h_meta, *ffn1)
    km, vmt = _proj(h_meta[None], consts, *tab_meta, is_meta=True)
    q, k, vt, cum_end = _proj(h1.reshape(b, s, d), consts, *tab_tok, is_meta=False)
    attn = _attention(q, k, vt, km, vmt, _first_live_chunk(cum_end, g_q_fox[l], g_k_fox[l], s))
    out = _mix_ffn(h1, attn.reshape(b * s, d), w_out[l], *ffn2)
    return out.reshape(b, s, d)
```

```python
import functools
import math

import jax
import jax.numpy as jnp
import numpy as np
from jax import lax
from jax.experimental import pallas as pl
from jax.experimental.pallas import tpu as pltpu

F32 = jnp.float32
BF16 = jnp.bfloat16

D_MODEL = 1024
D_FF = 2816
N_META = 16
BLOCK = 128
EPS = 1e-6
NEG_INF = -1e30
MLA_HEADS = 8
MLA_Q_RANK = 256
MLA_KV_RANK = 128
MLA_NOPE = 64
MLA_ROPE = 32
MLA_QK = MLA_NOPE + MLA_ROPE
MLA_V = 64
ROPE_THETA = 10000.0
FOX_HEADS = 8
FOX_DIM = 64
N_HEADS = MLA_HEADS + FOX_HEADS
LANES = 128
HALF = LANES // 2
N_SPLIT = 3
LOG2E = math.log2(math.e)

FF_CHUNK = 256
FFN_TM = 512
PROJ_TM = 512
ATT_TQ = 2048
ATT_TK = 1024
DIAG_COLS = 256
VMEM_LIMIT = 56 * 1024 * 1024

_O_CQ = 0
_O_CKV = _O_CQ + MLA_Q_RANK
_O_KPE = _O_CKV + MLA_KV_RANK
_O_KPE_SW = _O_KPE + LANES
_O_FQ = _O_KPE_SW + LANES
_O_FK = _O_FQ + FOX_HEADS * FOX_DIM
_O_FL = _O_FK + FOX_HEADS * FOX_DIM
_D_IN_PAD = _O_FL + LANES

_NT = (((1,), (1,)), ((), ()))


def _const_spec(shape):
    nd = len(shape)
    return pl.BlockSpec(shape, lambda *_: (0,) * nd, pipeline_mode=pl.Buffered(1))


def _rms_scale(x, width):
    return lax.rsqrt(jnp.sum(x * x, axis=-1, keepdims=True) * (1.0 / width) + EPS)


def _split3(x):
    pieces = []
    r = x
    for _ in range(N_SPLIT):
        p = r.astype(BF16)
        pieces.append(p)
        r = r - p.astype(F32)
    return pieces


def _ffn_body(h, g_ref, wg_ref, wu_ref, wd_ref):
    n = (h * _rms_scale(h, D_MODEL) * g_ref[...]).astype(wg_ref.dtype)
    acc = jnp.zeros(h.shape, F32)
    for c in range(D_FF // FF_CHUNK):
        sl = slice(c * FF_CHUNK, (c + 1) * FF_CHUNK)
        g = jnp.dot(n, wg_ref[:, sl], preferred_element_type=F32)
        u = jnp.dot(n, wu_ref[:, sl], preferred_element_type=F32)
        a = (g * (1.0 / (1.0 + jnp.exp(-g))) * u).astype(wd_ref.dtype)
        acc = acc + jnp.dot(a, wd_ref[sl, :], preferred_element_type=F32)
    return h + 0.5 * acc


def _ffn_meta_kernel(h_ref, hm_ref, g_ref, wg_ref, wu_ref, wd_ref, o_ref, om_ref):
    o_ref[...] = _ffn_body(h_ref[...], g_ref, wg_ref, wu_ref, wd_ref)

    @pl.when(pl.program_id(0) == 0)
    def _():
        om_ref[...] = _ffn_body(hm_ref[...], g_ref, wg_ref, wu_ref, wd_ref)


def _mix_ffn_kernel(h_ref, a_ref, wo_ref, g_ref, wg_ref, wu_ref, wd_ref, o_ref):
    h = h_ref[...] + jnp.dot(a_ref[...].astype(wo_ref.dtype), wo_ref[...], preferred_element_type=F32)
    o_ref[...] = _ffn_body(h, g_ref, wg_ref, wu_ref, wd_ref)


def _ffn_call(kern, name, m, row_args, const_args, n_meta_rows=0):
    tm = FFN_TM
    assert m % tm == 0
    row = pl.BlockSpec((tm, D_MODEL), lambda i: (i, 0))
    out_specs, out_shape = row, jax.ShapeDtypeStruct((m, D_MODEL), F32)
    if n_meta_rows:
        out_specs = [row, pl.BlockSpec((n_meta_rows, D_MODEL), lambda i: (0, 0))]
        out_shape = [out_shape, jax.ShapeDtypeStruct((n_meta_rows, D_MODEL), F32)]
    return pl.pallas_call(
        kern,
        grid=(m // tm,),
        in_specs=[row] * len(row_args) + [_const_spec(a.shape) for a in const_args],
        out_specs=out_specs,
        out_shape=out_shape,
        compiler_params=pltpu.CompilerParams(
            dimension_semantics=("arbitrary",), vmem_limit_bytes=VMEM_LIMIT),
        name=name,
    )(*row_args, *const_args)


def _ffn_with_meta(h, h_meta, g, wg, wu, wd):
    return _ffn_call(_ffn_meta_kernel, "ffn", h.shape[0], (h,), (h_meta, g, wg, wu, wd),
                     n_meta_rows=h_meta.shape[0])


def _mix_ffn(h, attn, w_out, g, wg, wu, wd):
    return _ffn_call(_mix_ffn_kernel, "ffn_mix", h.shape[0], (h, attn), (w_out, g, wg, wu, wd))


def _half_rms_scale(x, lo):
    x2 = x * x
    s_lo = jnp.sum(jnp.where(lo, x2, 0.0), axis=-1, keepdims=True)
    s_hi = jnp.sum(jnp.where(lo, 0.0, x2), axis=-1, keepdims=True)
    return jnp.where(lo, lax.rsqrt(s_lo * (1.0 / FOX_DIM) + EPS), lax.rsqrt(s_hi * (1.0 / FOX_DIM) + EPS))


def _proj_kernel(h_ref, gmix_ref, win_ref, gcq_ref, wuq_ref, gckv_ref, wuk_ref, wuvt_ref, wfvt_ref,
                 gq_ref, gqs_ref, gk_ref, gks_ref, cc_ref, ss_ref, bf_ref, gfq_ref, gfk_ref,
                 tri_ref, place_ref, *rest, tm, is_meta):
    if is_meta:
        k_out, vt_out = rest
        q_out = carry_sc = None
    else:
        q_out, k_out, vt_out, cum_end_out, carry_sc = rest

    lane = lax.broadcasted_iota(jnp.int32, (tm, LANES), 1)
    lo = lane < HALF
    bias_e = ((lane >= HALF) & (lane < HALF + N_SPLIT)).astype(F32)
    bias_o = (lane < N_SPLIT).astype(F32)
    ones_row = (lax.broadcasted_iota(jnp.int32, (HALF, tm), 0) == 0).astype(F32)

    def put_vt(head, blk):
        rows = [blk, ones_row] if head % 2 == 0 else [ones_row, blk]
        vt_out[0, head, 0] = jnp.concatenate(rows, axis=0).astype(BF16)

    h = h_ref[0]
    n = (h * _rms_scale(h, D_MODEL) * gmix_ref[...]).astype(BF16)
    proj = jnp.dot(n, win_ref[...], preferred_element_type=F32)

    c_q = proj[:, _O_CQ:_O_CKV]
    c_kv = proj[:, _O_CKV:_O_KPE]
    kpe = proj[:, _O_KPE:_O_KPE_SW]
    kpe_sw = proj[:, _O_KPE_SW:_O_FQ]

    cc = cc_ref[...]
    ss = ss_ref[...]

    ckvn = (c_kv * _rms_scale(c_kv, MLA_KV_RANK) * gckv_ref[...]).astype(BF16)
    kn = jnp.dot(ckvn, wuk_ref[...], preferred_element_type=F32)
    kc = kpe_sw * gks_ref[...] * ss
    for hd in range(MLA_HEADS):
        x = kn[:, hd * LANES:(hd + 1) * LANES] + kpe
        r = _rms_scale(x, MLA_QK)
        k_out[0, hd] = (r * (x * gk_ref[...] * cc + kc)).astype(BF16)
    vt = lax.dot_general(wuvt_ref[...], ckvn, _NT, preferred_element_type=F32)
    for hd in range(MLA_HEADS):
        put_vt(hd, vt[hd * MLA_V:(hd + 1) * MLA_V, :])
    if not is_meta:
        cqn = (c_q * _rms_scale(c_q, MLA_Q_RANK) * gcq_ref[...]).astype(BF16)
        qq = jnp.dot(cqn, wuq_ref[...], preferred_element_type=F32)
        for hd in range(MLA_HEADS):
            x = qq[:, hd * LANES:(hd + 1) * LANES]
            xs = qq[:, (MLA_HEADS + hd) * LANES:(MLA_HEADS + hd + 1) * LANES]
            r = _rms_scale(x, MLA_QK)
            q_out[0, hd] = (r * (x * gq_ref[...] * cc + xs * gqs_ref[...] * ss)).astype(BF16)

    fl = proj[:, _O_FL:_O_FL + LANES] + bf_ref[...]
    log_f = jnp.minimum(fl, 0.0) - jnp.log1p(jnp.exp(-jnp.abs(fl)))
    tri = tri_ref[...]
    pieces = _split3(log_f)
    assert N_SPLIT == 3
    pair = jnp.dot(tri, jnp.concatenate(pieces[:2], axis=1), preferred_element_type=F32)
    cum = pair[:, :LANES] + pair[:, LANES:] + jnp.dot(tri, pieces[2], preferred_element_type=F32)
    if is_meta:
        cum = cum - cum[tm - 1:tm, :]
    else:
        @pl.when(pl.program_id(1) == 0)
        def _():
            carry_sc[...] = jnp.zeros_like(carry_sc)
        cum = cum + carry_sc[0:1, :]
        carry_sc[...] = jnp.broadcast_to(cum[tm - 1:tm, :], carry_sc.shape)
        cum_end_out[0, 0] = carry_sc[...]
    bias = jnp.dot(jnp.concatenate(_split3(cum * (-LOG2E)), axis=1), place_ref[...],
                   preferred_element_type=F32)
    fvt = lax.dot_general(wfvt_ref[...], n, _NT, preferred_element_type=F32)
    for p in range(FOX_HEADS // 2):
        he, ho = MLA_HEADS + 2 * p, MLA_HEADS + 2 * p + 1
        x = proj[:, _O_FK + p * LANES:_O_FK + (p + 1) * LANES]
        y = x * _half_rms_scale(x, lo) * gfk_ref[...]
        pair_bias = bias[:, p * LANES:(p + 1) * LANES]
        k_out[0, he] = jnp.where(lo, y, pair_bias).astype(BF16)
        k_out[0, ho] = jnp.where(lo, pair_bias, y).astype(BF16)
        put_vt(he, fvt[(2 * p) * FOX_DIM:(2 * p + 1) * FOX_DIM, :])
        put_vt(ho, fvt[(2 * p + 1) * FOX_DIM:(2 * p + 2) * FOX_DIM, :])
        if not is_meta:
            x = proj[:, _O_FQ + p * LANES:_O_FQ + (p + 1) * LANES]
            y = x * _half_rms_scale(x, lo) * gfq_ref[...]
            q_out[0, he] = jnp.where(lo, y, bias_e).astype(BF16)
            q_out[0, ho] = jnp.where(lo, bias_o, y).astype(BF16)


def _proj(h, consts, cc, ss, *, is_meta):
    b, s, _ = h.shape
    tm = min(PROJ_TM, s)
    (gmix, win, gcq, wuq, gckv, wuk, wuvt, wfvt, gq, gqs, gk, gks, bf, gfq, gfk, place) = consts
    tri = jnp.tril(jnp.ones((tm, tm), BF16))
    row_in = pl.BlockSpec((1, tm, D_MODEL), lambda bi, i: (bi, i, 0))
    tab = pl.BlockSpec((tm, LANES), lambda bi, i: (i, 0))
    head_out = pl.BlockSpec((1, N_HEADS, tm, LANES), lambda bi, i: (bi, 0, i, 0))
    head_shape = jax.ShapeDtypeStruct((b, N_HEADS, s, LANES), BF16)
    vt_out = pl.BlockSpec((1, N_HEADS, 1, LANES, tm), lambda bi, i: (bi, 0, i, 0, 0))
    vt_shape = jax.ShapeDtypeStruct((b, N_HEADS, s // tm, LANES, tm), BF16)
    vec = _const_spec((1, LANES))
    in_specs = [row_in, _const_spec((1, D_MODEL)), _const_spec(win.shape),
                _const_spec((1, MLA_Q_RANK)), _const_spec(wuq.shape),
                _const_spec((1, MLA_KV_RANK)), _const_spec(wuk.shape),
                _const_spec(wuvt.shape), _const_spec(wfvt.shape),
                vec, vec, vec, vec, tab, tab, vec, vec, vec,
                _const_spec((tm, tm)), _const_spec(place.shape)]
    if is_meta:
        out_specs, out_shape, scratch = [head_out, vt_out], [head_shape, vt_shape], []
    else:
        cum_end = pl.BlockSpec((1, 1, 8, LANES), lambda bi, i: (bi, i, 0, 0))
        out_specs = [head_out, head_out, vt_out, cum_end]
        out_shape = [head_shape, head_shape, vt_shape, jax.ShapeDtypeStruct((b, s // tm, 8, LANES), F32)]
        scratch = [pltpu.VMEM((8, LANES), F32)]
    return pl.pallas_call(
        functools.partial(_proj_kernel, tm=tm, is_meta=is_meta),
        grid=(b, s // tm),
        in_specs=in_specs,
        out_specs=out_specs,
        out_shape=out_shape,
        scratch_shapes=scratch,
        compiler_params=pltpu.CompilerParams(
            dimension_semantics=("arbitrary", "arbitrary"), vmem_limit_bytes=VMEM_LIMIT),
        name="proj_meta" if is_meta else "proj",
    )(h, gmix, win, gcq, wuq, gckv, wuk, wuvt, wfvt, gq, gqs, gk, gks, cc, ss, bf, gfq, gfk, tri, place)


def _attn_kernel(first_chunk_ref, q_ref, k_ref, vt_ref, km_ref, vmt_ref, o_ref, m_sc, acc_sc, *, tq, tk):
    i = pl.program_id(2)
    head0 = 2 * (pl.program_id(0) * pl.num_programs(1) + pl.program_id(1))
    vchunk = vt_ref.shape[-1]

    def values(hh, block, n_keys):
        n = n_keys // vchunk
        return jnp.concatenate([vt_ref[0, hh, block * n + c] for c in range(n)], axis=1)

    def scores(hh, kc, cols=slice(None)):
        return lax.dot_general(kc, q_ref[0, hh, cols, :], _NT, preferred_element_type=F32)

    def update(hh, s, vt, cols=slice(None)):
        m_old = m_sc[hh, :, cols]
        m_new = jnp.maximum(m_old, jnp.max(s, axis=0, keepdims=True))
        p = jnp.exp2(s - m_new).astype(BF16)
        pv = jnp.dot(vt, p, preferred_element_type=F32)
        acc_sc[hh, :, cols] = jnp.exp2(m_old - m_new) * acc_sc[hh, :, cols] + pv
        m_sc[hh, :, cols] = m_new

    m_sc[...] = jnp.full(m_sc.shape, NEG_INF, F32)
    acc_sc[...] = jnp.zeros(acc_sc.shape, F32)

    def chunk_body(heads):
        def body(j, carry):
            off = pl.multiple_of(j * tk, tk)
            s = [scores(hh, k_ref[0, hh, pl.ds(off, tk), :]) for hh in heads]
            for hh, s_hh in zip(heads, s):
                update(hh, s_hh, values(hh, j, tk))
            return carry
        return body

    first = [first_chunk_ref[(head0 + hh) * pl.num_programs(2) + i] for hh in range(2)]
    both = jnp.maximum(first[0], first[1])
    for hh in range(2):
        lax.fori_loop(first[hh], both, chunk_body((hh,)), 0)
    lax.fori_loop(both, i * (tq // tk), chunk_body((0, 1)), 0)

    off = pl.multiple_of(i * tq, tq)
    vt = [values(hh, i, tq) for hh in range(2)]

    def diag_scores(c):
        nk = (c + 1) * DIAG_COLS
        row = lax.broadcasted_iota(jnp.int32, (nk + N_META, DIAG_COLS), 0)
        col = lax.broadcasted_iota(jnp.int32, (nk + N_META, DIAG_COLS), 1)
        valid = (row <= col + c * DIAG_COLS) | (row >= nk)
        out = []
        for hh in range(2):
            keys = jnp.concatenate([k_ref[0, hh, pl.ds(off, nk), :], km_ref[0, hh, BLOCK - N_META:, :]], axis=0)
            out.append(jnp.where(valid, scores(hh, keys, slice(c * DIAG_COLS, nk)), NEG_INF))
        return out

    n_blocks = tq // DIAG_COLS
    ahead = 2
    pend = {}
    for n in range(n_blocks + ahead):
        if n < n_blocks:
            pend[n] = diag_scores(n)
        if n >= ahead:
            c = n - ahead
            nk = (c + 1) * DIAG_COLS
            s = pend.pop(c)
            for hh in range(2):
                vals = jnp.concatenate([vt[hh][:, :nk], vmt_ref[0, hh, 0][:, BLOCK - N_META:]], axis=1)
                update(hh, s[hh], vals, slice(c * DIAG_COLS, nk))

    a_e = acc_sc[0]
    a_o = acc_sc[1]
    top = lax.broadcasted_iota(jnp.int32, (LANES, tq), 0) < HALF
    out_t = jnp.where(top, a_e / a_e[HALF:HALF + 1, :], a_o / a_o[0:1, :])
    o_ref[0] = out_t.T.astype(BF16)


def _first_live_chunk(cum_end, g_q_fox, g_k_fox, s):
    b = cum_end.shape[0]
    per = s // cum_end.shape[1]
    cum = cum_end[:, :, 0, :FOX_HEADS]
    n_q, n_k, ratio = s // ATT_TQ, s // ATT_TK, ATT_TQ // ATT_TK
    chunk_end = cum[:, ATT_TK // per - 1::ATT_TK // per]
    tile_start = jnp.concatenate([jnp.zeros((b, 1, FOX_HEADS), F32),
                                  cum[:, ATT_TQ // per - 1::ATT_TQ // per][:, :n_q - 1]], axis=1)
    decay = LOG2E * (chunk_end[:, None, :, :] - tile_start[:, :, None, :])
    bound = (64.0 * (1.0 + 2.0 ** -9) ** 2 * (LOG2E / math.sqrt(FOX_DIM))
             * jnp.max(jnp.abs(g_q_fox)) * jnp.max(jnp.abs(g_k_fox)))
    dead = decay > 2.0 * bound + 150.0
    before_tile = jnp.arange(n_k)[None, :] < ratio * jnp.arange(n_q)[:, None]
    flags = (dead & before_tile[None, :, :, None]).astype(F32)
    upto = (jnp.arange(n_k)[:, None] <= jnp.arange(n_k)[None, :]).astype(F32)
    lead = jnp.einsum("bqjh,jk->bqkh", flags, upto) == (1.0 + jnp.arange(n_k, dtype=F32))[None, None, :, None]
    first = jnp.sum(lead, axis=2).astype(jnp.int32)
    first = jnp.concatenate([jnp.zeros((b, n_q, MLA_HEADS), jnp.int32), first], axis=2)
    return jnp.transpose(first, (0, 2, 1)).reshape(-1)


def _attention(q, k, vt, km, vmt, first_chunk):
    b, _, s, _ = q.shape
    tq, tk = ATT_TQ, ATT_TK
    vchunk = vt.shape[-1]
    assert tq % tk == 0 and s % tq == 0 and tk % vchunk == 0
    pairs = N_HEADS // 2
    grid_spec = pltpu.PrefetchScalarGridSpec(
        num_scalar_prefetch=1,
        grid=(b, pairs, s // tq),
        in_specs=[
            pl.BlockSpec((1, 2, tq, LANES), lambda bi, p, i, fc: (bi, p, i, 0)),
            pl.BlockSpec((1, 2, s, LANES), lambda bi, p, i, fc: (bi, p, 0, 0)),
            pl.BlockSpec((1, 2, s // vchunk, LANES, vchunk), lambda bi, p, i, fc: (bi, p, 0, 0, 0)),
            pl.BlockSpec((1, 2, BLOCK, LANES), lambda bi, p, i, fc: (0, p, 0, 0)),
            pl.BlockSpec((1, 2, 1, LANES, BLOCK), lambda bi, p, i, fc: (0, p, 0, 0, 0)),
        ],
        out_specs=pl.BlockSpec((1, tq, LANES), lambda bi, p, i, fc: (bi, i, p)),
        scratch_shapes=[pltpu.VMEM((2, 1, tq), F32), pltpu.VMEM((2, LANES, tq), F32)],
    )
    return pl.pallas_call(
        functools.partial(_attn_kernel, tq=tq, tk=tk),
        grid_spec=grid_spec,
        out_shape=jax.ShapeDtypeStruct((b, s, pairs * LANES), BF16),
        compiler_params=pltpu.CompilerParams(
            dimension_semantics=("arbitrary", "arbitrary", "arbitrary"),
            vmem_limit_bytes=VMEM_LIMIT),
        name="attention",
    )(first_chunk, q, k, vt, km, vmt)


def _pad_lanes(x, offset=0):
    n = x.shape[-1]
    pad = [(0, 0)] * (x.ndim - 1) + [(offset, LANES - offset - n)]
    return jnp.pad(x, pad)


def _swap_halves(x):
    half = MLA_ROPE // 2
    return jnp.concatenate([x[..., half:], x[..., :half]], axis=-1)


def _rope_lane_gains(g, scale):
    g = g.astype(F32) * scale
    direct = _pad_lanes(g)
    paired = _pad_lanes(_swap_halves(g[MLA_NOPE:]), MLA_NOPE)
    return direct[None], paired[None]


def _rope_tables(pos_hi, pos_lo):
    half = MLA_ROPE // 2
    inv_freq = 1.0 / (ROPE_THETA ** (jnp.arange(half, dtype=F32) / half))
    off = jnp.zeros((MLA_NOPE,), F32)
    freq = jnp.concatenate([off, inv_freq, inv_freq, jnp.zeros((LANES - MLA_QK,), F32)])
    sign = jnp.concatenate([off, -jnp.ones((half,), F32), jnp.ones((half,), F32),
                            jnp.zeros((LANES - MLA_QK,), F32)])
    ang_hi = pos_hi.astype(F32)[:, None] * freq[None, :]
    ang_lo = pos_lo.astype(F32)[:, None] * freq[None, :]
    c_hi, s_hi = jnp.cos(ang_hi)[:, None, :], jnp.sin(ang_hi)[:, None, :]
    c_lo, s_lo = jnp.cos(ang_lo)[None, :, :], jnp.sin(ang_lo)[None, :, :]
    cc = (c_hi * c_lo - s_hi * s_lo).reshape(-1, LANES)
    ss = ((s_hi * c_lo + c_hi * s_lo) * sign).reshape(-1, LANES)
    return cc, ss


def _layout_params(g_mix, w_in, g_cq, w_uq, g_ckv, w_ukv, g_q_mla, g_k_mla, b_forget,
                   g_q_fox, g_k_fox):
    o_kpe = MLA_Q_RANK + MLA_KV_RANK
    o_fox = o_kpe + MLA_ROPE
    o_fv = o_fox + 2 * FOX_HEADS * FOX_DIM
    o_fl = o_fv + FOX_HEADS * FOX_DIM
    w_kpe = w_in[:, o_kpe:o_fox]
    win = jnp.concatenate([
        w_in[:, :o_kpe],
        _pad_lanes(w_kpe, MLA_NOPE),
        _pad_lanes(_swap_halves(w_kpe), MLA_NOPE),
        w_in[:, o_fox:o_fv],
        _pad_lanes(w_in[:, o_fl:]),
    ], axis=1).astype(BF16)
    assert win.shape[1] == _D_IN_PAD
    wfvt = w_in[:, o_fv:o_fl].T.astype(BF16)

    uq = w_uq.reshape(MLA_Q_RANK, MLA_HEADS, MLA_QK)
    uq_direct = _pad_lanes(uq).reshape(MLA_Q_RANK, MLA_HEADS * LANES)
    uq_paired = _pad_lanes(_swap_halves(uq[..., MLA_NOPE:]), MLA_NOPE).reshape(MLA_Q_RANK, MLA_HEADS * LANES)
    wuq = jnp.concatenate([uq_direct, uq_paired], axis=1).astype(BF16)

    ukv = w_ukv.reshape(MLA_KV_RANK, MLA_HEADS, MLA_NOPE + MLA_V)
    wuk = _pad_lanes(ukv[..., :MLA_NOPE]).reshape(MLA_KV_RANK, MLA_HEADS * LANES).astype(BF16)
    wuvt = ukv[..., MLA_NOPE:].reshape(MLA_KV_RANK, MLA_HEADS * MLA_V).T.astype(BF16)

    gq, gqs = _rope_lane_gains(g_q_mla, LOG2E / math.sqrt(MLA_QK))
    gk, gks = _rope_lane_gains(g_k_mla, 1.0)
    gfq = jnp.tile(g_q_fox.astype(F32) * (LOG2E / math.sqrt(FOX_DIM)), 2)[None]
    gfk = jnp.tile(g_k_fox.astype(F32), 2)[None]
    bf = _pad_lanes(b_forget.astype(F32))[None]

    place = np.zeros((N_SPLIT * LANES, FOX_HEADS // 2 * LANES), np.float32)
    for h in range(FOX_HEADS):
        base = HALF if h % 2 == 0 else 0
        for j in range(N_SPLIT):
            place[j * LANES + h, (h // 2) * LANES + base + j] = 1.0
    place = jnp.asarray(place, BF16)

    return (g_mix[None].astype(F32), win, g_cq[None].astype(F32), wuq, g_ckv[None].astype(F32),
            wuk, wuvt, wfvt, gq, gqs, gk, gks, bf, gfq, gfk, place)


def kernel(x, meta_tokens, g_ffn1, w1_gate, w1_up, w1_down, g_mix, w_in, g_cq, w_uq, g_ckv, w_ukv,
           g_q_mla, g_k_mla, b_forget, g_q_fox, g_k_fox, w_out, g_ffn2, w2_gate, w2_up, w2_down):
    b, s, d = x.shape
    depth = g_ffn1.shape[0]
    assert depth == 1 and d == D_MODEL
    l = 0

    ffn1 = (g_ffn1[l][None], w1_gate[l], w1_up[l], w1_down[l])
    ffn2 = (g_ffn2[l][None], w2_gate[l], w2_up[l], w2_down[l])
    consts = _layout_params(g_mix[l], w_in[l], g_cq[l], w_uq[l], g_ckv[l], w_ukv[l], g_q_mla[l],
                            g_k_mla[l], b_forget[l], g_q_fox[l], g_k_fox[l])

    pad = BLOCK - N_META
    h_meta = jnp.concatenate([jnp.zeros((pad, d), F32), meta_tokens.astype(F32)], axis=0)
    tab_meta = _rope_tables(jnp.maximum(jnp.arange(BLOCK) - pad, 0), jnp.zeros((1,), jnp.int32))
    assert s % BLOCK == 0
    tab_tok = _rope_tables(N_META + BLOCK * jnp.arange(s // BLOCK), jnp.arange(BLOCK))

    h1, h_meta = _ffn_with_meta(x.reshape(b * s, d), h_meta, *ffn1)
    km, vmt = _proj(h_meta[None], consts, *tab_meta, is_meta=True)
    q, k, vt, cum_end = _proj(h1.reshape(b, s, d), consts, *tab_tok, is_meta=False)
    attn = _attention(q, k, vt, km, vmt, _first_live_chunk(cum_end, g_q_fox[l], g_k_fox[l], s))
    out = _mix_ffn(h1, attn.reshape(b * s, d), w_out[l], *ffn2)
    return out.reshape(b, s, d)
```

```python
import functools
import math

import jax
import jax.numpy as jnp
import numpy as np
from jax import lax
from jax.experimental import pallas as pl
from jax.experimental.pallas import tpu as pltpu

F32 = jnp.float32
BF16 = jnp.bfloat16

D_MODEL = 1024
D_FF = 2816
N_META = 16
BLOCK = 128
EPS = 1e-6
NEG_INF = -1e30
MLA_HEADS = 8
MLA_Q_RANK = 256
MLA_KV_RANK = 128
MLA_NOPE = 64
MLA_ROPE = 32
MLA_QK = MLA_NOPE + MLA_ROPE
MLA_V = 64
ROPE_THETA = 10000.0
FOX_HEADS = 8
FOX_DIM = 64
N_HEADS = MLA_HEADS + FOX_HEADS
LANES = 128
HALF = LANES // 2
N_SPLIT = 3
LOG2E = math.log2(math.e)

FF_CHUNK = 256
FFN_TM = 512
PROJ_TM = 512
ATT_TQ = 2048
ATT_TK = 1024
DIAG_COLS = 256
VMEM_LIMIT = 56 * 1024 * 1024

_O_CQ = 0
_O_CKV = _O_CQ + MLA_Q_RANK
_O_KPE = _O_CKV + MLA_KV_RANK
_O_KPE_SW = _O_KPE + LANES
_O_FQ = _O_KPE_SW + LANES
_O_FK = _O_FQ + FOX_HEADS * FOX_DIM
_O_FL = _O_FK + FOX_HEADS * FOX_DIM
_D_IN_PAD = _O_FL + LANES

_NT = (((1,), (1,)), ((), ()))


def _const_spec(shape):
    nd = len(shape)
    return pl.BlockSpec(shape, lambda *_: (0,) * nd, pipeline_mode=pl.Buffered(1))


def _rms_scale(x, width):
    return lax.rsqrt(jnp.sum(x * x, axis=-1, keepdims=True) * (1.0 / width) + EPS)


def _split3(x):
    pieces = []
    r = x
    for _ in range(N_SPLIT):
        p = r.astype(BF16)
        pieces.append(p)
        r = r - p.astype(F32)
    return pieces


N_FF_CHUNKS = D_FF // FF_CHUNK


def _weight_copies(wg_hbm, wu_hbm, wd_hbm, wg_v, wu_v, wd_v, sem, c):
    cols = pl.ds(c * FF_CHUNK, FF_CHUNK)
    return (pltpu.make_async_copy(wg_hbm.at[:, cols], wg_v.at[c], sem.at[0, c]),
            pltpu.make_async_copy(wu_hbm.at[:, cols], wu_v.at[c], sem.at[1, c]),
            pltpu.make_async_copy(wd_hbm.at[cols, :], wd_v.at[c], sem.at[2, c]))


def _ffn_body(h, g_ref, chunk, before_chunk=None):
    acc = jnp.zeros(h.shape, F32)
    n = None
    for c in range(N_FF_CHUNKS):
        if before_chunk is not None:
            before_chunk(c)
        wg, wu, wd = chunk(c)
        if n is None:
            n = (h * _rms_scale(h, D_MODEL) * g_ref[...]).astype(wg.dtype)
        g = jnp.dot(n, wg[...], preferred_element_type=F32)
        u = jnp.dot(n, wu[...], preferred_element_type=F32)
        a = (g * (1.0 / (1.0 + jnp.exp(-g))) * u).astype(wd.dtype)
        acc = acc + jnp.dot(a, wd[...], preferred_element_type=F32)
    return h + 0.5 * acc


def _ffn_steps(first_tile, later_tile, w_hbm, w_vmem, sem):
    first = pl.program_id(0) == 0

    @pl.when(first)
    def _():
        for c in range(N_FF_CHUNKS):
            for cp in _weight_copies(*w_hbm, *w_vmem, sem, c):
                cp.start()

        def arrive(c):
            for cp in _weight_copies(*w_hbm, *w_vmem, sem, c):
                cp.wait()

        first_tile(arrive)

    @pl.when(jnp.logical_not(first))
    def _():
        later_tile()


def _ffn_meta_kernel(h_ref, hm_ref, g_ref, wg_hbm, wu_hbm, wd_hbm, o_ref, om_ref, wg_v, wu_v, wd_v, sem):
    w = (wg_v, wu_v, wd_v)

    def chunk(c):
        return wg_v.at[c], wu_v.at[c], wd_v.at[c]

    def first_tile(arrive):
        o_ref[...] = _ffn_body(h_ref[...], g_ref, chunk, before_chunk=arrive)
        om_ref[...] = _ffn_body(hm_ref[...], g_ref, chunk)

    def later_tile():
        o_ref[...] = _ffn_body(h_ref[...], g_ref, chunk)

    _ffn_steps(first_tile, later_tile, (wg_hbm, wu_hbm, wd_hbm), w, sem)


def _mix_ffn_kernel(h_ref, a_ref, wo_ref, g_ref, wg_ref, wu_ref, wd_ref, o_ref):
    def chunk(c):
        cols = slice(c * FF_CHUNK, (c + 1) * FF_CHUNK)
        return wg_ref.at[:, cols], wu_ref.at[:, cols], wd_ref.at[cols, :]

    h = h_ref[...] + jnp.dot(a_ref[...].astype(wo_ref.dtype), wo_ref[...], preferred_element_type=F32)
    o_ref[...] = _ffn_body(h, g_ref, chunk)


def _ffn_call(kern, name, m, row_args, const_args, hbm_weights=(), n_meta_rows=0):
    tm = FFN_TM
    assert m % tm == 0
    row = pl.BlockSpec((tm, D_MODEL), lambda i: (i, 0))
    out_specs, out_shape = row, jax.ShapeDtypeStruct((m, D_MODEL), F32)
    if n_meta_rows:
        out_specs = [row, pl.BlockSpec((n_meta_rows, D_MODEL), lambda i: (0, 0))]
        out_shape = [out_shape, jax.ShapeDtypeStruct((n_meta_rows, D_MODEL), F32)]
    scratch = []
    if hbm_weights:
        wdt = hbm_weights[0].dtype
        scratch = [pltpu.VMEM((N_FF_CHUNKS, D_MODEL, FF_CHUNK), wdt),
                   pltpu.VMEM((N_FF_CHUNKS, D_MODEL, FF_CHUNK), wdt),
                   pltpu.VMEM((N_FF_CHUNKS, FF_CHUNK, D_MODEL), wdt),
                   pltpu.SemaphoreType.DMA((len(hbm_weights), N_FF_CHUNKS))]
    return pl.pallas_call(
        kern,
        grid=(m // tm,),
        in_specs=([row] * len(row_args) + [_const_spec(a.shape) for a in const_args]
                  + [pl.BlockSpec(memory_space=pl.ANY)] * len(hbm_weights)),
        out_specs=out_specs,
        out_shape=out_shape,
        scratch_shapes=scratch,
        compiler_params=pltpu.CompilerParams(
            dimension_semantics=("arbitrary",), vmem_limit_bytes=VMEM_LIMIT),
        name=name,
    )(*row_args, *const_args, *hbm_weights)


def _ffn_with_meta(h, h_meta, g, wg, wu, wd):
    return _ffn_call(_ffn_meta_kernel, "ffn", h.shape[0], (h,), (h_meta, g), (wg, wu, wd),
                     n_meta_rows=h_meta.shape[0])


def _mix_ffn(h, attn, w_out, g, wg, wu, wd):
    return _ffn_call(_mix_ffn_kernel, "ffn_mix", h.shape[0], (h, attn), (w_out, g, wg, wu, wd))


def _half_rms_scale(x, lo):
    x2 = x * x
    s_lo = jnp.sum(jnp.where(lo, x2, 0.0), axis=-1, keepdims=True)
    s_hi = jnp.sum(jnp.where(lo, 0.0, x2), axis=-1, keepdims=True)
    return jnp.where(lo, lax.rsqrt(s_lo * (1.0 / FOX_DIM) + EPS), lax.rsqrt(s_hi * (1.0 / FOX_DIM) + EPS))


def _proj_kernel(h_ref, gmix_ref, win_ref, gcq_ref, wuq_ref, gckv_ref, wuk_ref, wuvt_ref, wfvt_ref,
                 gq_ref, gqs_ref, gk_ref, gks_ref, cc_ref, ss_ref, bf_ref, gfq_ref, gfk_ref,
                 tri_ref, place_ref, *rest, tm, is_meta):
    if is_meta:
        k_out, vt_out = rest
        q_out = carry_sc = None
    else:
        q_out, k_out, vt_out, cum_end_out, carry_sc = rest

    lane = lax.broadcasted_iota(jnp.int32, (tm, LANES), 1)
    lo = lane < HALF
    bias_e = ((lane >= HALF) & (lane < HALF + N_SPLIT)).astype(F32)
    bias_o = (lane < N_SPLIT).astype(F32)
    ones_row = (lax.broadcasted_iota(jnp.int32, (HALF, tm), 0) == 0).astype(F32)

    def put_vt(head, blk):
        rows = [blk, ones_row] if head % 2 == 0 else [ones_row, blk]
        vt_out[0, head, 0] = jnp.concatenate(rows, axis=0).astype(BF16)

    h = h_ref[0]
    n = (h * _rms_scale(h, D_MODEL) * gmix_ref[...]).astype(BF16)
    proj = jnp.dot(n, win_ref[...], preferred_element_type=F32)

    c_q = proj[:, _O_CQ:_O_CKV]
    c_kv = proj[:, _O_CKV:_O_KPE]
    kpe = proj[:, _O_KPE:_O_KPE_SW]
    kpe_sw = proj[:, _O_KPE_SW:_O_FQ]

    cc = cc_ref[...]
    ss = ss_ref[...]

    ckvn = (c_kv * _rms_scale(c_kv, MLA_KV_RANK) * gckv_ref[...]).astype(BF16)
    kn = jnp.dot(ckvn, wuk_ref[...], preferred_element_type=F32)
    kc = kpe_sw * gks_ref[...] * ss
    for hd in range(MLA_HEADS):
        x = kn[:, hd * LANES:(hd + 1) * LANES] + kpe
        r = _rms_scale(x, MLA_QK)
        k_out[0, hd] = (r * (x * gk_ref[...] * cc + kc)).astype(BF16)
    vt = lax.dot_general(wuvt_ref[...], ckvn, _NT, preferred_element_type=F32)
    for hd in range(MLA_HEADS):
        put_vt(hd, vt[hd * MLA_V:(hd + 1) * MLA_V, :])
    if not is_meta:
        cqn = (c_q * _rms_scale(c_q, MLA_Q_RANK) * gcq_ref[...]).astype(BF16)
        qq = jnp.dot(cqn, wuq_ref[...], preferred_element_type=F32)
        for hd in range(MLA_HEADS):
            x = qq[:, hd * LANES:(hd + 1) * LANES]
            xs = qq[:, (MLA_HEADS + hd) * LANES:(MLA_HEADS + hd + 1) * LANES]
            r = _rms_scale(x, MLA_QK)
            q_out[0, hd] = (r * (x * gq_ref[...] * cc + xs * gqs_ref[...] * ss)).astype(BF16)

    fl = proj[:, _O_FL:_O_FL + LANES] + bf_ref[...]
    log_f = jnp.minimum(fl, 0.0) - jnp.log1p(jnp.exp(-jnp.abs(fl)))
    tri = tri_ref[...]
    pieces = _split3(log_f)
    assert N_SPLIT == 3
    pair = jnp.dot(tri, jnp.concatenate(pieces[:2], axis=1), preferred_element_type=F32)
    cum = pair[:, :LANES] + pair[:, LANES:] + jnp.dot(tri, pieces[2], preferred_element_type=F32)
    if is_meta:
        cum = cum - cum[tm - 1:tm, :]
    else:
        @pl.when(pl.program_id(1) == 0)
        def _():
            carry_sc[...] = jnp.zeros_like(carry_sc)
        cum = cum + carry_sc[0:1, :]
        carry_sc[...] = jnp.broadcast_to(cum[tm - 1:tm, :], carry_sc.shape)
        cum_end_out[0, 0] = carry_sc[...]
    bias = jnp.dot(jnp.concatenate(_split3(cum * (-LOG2E)), axis=1), place_ref[...],
                   preferred_element_type=F32)
    fvt = lax.dot_general(wfvt_ref[...], n, _NT, preferred_element_type=F32)
    for p in range(FOX_HEADS // 2):
        he, ho = MLA_HEADS + 2 * p, MLA_HEADS + 2 * p + 1
        x = proj[:, _O_FK + p * LANES:_O_FK + (p + 1) * LANES]
        y = x * _half_rms_scale(x, lo) * gfk_ref[...]
        pair_bias = bias[:, p * LANES:(p + 1) * LANES]
        k_out[0, he] = jnp.where(lo, y, pair_bias).astype(BF16)
        k_out[0, ho] = jnp.where(lo, pair_bias, y).astype(BF16)
        put_vt(he, fvt[(2 * p) * FOX_DIM:(2 * p + 1) * FOX_DIM, :])
        put_vt(ho, fvt[(2 * p + 1) * FOX_DIM:(2 * p + 2) * FOX_DIM, :])
        if not is_meta:
            x = proj[:, _O_FQ + p * LANES:_O_FQ + (p + 1) * LANES]
            y = x * _half_rms_scale(x, lo) * gfq_ref[...]
            q_out[0, he] = jnp.where(lo, y, bias_e).astype(BF16)
            q_out[0, ho] = jnp.where(lo, bias_o, y).astype(BF16)


def _proj(h, consts, cc, ss, *, is_meta):
    b, s, _ = h.shape
    tm = min(PROJ_TM, s)
    (gmix, win, gcq, wuq, gckv, wuk, wuvt, wfvt, gq, gqs, gk, gks, bf, gfq, gfk, place) = consts
    tri = jnp.tril(jnp.ones((tm, tm), BF16))
    row_in = pl.BlockSpec((1, tm, D_MODEL), lambda bi, i: (bi, i, 0))
    tab = pl.BlockSpec((tm, LANES), lambda bi, i: (i, 0))
    head_out = pl.BlockSpec((1, N_HEADS, tm, LANES), lambda bi, i: (bi, 0, i, 0))
    head_shape = jax.ShapeDtypeStruct((b, N_HEADS, s, LANES), BF16)
    vt_out = pl.BlockSpec((1, N_HEADS, 1, LANES, tm), lambda bi, i: (bi, 0, i, 0, 0))
    vt_shape = jax.ShapeDtypeStruct((b, N_HEADS, s // tm, LANES, tm), BF16)
    vec = _const_spec((1, LANES))
    in_specs = [row_in, _const_spec((1, D_MODEL)), _const_spec(win.shape),
                _const_spec((1, MLA_Q_RANK)), _const_spec(wuq.shape),
                _const_spec((1, MLA_KV_RANK)), _const_spec(wuk.shape),
                _const_spec(wuvt.shape), _const_spec(wfvt.shape),
                vec, vec, vec, vec, tab, tab, vec, vec, vec,
                _const_spec((tm, tm)), _const_spec(place.shape)]
    if is_meta:
        out_specs, out_shape, scratch = [head_out, vt_out], [head_shape, vt_shape], []
    else:
        cum_end = pl.BlockSpec((1, 1, 8, LANES), lambda bi, i: (bi, i, 0, 0))
        out_specs = [head_out, head_out, vt_out, cum_end]
        out_shape = [head_shape, head_shape, vt_shape, jax.ShapeDtypeStruct((b, s // tm, 8, LANES), F32)]
        scratch = [pltpu.VMEM((8, LANES), F32)]
    return pl.pallas_call(
        functools.partial(_proj_kernel, tm=tm, is_meta=is_meta),
        grid=(b, s // tm),
        in_specs=in_specs,
        out_specs=out_specs,
        out_shape=out_shape,
        scratch_shapes=scratch,
        compiler_params=pltpu.CompilerParams(
            dimension_semantics=("arbitrary", "arbitrary"), vmem_limit_bytes=VMEM_LIMIT),
        name="proj_meta" if is_meta else "proj",
    )(h, gmix, win, gcq, wuq, gckv, wuk, wuvt, wfvt, gq, gqs, gk, gks, cc, ss, bf, gfq, gfk, tri, place)


def _attn_kernel(first_chunk_ref, q_ref, k_ref, vt_ref, km_ref, vmt_ref, o_ref, m_sc, acc_sc, *, tq, tk):
    i = pl.program_id(2)
    head0 = 2 * (pl.program_id(0) * pl.num_programs(1) + pl.program_id(1))
    vchunk = vt_ref.shape[-1]

    def values(hh, block, n_keys):
        n = n_keys // vchunk
        return jnp.concatenate([vt_ref[0, hh, block * n + c] for c in range(n)], axis=1)

    def scores(hh, kc, cols=slice(None)):
        return lax.dot_general(kc, q_ref[0, hh, cols, :], _NT, preferred_element_type=F32)

    def update(hh, s, vt, cols=slice(None)):
        m_old = m_sc[hh, :, cols]
        m_new = jnp.maximum(m_old, jnp.max(s, axis=0, keepdims=True))
        p = jnp.exp2(s - m_new).astype(BF16)
        pv = jnp.dot(vt, p, preferred_element_type=F32)
        acc_sc[hh, :, cols] = jnp.exp2(m_old - m_new) * acc_sc[hh, :, cols] + pv
        m_sc[hh, :, cols] = m_new

    m_sc[...] = jnp.full(m_sc.shape, NEG_INF, F32)
    acc_sc[...] = jnp.zeros(acc_sc.shape, F32)

    def chunk_body(heads):
        def body(j, carry):
            off = pl.multiple_of(j * tk, tk)
            s = [scores(hh, k_ref[0, hh, pl.ds(off, tk), :]) for hh in heads]
            for hh, s_hh in zip(heads, s):
                update(hh, s_hh, values(hh, j, tk))
            return carry
        return body

    first = [first_chunk_ref[(head0 + hh) * pl.num_programs(2) + i] for hh in range(2)]
    both = jnp.maximum(first[0], first[1])
    for hh in range(2):
        lax.fori_loop(first[hh], both, chunk_body((hh,)), 0)
    lax.fori_loop(both, i * (tq // tk), chunk_body((0, 1)), 0)

    off = pl.multiple_of(i * tq, tq)
    vt = [values(hh, i, tq) for hh in range(2)]

    def diag_scores(c):
        nk = (c + 1) * DIAG_COLS
        row = lax.broadcasted_iota(jnp.int32, (nk + BLOCK, DIAG_COLS), 0)
        col = lax.broadcasted_iota(jnp.int32, (nk + BLOCK, DIAG_COLS), 1)
        valid = (row <= col + c * DIAG_COLS) | (row >= nk + BLOCK - N_META)
        out = []
        for hh in range(2):
            keys = jnp.concatenate([k_ref[0, hh, pl.ds(off, nk), :], km_ref[0, hh]], axis=0)
            out.append(jnp.where(valid, scores(hh, keys, slice(c * DIAG_COLS, nk)), NEG_INF))
        return out

    n_blocks = tq // DIAG_COLS
    ahead = 2
    pend = {}
    for n in range(n_blocks + ahead):
        if n < n_blocks:
            pend[n] = diag_scores(n)
        if n >= ahead:
            c = n - ahead
            nk = (c + 1) * DIAG_COLS
            s = pend.pop(c)
            for hh in range(2):
                vals = jnp.concatenate([vt[hh][:, :nk], vmt_ref[0, hh, 0]], axis=1)
                update(hh, s[hh], vals, slice(c * DIAG_COLS, nk))

    a_e = acc_sc[0]
    a_o = acc_sc[1]
    top = lax.broadcasted_iota(jnp.int32, (LANES, tq), 0) < HALF
    out_t = jnp.where(top, a_e / a_e[HALF:HALF + 1, :], a_o / a_o[0:1, :])
    o_ref[0] = out_t.T.astype(BF16)


def _first_live_chunk(cum_end, g_q_fox, g_k_fox, s):
    b = cum_end.shape[0]
    per = s // cum_end.shape[1]
    cum = cum_end[:, :, 0, :FOX_HEADS]
    n_q, n_k, ratio = s // ATT_TQ, s // ATT_TK, ATT_TQ // ATT_TK
    chunk_end = cum[:, ATT_TK // per - 1::ATT_TK // per]
    tile_start = jnp.concatenate([jnp.zeros((b, 1, FOX_HEADS), F32),
                                  cum[:, ATT_TQ // per - 1::ATT_TQ // per][:, :n_q - 1]], axis=1)
    decay = LOG2E * (chunk_end[:, None, :, :] - tile_start[:, :, None, :])
    bound = (64.0 * (1.0 + 2.0 ** -9) ** 2 * (LOG2E / math.sqrt(FOX_DIM))
             * jnp.max(jnp.abs(g_q_fox)) * jnp.max(jnp.abs(g_k_fox)))
    dead = decay > 2.0 * bound + 150.0
    before_tile = jnp.arange(n_k)[None, :] < ratio * jnp.arange(n_q)[:, None]
    flags = (dead & before_tile[None, :, :, None]).astype(F32)
    upto = (jnp.arange(n_k)[:, None] <= jnp.arange(n_k)[None, :]).astype(F32)
    lead = jnp.einsum("bqjh,jk->bqkh", flags, upto) == (1.0 + jnp.arange(n_k, dtype=F32))[None, None, :, None]
    first = jnp.sum(lead, axis=2).astype(jnp.int32)
    first = jnp.concatenate([jnp.zeros((b, n_q, MLA_HEADS), jnp.int32), first], axis=2)
    return jnp.transpose(first, (0, 2, 1)).reshape(-1)


def _attention(q, k, vt, km, vmt, first_chunk):
    b, _, s, _ = q.shape
    tq, tk = ATT_TQ, ATT_TK
    vchunk = vt.shape[-1]
    assert tq % tk == 0 and s % tq == 0 and tk % vchunk == 0
    pairs = N_HEADS // 2
    grid_spec = pltpu.PrefetchScalarGridSpec(
        num_scalar_prefetch=1,
        grid=(b, pairs, s // tq),
        in_specs=[
            pl.BlockSpec((1, 2, tq, LANES), lambda bi, p, i, fc: (bi, p, i, 0)),
            pl.BlockSpec((1, 2, s, LANES), lambda bi, p, i, fc: (bi, p, 0, 0)),
            pl.BlockSpec((1, 2, s // vchunk, LANES, vchunk), lambda bi, p, i, fc: (bi, p, 0, 0, 0)),
            pl.BlockSpec((1, 2, BLOCK, LANES), lambda bi, p, i, fc: (0, p, 0, 0)),
            pl.BlockSpec((1, 2, 1, LANES, BLOCK), lambda bi, p, i, fc: (0, p, 0, 0, 0)),
        ],
        out_specs=pl.BlockSpec((1, tq, LANES), lambda bi, p, i, fc: (bi, i, p)),
        scratch_shapes=[pltpu.VMEM((2, 1, tq), F32), pltpu.VMEM((2, LANES, tq), F32)],
    )
    return pl.pallas_call(
        functools.partial(_attn_kernel, tq=tq, tk=tk),
        grid_spec=grid_spec,
        out_shape=jax.ShapeDtypeStruct((b, s, pairs * LANES), BF16),
        compiler_params=pltpu.CompilerParams(
            dimension_semantics=("arbitrary", "arbitrary", "arbitrary"),
            vmem_limit_bytes=VMEM_LIMIT),
        name="attention",
    )(first_chunk, q, k, vt, km, vmt)


def _pad_lanes(x, offset=0):
    n = x.shape[-1]
    pad = [(0, 0)] * (x.ndim - 1) + [(offset, LANES - offset - n)]
    return jnp.pad(x, pad)


def _swap_halves(x):
    half = MLA_ROPE // 2
    return jnp.concatenate([x[..., half:], x[..., :half]], axis=-1)


def _rope_lane_gains(g, scale):
    g = g.astype(F32) * scale
    direct = _pad_lanes(g)
    paired = _pad_lanes(_swap_halves(g[MLA_NOPE:]), MLA_NOPE)
    return direct[None], paired[None]


def _rope_tables(pos_hi, pos_lo):
    half = MLA_ROPE // 2
    inv_freq = 1.0 / (ROPE_THETA ** (jnp.arange(half, dtype=F32) / half))
    off = jnp.zeros((MLA_NOPE,), F32)
    freq = jnp.concatenate([off, inv_freq, inv_freq, jnp.zeros((LANES - MLA_QK,), F32)])
    sign = jnp.concatenate([off, -jnp.ones((half,), F32), jnp.ones((half,), F32),
                            jnp.zeros((LANES - MLA_QK,), F32)])
    ang_hi = pos_hi.astype(F32)[:, None] * freq[None, :]
    ang_lo = pos_lo.astype(F32)[:, None] * freq[None, :]
    c_hi, s_hi = jnp.cos(ang_hi)[:, None, :], jnp.sin(ang_hi)[:, None, :]
    c_lo, s_lo = jnp.cos(ang_lo)[None, :, :], jnp.sin(ang_lo)[None, :, :]
    cc = (c_hi * c_lo - s_hi * s_lo).reshape(-1, LANES)
    ss = ((s_hi * c_lo + c_hi * s_lo) * sign).reshape(-1, LANES)
    return cc, ss


def _layout_params(g_mix, w_in, g_cq, w_uq, g_ckv, w_ukv, g_q_mla, g_k_mla, b_forget,
                   g_q_fox, g_k_fox):
    o_kpe = MLA_Q_RANK + MLA_KV_RANK
    o_fox = o_kpe + MLA_ROPE
    o_fv = o_fox + 2 * FOX_HEADS * FOX_DIM
    o_fl = o_fv + FOX_HEADS * FOX_DIM
    w_kpe = w_in[:, o_kpe:o_fox]
    win = jnp.concatenate([
        w_in[:, :o_kpe],
        _pad_lanes(w_kpe, MLA_NOPE),
        _pad_lanes(_swap_halves(w_kpe), MLA_NOPE),
        w_in[:, o_fox:o_fv],
        _pad_lanes(w_in[:, o_fl:]),
    ], axis=1).astype(BF16)
    assert win.shape[1] == _D_IN_PAD
    wfvt = w_in[:, o_fv:o_fl].T.astype(BF16)

    uq = w_uq.reshape(MLA_Q_RANK, MLA_HEADS, MLA_QK)
    uq_direct = _pad_lanes(uq).reshape(MLA_Q_RANK, MLA_HEADS * LANES)
    uq_paired = _pad_lanes(_swap_halves(uq[..., MLA_NOPE:]), MLA_NOPE).reshape(MLA_Q_RANK, MLA_HEADS * LANES)
    wuq = jnp.concatenate([uq_direct, uq_paired], axis=1).astype(BF16)

    ukv = w_ukv.reshape(MLA_KV_RANK, MLA_HEADS, MLA_NOPE + MLA_V)
    wuk = _pad_lanes(ukv[..., :MLA_NOPE]).reshape(MLA_KV_RANK, MLA_HEADS * LANES).astype(BF16)
    wuvt = ukv[..., MLA_NOPE:].reshape(MLA_KV_RANK, MLA_HEADS * MLA_V).T.astype(BF16)

    gq, gqs = _rope_lane_gains(g_q_mla, LOG2E / math.sqrt(MLA_QK))
    gk, gks = _rope_lane_gains(g_k_mla, 1.0)
    gfq = jnp.tile(g_q_fox.astype(F32) * (LOG2E / math.sqrt(FOX_DIM)), 2)[None]
    gfk = jnp.tile(g_k_fox.astype(F32), 2)[None]
    bf = _pad_lanes(b_forget.astype(F32))[None]

    place = np.zeros((N_SPLIT * LANES, FOX_HEADS // 2 * LANES), np.float32)
    for h in range(FOX_HEADS):
        base = HALF if h % 2 == 0 else 0
        for j in range(N_SPLIT):
            place[j * LANES + h, (h // 2) * LANES + base + j] = 1.0
    place = jnp.asarray(place, BF16)

    return (g_mix[None].astype(F32), win, g_cq[None].astype(F32), wuq, g_ckv[None].astype(F32),
            wuk, wuvt, wfvt, gq, gqs, gk, gks, bf, gfq, gfk, place)


def kernel(x, meta_tokens, g_ffn1, w1_gate, w1_up, w1_down, g_mix, w_in, g_cq, w_uq, g_ckv, w_ukv,
           g_q_mla, g_k_mla, b_forget, g_q_fox, g_k_fox, w_out, g_ffn2, w2_gate, w2_up, w2_down):
    b, s, d = x.shape
    depth = g_ffn1.shape[0]
    assert depth == 1 and d == D_MODEL
    l = 0

    ffn1 = (g_ffn1[l][None], w1_gate[l], w1_up[l], w1_down[l])
    ffn2 = (g_ffn2[l][None], w2_gate[l], w2_up[l], w2_down[l])
    consts = _layout_params(g_mix[l], w_in[l], g_cq[l], w_uq[l], g_ckv[l], w_ukv[l], g_q_mla[l],
                            g_k_mla[l], b_forget[l], g_q_fox[l], g_k_fox[l])

    pad = BLOCK - N_META
    h_meta = jnp.concatenate([jnp.zeros((pad, d), F32), meta_tokens.astype(F32)], axis=0)
    tab_meta = _rope_tables(jnp.maximum(jnp.arange(BLOCK) - pad, 0), jnp.zeros((1,), jnp.int32))
    assert s % BLOCK == 0
    tab_tok = _rope_tables(N_META + BLOCK * jnp.arange(s // BLOCK), jnp.arange(BLOCK))

    h1, h_meta = _ffn_with_meta(x.reshape(b * s, d), h_meta, *ffn1)
    km, vmt = _proj(h_meta[None], consts, *tab_meta, is_meta=True)
    q, k, vt, cum_end = _proj(h1.reshape(b, s, d), consts, *tab_tok, is_meta=False)
    attn = _attention(q, k, vt, km, vmt, _first_live_chunk(cum_end, g_q_fox[l], g_k_fox[l], s))
    out = _mix_ffn(h1, attn.reshape(b * s, d), w_out[l], *ffn2)
    return out.reshape(b, s, d)
```

```python
import functools
import math

import jax
import jax.numpy as jnp
import numpy as np
from jax import lax
from jax.experimental import pallas as pl
from jax.experimental.pallas import tpu as pltpu

F32 = jnp.float32
BF16 = jnp.bfloat16

D_MODEL = 1024
D_FF = 2816
N_META = 16
BLOCK = 128
EPS = 1e-6
NEG_INF = -1e30
MLA_HEADS = 8
MLA_Q_RANK = 256
MLA_KV_RANK = 128
MLA_NOPE = 64
MLA_ROPE = 32
MLA_QK = MLA_NOPE + MLA_ROPE
MLA_V = 64
ROPE_THETA = 10000.0
FOX_HEADS = 8
FOX_DIM = 64
N_HEADS = MLA_HEADS + FOX_HEADS
LANES = 128
HALF = LANES // 2
N_SPLIT = 3
LOG2E = math.log2(math.e)

FF_CHUNK = 256
FFN_TM = 512
PROJ_TM = 512
ATT_TQ = 2048
ATT_TK = 1024
DIAG_COLS = 256
VMEM_LIMIT = 56 * 1024 * 1024
MIX_VMEM_LIMIT = 62 * 1024 * 1024

_O_CQ = 0
_O_CKV = _O_CQ + MLA_Q_RANK
_O_KPE = _O_CKV + MLA_KV_RANK
_O_KPE_SW = _O_KPE + LANES
_O_FQ = _O_KPE_SW + LANES
_O_FK = _O_FQ + FOX_HEADS * FOX_DIM
_O_FL = _O_FK + FOX_HEADS * FOX_DIM
_D_IN_PAD = _O_FL + LANES

_NT = (((1,), (1,)), ((), ()))


def _const_spec(shape):
    nd = len(shape)
    return pl.BlockSpec(shape, lambda *_: (0,) * nd, pipeline_mode=pl.Buffered(1))


def _rms_scale(x, width):
    return lax.rsqrt(jnp.sum(x * x, axis=-1, keepdims=True) * (1.0 / width) + EPS)


def _split3(x):
    pieces = []
    r = x
    for _ in range(N_SPLIT):
        p = r.astype(BF16)
        pieces.append(p)
        r = r - p.astype(F32)
    return pieces


N_FF_CHUNKS = D_FF // FF_CHUNK


def _weight_copies(wg_hbm, wu_hbm, wd_hbm, wg_v, wu_v, wd_v, sem, c):
    cols = pl.ds(c * FF_CHUNK, FF_CHUNK)
    return (pltpu.make_async_copy(wg_hbm.at[:, cols], wg_v.at[c], sem.at[0, c]),
            pltpu.make_async_copy(wu_hbm.at[:, cols], wu_v.at[c], sem.at[1, c]),
            pltpu.make_async_copy(wd_hbm.at[cols, :], wd_v.at[c], sem.at[2, c]))


def _ffn_body(h, g_ref, chunk, before_chunk=None):
    acc = jnp.zeros(h.shape, F32)
    n = None
    for c in range(N_FF_CHUNKS):
        if before_chunk is not None:
            before_chunk(c)
        wg, wu, wd = chunk(c)
        if n is None:
            n = (h * _rms_scale(h, D_MODEL) * g_ref[...]).astype(wg.dtype)
        g = jnp.dot(n, wg[...], preferred_element_type=F32)
        u = jnp.dot(n, wu[...], preferred_element_type=F32)
        a = (g * (1.0 / (1.0 + jnp.exp(-g))) * u).astype(wd.dtype)
        acc = acc + jnp.dot(a, wd[...], preferred_element_type=F32)
    return h + 0.5 * acc


def _ffn_steps(first_tile, later_tile, w_hbm, w_vmem, sem):
    first = pl.program_id(0) == 0

    @pl.when(first)
    def _():
        for c in range(N_FF_CHUNKS):
            for cp in _weight_copies(*w_hbm, *w_vmem, sem, c):
                cp.start()

        def arrive(c):
            for cp in _weight_copies(*w_hbm, *w_vmem, sem, c):
                cp.wait()

        first_tile(arrive)

    @pl.when(jnp.logical_not(first))
    def _():
        later_tile()


def _ffn_meta_kernel(h_ref, hm_ref, g_ref, wg_hbm, wu_hbm, wd_hbm, o_ref, om_ref, wg_v, wu_v, wd_v, sem):
    w = (wg_v, wu_v, wd_v)

    def chunk(c):
        return wg_v.at[c], wu_v.at[c], wd_v.at[c]

    def first_tile(arrive):
        o_ref[...] = _ffn_body(h_ref[...], g_ref, chunk, before_chunk=arrive)
        om_ref[...] = _ffn_body(hm_ref[...], g_ref, chunk)

    def later_tile():
        o_ref[...] = _ffn_body(h_ref[...], g_ref, chunk)

    _ffn_steps(first_tile, later_tile, (wg_hbm, wu_hbm, wd_hbm), w, sem)


def _mix_ffn_kernel(h_ref, a_ref, wo_ref, g_ref, wg_hbm, wu_hbm, wd_hbm, o_ref, wg_v, wu_v, wd_v, sem):
    def chunk(c):
        return wg_v.at[c], wu_v.at[c], wd_v.at[c]

    def tile(arrive=None):
        h = h_ref[...] + jnp.dot(a_ref[...].astype(wo_ref.dtype), wo_ref[...], preferred_element_type=F32)
        o_ref[...] = _ffn_body(h, g_ref, chunk, before_chunk=arrive)

    _ffn_steps(tile, tile, (wg_hbm, wu_hbm, wd_hbm), (wg_v, wu_v, wd_v), sem)


def _ffn_call(kern, name, m, row_args, const_args, hbm_weights=(), n_meta_rows=0, vmem_limit=VMEM_LIMIT):
    tm = FFN_TM
    assert m % tm == 0
    row = pl.BlockSpec((tm, D_MODEL), lambda i: (i, 0))
    out_specs, out_shape = row, jax.ShapeDtypeStruct((m, D_MODEL), F32)
    if n_meta_rows:
        out_specs = [row, pl.BlockSpec((n_meta_rows, D_MODEL), lambda i: (0, 0))]
        out_shape = [out_shape, jax.ShapeDtypeStruct((n_meta_rows, D_MODEL), F32)]
    scratch = []
    if hbm_weights:
        wdt = hbm_weights[0].dtype
        scratch = [pltpu.VMEM((N_FF_CHUNKS, D_MODEL, FF_CHUNK), wdt),
                   pltpu.VMEM((N_FF_CHUNKS, D_MODEL, FF_CHUNK), wdt),
                   pltpu.VMEM((N_FF_CHUNKS, FF_CHUNK, D_MODEL), wdt),
                   pltpu.SemaphoreType.DMA((len(hbm_weights), N_FF_CHUNKS))]
    return pl.pallas_call(
        kern,
        grid=(m // tm,),
        in_specs=([row] * len(row_args) + [_const_spec(a.shape) for a in const_args]
                  + [pl.BlockSpec(memory_space=pl.ANY)] * len(hbm_weights)),
        out_specs=out_specs,
        out_shape=out_shape,
        scratch_shapes=scratch,
        compiler_params=pltpu.CompilerParams(
            dimension_semantics=("arbitrary",), vmem_limit_bytes=vmem_limit),
        name=name,
    )(*row_args, *const_args, *hbm_weights)


def _ffn_with_meta(h, h_meta, g, wg, wu, wd):
    return _ffn_call(_ffn_meta_kernel, "ffn", h.shape[0], (h,), (h_meta, g), (wg, wu, wd),
                     n_meta_rows=h_meta.shape[0])


def _mix_ffn(h, attn, w_out, g, wg, wu, wd):
    return _ffn_call(_mix_ffn_kernel, "ffn_mix", h.shape[0], (h, attn), (w_out, g), (wg, wu, wd),
                     vmem_limit=MIX_VMEM_LIMIT)


def _half_rms_scale(x, lo):
    x2 = x * x
    s_lo = jnp.sum(jnp.where(lo, x2, 0.0), axis=-1, keepdims=True)
    s_hi = jnp.sum(jnp.where(lo, 0.0, x2), axis=-1, keepdims=True)
    return jnp.where(lo, lax.rsqrt(s_lo * (1.0 / FOX_DIM) + EPS), lax.rsqrt(s_hi * (1.0 / FOX_DIM) + EPS))


def _proj_kernel(h_ref, gmix_ref, win_ref, gcq_ref, wuq_ref, gckv_ref, wuk_ref, wuvt_ref, wfvt_ref,
                 gq_ref, gqs_ref, gk_ref, gks_ref, cc_ref, ss_ref, bf_ref, gfq_ref, gfk_ref,
                 tri_ref, place_ref, *rest, tm, is_meta):
    if is_meta:
        k_out, vt_out = rest
        q_out = carry_sc = None
    else:
        q_out, k_out, vt_out, cum_end_out, carry_sc = rest

    lane = lax.broadcasted_iota(jnp.int32, (tm, LANES), 1)
    lo = lane < HALF
    bias_e = ((lane >= HALF) & (lane < HALF + N_SPLIT)).astype(F32)
    bias_o = (lane < N_SPLIT).astype(F32)
    ones_row = (lax.broadcasted_iota(jnp.int32, (HALF, tm), 0) == 0).astype(F32)

    def put_vt(head, blk):
        rows = [blk, ones_row] if head % 2 == 0 else [ones_row, blk]
        vt_out[0, head, 0] = jnp.concatenate(rows, axis=0).astype(BF16)

    h = h_ref[0]
    n = (h * _rms_scale(h, D_MODEL) * gmix_ref[...]).astype(BF16)
    proj = jnp.dot(n, win_ref[...], preferred_element_type=F32)

    c_q = proj[:, _O_CQ:_O_CKV]
    c_kv = proj[:, _O_CKV:_O_KPE]
    kpe = proj[:, _O_KPE:_O_KPE_SW]
    kpe_sw = proj[:, _O_KPE_SW:_O_FQ]

    cc = cc_ref[...]
    ss = ss_ref[...]

    ckvn = (c_kv * _rms_scale(c_kv, MLA_KV_RANK) * gckv_ref[...]).astype(BF16)
    kn = jnp.dot(ckvn, wuk_ref[...], preferred_element_type=F32)
    kc = kpe_sw * gks_ref[...] * ss
    for hd in range(MLA_HEADS):
        x = kn[:, hd * LANES:(hd + 1) * LANES] + kpe
        r = _rms_scale(x, MLA_QK)
        k_out[0, hd] = (r * (x * gk_ref[...] * cc + kc)).astype(BF16)
    vt = lax.dot_general(wuvt_ref[...], ckvn, _NT, preferred_element_type=F32)
    for hd in range(MLA_HEADS):
        put_vt(hd, vt[hd * MLA_V:(hd + 1) * MLA_V, :])
    if not is_meta:
        cqn = (c_q * _rms_scale(c_q, MLA_Q_RANK) * gcq_ref[...]).astype(BF16)
        qq = jnp.dot(cqn, wuq_ref[...], preferred_element_type=F32)
        for hd in range(MLA_HEADS):
            x = qq[:, hd * LANES:(hd + 1) * LANES]
            xs = qq[:, (MLA_HEADS + hd) * LANES:(MLA_HEADS + hd + 1) * LANES]
            r = _rms_scale(x, MLA_QK)
            q_out[0, hd] = (r * (x * gq_ref[...] * cc + xs * gqs_ref[...] * ss)).astype(BF16)

    fl = proj[:, _O_FL:_O_FL + LANES] + bf_ref[...]
    log_f = jnp.minimum(fl, 0.0) - jnp.log1p(jnp.exp(-jnp.abs(fl)))
    tri = tri_ref[...]
    pieces = _split3(log_f)
    assert N_SPLIT == 3
    pair = jnp.dot(tri, jnp.concatenate(pieces[:2], axis=1), preferred_element_type=F32)
    cum = pair[:, :LANES] + pair[:, LANES:] + jnp.dot(tri, pieces[2], preferred_element_type=F32)
    if is_meta:
        cum = cum - cum[tm - 1:tm, :]
    else:
        @pl.when(pl.program_id(1) == 0)
        def _():
            carry_sc[...] = jnp.zeros_like(carry_sc)
        cum = cum + carry_sc[0:1, :]
        carry_sc[...] = jnp.broadcast_to(cum[tm - 1:tm, :], carry_sc.shape)
        cum_end_out[0, 0] = carry_sc[...]
    bias = jnp.dot(jnp.concatenate(_split3(cum * (-LOG2E)), axis=1), place_ref[...],
                   preferred_element_type=F32)
    fvt = lax.dot_general(wfvt_ref[...], n, _NT, preferred_element_type=F32)
    for p in range(FOX_HEADS // 2):
        he, ho = MLA_HEADS + 2 * p, MLA_HEADS + 2 * p + 1
        x = proj[:, _O_FK + p * LANES:_O_FK + (p + 1) * LANES]
        y = x * _half_rms_scale(x, lo) * gfk_ref[...]
        pair_bias = bias[:, p * LANES:(p + 1) * LANES]
        k_out[0, he] = jnp.where(lo, y, pair_bias).astype(BF16)
        k_out[0, ho] = jnp.where(lo, pair_bias, y).astype(BF16)
        put_vt(he, fvt[(2 * p) * FOX_DIM:(2 * p + 1) * FOX_DIM, :])
        put_vt(ho, fvt[(2 * p + 1) * FOX_DIM:(2 * p + 2) * FOX_DIM, :])
        if not is_meta:
            x = proj[:, _O_FQ + p * LANES:_O_FQ + (p + 1) * LANES]
            y = x * _half_rms_scale(x, lo) * gfq_ref[...]
            q_out[0, he] = jnp.where(lo, y, bias_e).astype(BF16)
            q_out[0, ho] = jnp.where(lo, bias_o, y).astype(BF16)


def _proj(h, consts, cc, ss, *, is_meta):
    b, s, _ = h.shape
    tm = min(PROJ_TM, s)
    (gmix, win, gcq, wuq, gckv, wuk, wuvt, wfvt, gq, gqs, gk, gks, bf, gfq, gfk, place) = consts
    tri = jnp.tril(jnp.ones((tm, tm), BF16))
    row_in = pl.BlockSpec((1, tm, D_MODEL), lambda bi, i: (bi, i, 0))
    tab = pl.BlockSpec((tm, LANES), lambda bi, i: (i, 0))
    head_out = pl.BlockSpec((1, N_HEADS, tm, LANES), lambda bi, i: (bi, 0, i, 0))
    head_shape = jax.ShapeDtypeStruct((b, N_HEADS, s, LANES), BF16)
    vt_out = pl.BlockSpec((1, N_HEADS, 1, LANES, tm), lambda bi, i: (bi, 0, i, 0, 0))
    vt_shape = jax.ShapeDtypeStruct((b, N_HEADS, s // tm, LANES, tm), BF16)
    vec = _const_spec((1, LANES))
    in_specs = [row_in, _const_spec((1, D_MODEL)), _const_spec(win.shape),
                _const_spec((1, MLA_Q_RANK)), _const_spec(wuq.shape),
                _const_spec((1, MLA_KV_RANK)), _const_spec(wuk.shape),
                _const_spec(wuvt.shape), _const_spec(wfvt.shape),
                vec, vec, vec, vec, tab, tab, vec, vec, vec,
                _const_spec((tm, tm)), _const_spec(place.shape)]
    if is_meta:
        out_specs, out_shape, scratch = [head_out, vt_out], [head_shape, vt_shape], []
    else:
        cum_end = pl.BlockSpec((1, 1, 8, LANES), lambda bi, i: (bi, i, 0, 0))
        out_specs = [head_out, head_out, vt_out, cum_end]
        out_shape = [head_shape, head_shape, vt_shape, jax.ShapeDtypeStruct((b, s // tm, 8, LANES), F32)]
        scratch = [pltpu.VMEM((8, LANES), F32)]
    return pl.pallas_call(
        functools.partial(_proj_kernel, tm=tm, is_meta=is_meta),
        grid=(b, s // tm),
        in_specs=in_specs,
        out_specs=out_specs,
        out_shape=out_shape,
        scratch_shapes=scratch,
        compiler_params=pltpu.CompilerParams(
            dimension_semantics=("arbitrary", "arbitrary"), vmem_limit_bytes=VMEM_LIMIT),
        name="proj_meta" if is_meta else "proj",
    )(h, gmix, win, gcq, wuq, gckv, wuk, wuvt, wfvt, gq, gqs, gk, gks, cc, ss, bf, gfq, gfk, tri, place)


def _attn_kernel(first_chunk_ref, q_ref, k_ref, vt_ref, km_ref, vmt_ref, o_ref, m_sc, acc_sc, *, tq, tk):
    i = pl.program_id(2)
    head0 = 2 * (pl.program_id(0) * pl.num_programs(1) + pl.program_id(1))
    vchunk = vt_ref.shape[-1]

    def values(hh, block, n_keys):
        n = n_keys // vchunk
        return jnp.concatenate([vt_ref[0, hh, block * n + c] for c in range(n)], axis=1)

    def scores(hh, kc, cols=slice(None)):
        return lax.dot_general(kc, q_ref[0, hh, cols, :], _NT, preferred_element_type=F32)

    def update(hh, s, vt, cols=slice(None)):
        m_old = m_sc[hh, :, cols]
        m_new = jnp.maximum(m_old, jnp.max(s, axis=0, keepdims=True))
        p = jnp.exp2(s - m_new).astype(BF16)
        pv = jnp.dot(vt, p, preferred_element_type=F32)
        acc_sc[hh, :, cols] = jnp.exp2(m_old - m_new) * acc_sc[hh, :, cols] + pv
        m_sc[hh, :, cols] = m_new

    m_sc[...] = jnp.full(m_sc.shape, NEG_INF, F32)
    acc_sc[...] = jnp.zeros(acc_sc.shape, F32)

    def chunk_body(heads):
        def body(j, carry):
            off = pl.multiple_of(j * tk, tk)
            s = [scores(hh, k_ref[0, hh, pl.ds(off, tk), :]) for hh in heads]
            for hh, s_hh in zip(heads, s):
                update(hh, s_hh, values(hh, j, tk))
            return carry
        return body

    first = [first_chunk_ref[(head0 + hh) * pl.num_programs(2) + i] for hh in range(2)]
    both = jnp.maximum(first[0], first[1])
    for hh in range(2):
        lax.fori_loop(first[hh], both, chunk_body((hh,)), 0)
    lax.fori_loop(both, i * (tq // tk), chunk_body((0, 1)), 0)

    off = pl.multiple_of(i * tq, tq)
    vt = [values(hh, i, tq) for hh in range(2)]

    def diag_scores(c):
        nk = (c + 1) * DIAG_COLS
        row = lax.broadcasted_iota(jnp.int32, (nk + BLOCK, DIAG_COLS), 0)
        col = lax.broadcasted_iota(jnp.int32, (nk + BLOCK, DIAG_COLS), 1)
        valid = (row <= col + c * DIAG_COLS) | (row >= nk + BLOCK - N_META)
        out = []
        for hh in range(2):
            keys = jnp.concatenate([k_ref[0, hh, pl.ds(off, nk), :], km_ref[0, hh]], axis=0)
            out.append(jnp.where(valid, scores(hh, keys, slice(c * DIAG_COLS, nk)), NEG_INF))
        return out

    n_blocks = tq // DIAG_COLS
    ahead = 2
    pend = {}
    for n in range(n_blocks + ahead):
        if n < n_blocks:
            pend[n] = diag_scores(n)
        if n >= ahead:
            c = n - ahead
            nk = (c + 1) * DIAG_COLS
            s = pend.pop(c)
            for hh in range(2):
                vals = jnp.concatenate([vt[hh][:, :nk], vmt_ref[0, hh, 0]], axis=1)
                update(hh, s[hh], vals, slice(c * DIAG_COLS, nk))

    a_e = acc_sc[0]
    a_o = acc_sc[1]
    top = lax.broadcasted_iota(jnp.int32, (LANES, tq), 0) < HALF
    out_t = jnp.where(top, a_e / a_e[HALF:HALF + 1, :], a_o / a_o[0:1, :])
    o_ref[0] = out_t.T.astype(BF16)


def _first_live_chunk(cum_end, g_q_fox, g_k_fox, s):
    b = cum_end.shape[0]
    per = s // cum_end.shape[1]
    cum = cum_end[:, :, 0, :FOX_HEADS]
    n_q, n_k, ratio = s // ATT_TQ, s // ATT_TK, ATT_TQ // ATT_TK
    chunk_end = cum[:, ATT_TK // per - 1::ATT_TK // per]
    tile_start = jnp.concatenate([jnp.zeros((b, 1, FOX_HEADS), F32),
                                  cum[:, ATT_TQ // per - 1::ATT_TQ // per][:, :n_q - 1]], axis=1)
    decay = LOG2E * (chunk_end[:, None, :, :] - tile_start[:, :, None, :])
    bound = (64.0 * (1.0 + 2.0 ** -9) ** 2 * (LOG2E / math.sqrt(FOX_DIM))
             * jnp.max(jnp.abs(g_q_fox)) * jnp.max(jnp.abs(g_k_fox)))
    dead = decay > 2.0 * bound + 150.0
    before_tile = jnp.arange(n_k)[None, :] < ratio * jnp.arange(n_q)[:, None]
    flags = (dead & before_tile[None, :, :, None]).astype(F32)
    upto = (jnp.arange(n_k)[:, None] <= jnp.arange(n_k)[None, :]).astype(F32)
    lead = jnp.einsum("bqjh,jk->bqkh", flags, upto) == (1.0 + jnp.arange(n_k, dtype=F32))[None, None, :, None]
    first = jnp.sum(lead, axis=2).astype(jnp.int32)
    first = jnp.concatenate([jnp.zeros((b, n_q, MLA_HEADS), jnp.int32), first], axis=2)
    return jnp.transpose(first, (0, 2, 1)).reshape(-1)


def _attention(q, k, vt, km, vmt, first_chunk):
    b, _, s, _ = q.shape
    tq, tk = ATT_TQ, ATT_TK
    vchunk = vt.shape[-1]
    assert tq % tk == 0 and s % tq == 0 and tk % vchunk == 0
    pairs = N_HEADS // 2
    grid_spec = pltpu.PrefetchScalarGridSpec(
        num_scalar_prefetch=1,
        grid=(b, pairs, s // tq),
        in_specs=[
            pl.BlockSpec((1, 2, tq, LANES), lambda bi, p, i, fc: (bi, p, i, 0)),
            pl.BlockSpec((1, 2, s, LANES), lambda bi, p, i, fc: (bi, p, 0, 0)),
            pl.BlockSpec((1, 2, s // vchunk, LANES, vchunk), lambda bi, p, i, fc: (bi, p, 0, 0, 0)),
            pl.BlockSpec((1, 2, BLOCK, LANES), lambda bi, p, i, fc: (0, p, 0, 0)),
            pl.BlockSpec((1, 2, 1, LANES, BLOCK), lambda bi, p, i, fc: (0, p, 0, 0, 0)),
        ],
        out_specs=pl.BlockSpec((1, tq, LANES), lambda bi, p, i, fc: (bi, i, p)),
        scratch_shapes=[pltpu.VMEM((2, 1, tq), F32), pltpu.VMEM((2, LANES, tq), F32)],
    )
    return pl.pallas_call(
        functools.partial(_attn_kernel, tq=tq, tk=tk),
        grid_spec=grid_spec,
        out_shape=jax.ShapeDtypeStruct((b, s, pairs * LANES), BF16),
        compiler_params=pltpu.CompilerParams(
            dimension_semantics=("arbitrary", "arbitrary", "arbitrary"),
            vmem_limit_bytes=VMEM_LIMIT),
        name="attention",
    )(first_chunk, q, k, vt, km, vmt)


def _pad_lanes(x, offset=0):
    n = x.shape[-1]
    pad = [(0, 0)] * (x.ndim - 1) + [(offset, LANES - offset - n)]
    return jnp.pad(x, pad)


def _swap_halves(x):
    half = MLA_ROPE // 2
    return jnp.concatenate([x[..., half:], x[..., :half]], axis=-1)


def _rope_lane_gains(g, scale):
    g = g.astype(F32) * scale
    direct = _pad_lanes(g)
    paired = _pad_lanes(_swap_halves(g[MLA_NOPE:]), MLA_NOPE)
    return direct[None], paired[None]


def _rope_tables(pos_hi, pos_lo):
    half = MLA_ROPE // 2
    inv_freq = 1.0 / (ROPE_THETA ** (jnp.arange(half, dtype=F32) / half))
    off = jnp.zeros((MLA_NOPE,), F32)
    freq = jnp.concatenate([off, inv_freq, inv_freq, jnp.zeros((LANES - MLA_QK,), F32)])
    sign = jnp.concatenate([off, -jnp.ones((half,), F32), jnp.ones((half,), F32),
                            jnp.zeros((LANES - MLA_QK,), F32)])
    ang_hi = pos_hi.astype(F32)[:, None] * freq[None, :]
    ang_lo = pos_lo.astype(F32)[:, None] * freq[None, :]
    c_hi, s_hi = jnp.cos(ang_hi)[:, None, :], jnp.sin(ang_hi)[:, None, :]
    c_lo, s_lo = jnp.cos(ang_lo)[None, :, :], jnp.sin(ang_lo)[None, :, :]
    cc = (c_hi * c_lo - s_hi * s_lo).reshape(-1, LANES)
    ss = ((s_hi * c_lo + c_hi * s_lo) * sign).reshape(-1, LANES)
    return cc, ss


def _layout_params(g_mix, w_in, g_cq, w_uq, g_ckv, w_ukv, g_q_mla, g_k_mla, b_forget,
                   g_q_fox, g_k_fox):
    o_kpe = MLA_Q_RANK + MLA_KV_RANK
    o_fox = o_kpe + MLA_ROPE
    o_fv = o_fox + 2 * FOX_HEADS * FOX_DIM
    o_fl = o_fv + FOX_HEADS * FOX_DIM
    w_kpe = w_in[:, o_kpe:o_fox]
    win = jnp.concatenate([
        w_in[:, :o_kpe],
        _pad_lanes(w_kpe, MLA_NOPE),
        _pad_lanes(_swap_halves(w_kpe), MLA_NOPE),
        w_in[:, o_fox:o_fv],
        _pad_lanes(w_in[:, o_fl:]),
    ], axis=1).astype(BF16)
    assert win.shape[1] == _D_IN_PAD
    wfvt = w_in[:, o_fv:o_fl].T.astype(BF16)

    uq = w_uq.reshape(MLA_Q_RANK, MLA_HEADS, MLA_QK)
    uq_direct = _pad_lanes(uq).reshape(MLA_Q_RANK, MLA_HEADS * LANES)
    uq_paired = _pad_lanes(_swap_halves(uq[..., MLA_NOPE:]), MLA_NOPE).reshape(MLA_Q_RANK, MLA_HEADS * LANES)
    wuq = jnp.concatenate([uq_direct, uq_paired], axis=1).astype(BF16)

    ukv = w_ukv.reshape(MLA_KV_RANK, MLA_HEADS, MLA_NOPE + MLA_V)
    wuk = _pad_lanes(ukv[..., :MLA_NOPE]).reshape(MLA_KV_RANK, MLA_HEADS * LANES).astype(BF16)
    wuvt = ukv[..., MLA_NOPE:].reshape(MLA_KV_RANK, MLA_HEADS * MLA_V).T.astype(BF16)

    gq, gqs = _rope_lane_gains(g_q_mla, LOG2E / math.sqrt(MLA_QK))
    gk, gks = _rope_lane_gains(g_k_mla, 1.0)
    gfq = jnp.tile(g_q_fox.astype(F32) * (LOG2E / math.sqrt(FOX_DIM)), 2)[None]
    gfk = jnp.tile(g_k_fox.astype(F32), 2)[None]
    bf = _pad_lanes(b_forget.astype(F32))[None]

    place = np.zeros((N_SPLIT * LANES, FOX_HEADS // 2 * LANES), np.float32)
    for h in range(FOX_HEADS):
        base = HALF if h % 2 == 0 else 0
        for j in range(N_SPLIT):
            place[j * LANES + h, (h // 2) * LANES + base + j] = 1.0
    place = jnp.asarray(place, BF16)

    return (g_mix[None].astype(F32), win, g_cq[None].astype(F32), wuq, g_ckv[None].astype(F32),
            wuk, wuvt, wfvt, gq, gqs, gk, gks, bf, gfq, gfk, place)


def kernel(x, meta_tokens, g_ffn1, w1_gate, w1_up, w1_down, g_mix, w_in, g_cq, w_uq, g_ckv, w_ukv,
           g_q_mla, g_k_mla, b_forget, g_q_fox, g_k_fox, w_out, g_ffn2, w2_gate, w2_up, w2_down):
    b, s, d = x.shape
    depth = g_ffn1.shape[0]
    assert depth == 1 and d == D_MODEL
    l = 0

    ffn1 = (g_ffn1[l][None], w1_gate[l], w1_up[l], w1_down[l])
    ffn2 = (g_ffn2[l][None], w2_gate[l], w2_up[l], w2_down[l])
    consts = _layout_params(g_mix[l], w_in[l], g_cq[l], w_uq[l], g_ckv[l], w_ukv[l], g_q_mla[l],
                            g_k_mla[l], b_forget[l], g_q_fox[l], g_k_fox[l])

    pad = BLOCK - N_META
    h_meta = jnp.concatenate([jnp.zeros((pad, d), F32), meta_tokens.astype(F32)], axis=0)
    tab_meta = _rope_tables(jnp.maximum(jnp.arange(BLOCK) - pad, 0), jnp.zeros((1,), jnp.int32))
    assert s % BLOCK == 0
    tab_tok = _rope_tables(N_META + BLOCK * jnp.arange(s // BLOCK), jnp.arange(BLOCK))

    h1, h_meta = _ffn_with_meta(x.reshape(b * s, d), h_meta, *ffn1)
    km, vmt = _proj(h_meta[None], consts, *tab_meta, is_meta=True)
    q, k, vt, cum_end = _proj(h1.reshape(b, s, d), consts, *tab_tok, is_meta=False)
    attn = _attention(q, k, vt, km, vmt, _first_live_chunk(cum_end, g_q_fox[l], g_k_fox[l], s))
    out = _mix_ffn(h1, attn.reshape(b * s, d), w_out[l], *ffn2)
    return out.reshape(b, s, d)
```
